```python
import jax, jax.numpy as jnp
from jax import lax
import numpy as np

D_MODEL = 1024
BATCH = 8
SEQ = 2048
DEPTH = 1
DEC_BATCH = 128
DEC_SEQ = 1
PAST_LEN = 16384
PAGE_SIZE = 128

RWKV_WIDTH = D_MODEL // 2
RWKV_HEAD = 64
RWKV_HEADS = RWKV_WIDTH // RWKV_HEAD
RWKV_DECAY_LORA = 64
RWKV_A_LORA = 64
RWKV_GATE_LORA = 128
RWKV_PROJ = 3 * RWKV_WIDTH + RWKV_DECAY_LORA + RWKV_A_LORA + RWKV_GATE_LORA
RWKV_SPLITS = (RWKV_WIDTH, 2 * RWKV_WIDTH, 3 * RWKV_WIDTH,
               3 * RWKV_WIDTH + RWKV_DECAY_LORA, 3 * RWKV_WIDTH + RWKV_DECAY_LORA + RWKV_A_LORA)
RWKV_GN_EPS = 64e-5
L2_EPS = 1e-12
HGRN_WIDTH = D_MODEL // 2
HGRN_HEADS = 4
HGRN_HEAD = HGRN_WIDTH // HGRN_HEADS
HGRN_PROJ = 4 * HGRN_WIDTH
HGRN_CHUNK = 64
GATE_PROJ = 2 * D_MODEL
IN_PROJ = RWKV_PROJ + HGRN_PROJ + GATE_PROJ
D_FF = ((8 * D_MODEL + 3 * 256 - 1) // (3 * 256)) * 256
RMS_EPS = 1e-6

kernel_name = 'rwkv7_hgrn2_gated_parallel_decoder_step'


def _rms_norm(x, g):
    xf = x.astype(jnp.float32)
    y = xf * lax.rsqrt(jnp.mean(xf * xf, axis=-1, keepdims=True) + RMS_EPS)
    return (y * g.astype(jnp.float32)).astype(x.dtype)


def _rwkv7_mixer(p, shift_prev, wkv0, mu, w0, w2, a0, a2, g2, k_k, k_a, r_k, ln_g, ln_b):
    B, T, _ = p.shape
    f32 = jnp.float32
    pf = p.astype(f32)
    p_prev = jnp.concatenate([shift_prev[:, None, :].astype(f32), pf[:, :-1]], axis=1)
    ps = pf + mu.astype(f32) * (p_prev - pf)
    r, k, v, wd, ad, gd = jnp.split(ps, list(RWKV_SPLITS), axis=-1)
    w = -jax.nn.softplus(-(w0 + jnp.tanh(wd) @ w2)) - 0.5
    decay = jnp.exp(-jnp.exp(w))
    a = jax.nn.sigmoid(a0 + ad @ a2)
    g = jax.nn.sigmoid(gd) @ g2

    def heads(t):
        return t.reshape(B, T, RWKV_HEADS, RWKV_HEAD)

    kk = heads(k * k_k)
    kk = kk / jnp.maximum(jnp.linalg.norm(kk, axis=-1, keepdims=True), L2_EPS)
    k = k * (1.0 + (a - 1.0) * k_a)
    r_h, k_h, v_h, w_h, a_h = heads(r), heads(k), heads(v), heads(decay), heads(a)
    b_h = kk * a_h

    def step(S, inp):
        r_t, w_t, k_t, v_t, aa_t, bb_t = inp
        sa = jnp.einsum('bhvk,bhk->bhv', S, aa_t)
        S = (S * w_t[:, :, None, :] + sa[..., None] * bb_t[:, :, None, :]
             + v_t[..., None] * k_t[:, :, None, :])
        y = jnp.einsum('bhvk,bhk->bhv', S, r_t)
        return S, y

    seq = tuple(jnp.moveaxis(t, 1, 0) for t in (r_h, w_h, k_h, v_h, -kk, b_h))
    S_T, y = lax.scan(step, wkv0.astype(f32), seq)
    y = jnp.moveaxis(y, 0, 1)
    mean = jnp.mean(y, axis=-1, keepdims=True)
    var = jnp.mean(jnp.square(y - mean), axis=-1, keepdims=True)
    y = ((y - mean) * lax.rsqrt(var + RWKV_GN_EPS)).reshape(B, T, RWKV_WIDTH) * ln_g + ln_b
    bonus = jnp.sum(r_h * k_h * r_k, axis=-1, keepdims=True) * v_h
    out = (y + bonus.reshape(B, T, RWKV_WIDTH)) * g
    return out.astype(p.dtype), S_T.astype(wkv0.dtype), p[:, -1]


def _gla_chunkwise(q, k, v, log_f, S0):
    B, T, H, K = q.shape
    V = v.shape[-1]
    C = min(HGRN_CHUNK, T)
    nc = -(-T // C)
    pad = nc * C - T

    def blocks(t):
        t = jnp.pad(t, ((0, 0), (0, pad), (0, 0), (0, 0)))
        return t.reshape(B, nc, C, H, t.shape[-1]).transpose(1, 0, 3, 2, 4)

    causal = jnp.tril(jnp.ones((C, C), dtype=bool))

    def step(S, inp):
        qc, kc, vc, gc = inp
        b = jnp.cumsum(gc, axis=2)
        inter = jnp.einsum('bhtk,bhkv->bhtv', qc * jnp.exp(b), S)
        diff = jnp.where(causal[:, :, None], b[:, :, :, None, :] - b[:, :, None, :, :], -jnp.inf)
        scores = jnp.einsum('bhtk,bhtsk,bhsk->bhts', qc, jnp.exp(diff), kc)
        intra = jnp.einsum('bhts,bhsv->bhtv', scores, vc)
        b_last = b[:, :, -1:, :]
        S = (jnp.exp(b_last[:, :, 0, :])[..., None] * S
             + jnp.einsum('bhsk,bhsv->bhkv', kc * jnp.exp(b_last - b), vc))
        return S, inter + intra

    S_T, o = lax.scan(step, S0, tuple(blocks(t) for t in (q, k, v, log_f)))
    o = o.transpose(1, 0, 3, 2, 4).reshape(B, nc * C, H, V)[:, :T]
    return o, S_T


def _hgrn2_mixer(p, S0, lb, norm_g):
    B, T, _ = p.shape
    f32 = jnp.float32
    q, f_logit, i, og = jnp.split(p.astype(f32), 4, axis=-1)
    f = lb + (1.0 - lb) * jax.nn.sigmoid(f_logit)
    log_f = jnp.log(f)
    k = 1.0 - f

    def heads(t):
        return t.reshape(B, T, HGRN_HEADS, HGRN_HEAD)

    o, S_T = _gla_chunkwise(heads(jax.nn.silu(q)), heads(k), heads(i), heads(log_f), S0.astype(f32))
    o = o * lax.rsqrt(jnp.mean(o * o, axis=-1, keepdims=True) + RMS_EPS)
    o = o.reshape(B, T, HGRN_WIDTH) * norm_g * jax.nn.sigmoid(og)
    return o.astype(p.dtype), S_T.astype(S0.dtype)


def _trunk(x, wkv0, shift0, hgrn0, prm):
    lb_all = jnp.cumsum(jax.nn.softmax(prm['hgrn_lb'].astype(jnp.float32), axis=0), axis=0)
    new_wkv, new_shift, new_hgrn = [], [], []
    for l in range(DEPTH):
        h = _rms_norm(x, prm['norm_mix_g'][l])
        proj = h @ prm['w_in'][l]
        p_rwkv, p_hgrn, p_gate = jnp.split(proj, [RWKV_PROJ, RWKV_PROJ + HGRN_PROJ], axis=-1)
        o_a, wkv_l, shift_l = _rwkv7_mixer(
            p_rwkv, shift0[l], wkv0[l], prm['rwkv_mu'][l], prm['rwkv_w0'][l], prm['rwkv_w2'][l],
            prm['rwkv_a0'][l], prm['rwkv_a2'][l], prm['rwkv_g2'][l], prm['rwkv_k_k'][l],
            prm['rwkv_k_a'][l], prm['rwkv_r_k'][l], prm['rwkv_ln_g'][l], prm['rwkv_ln_b'][l])
        o_b, hgrn_l = _hgrn2_mixer(p_hgrn, hgrn0[l], lb_all[l], prm['hgrn_norm_g'][l])
        gate_a, gate_b = jnp.split(jax.nn.sigmoid(p_gate), 2, axis=-1)
        merged = gate_a * (o_a @ prm['w_up_a'][l]) + gate_b * (o_b @ prm['w_up_b'][l])
        x = x + merged @ prm['w_out'][l]
        h = _rms_norm(x, prm['norm_ffn_g'][l])
        x = x + (jax.nn.silu(h @ prm['w_ffn_gate'][l]) * (h @ prm['w_ffn_up'][l])) @ prm['w_ffn_down'][l]
        new_wkv.append(wkv_l)
        new_shift.append(shift_l)
        new_hgrn.append(hgrn_l)
    y = _rms_norm(x, prm['norm_final_g'])
    return y, jnp.stack(new_wkv), jnp.stack(new_shift), jnp.stack(new_hgrn)


def setup_inputs(seed: int = 0) -> dict:
    key = jax.random.key(seed)
    ks = jax.random.split(key, 32)
    f32 = jnp.float32

    def nrm(k, shape, scale):
        return jax.random.normal(k, shape, f32) * scale

    L = DEPTH
    return {
        'x_prompt': nrm(ks[0], (BATCH, SEQ, D_MODEL), 1.0),
        'x_sample': nrm(ks[1], (DEC_BATCH, DEC_SEQ, D_MODEL), 1.0),
        'state_rwkv_wkv': nrm(ks[2], (L, DEC_BATCH, RWKV_HEADS, RWKV_HEAD, RWKV_HEAD), 0.5),
        'state_rwkv_shift': nrm(ks[3], (L, DEC_BATCH, RWKV_PROJ), 1.0),
        'state_hgrn': nrm(ks[4], (L, DEC_BATCH, HGRN_HEADS, HGRN_HEAD, HGRN_HEAD), 0.5),
        'norm_mix_g': 1.0 + nrm(ks[5], (L, D_MODEL), 0.02),
        'w_in': nrm(ks[6], (L, D_MODEL, IN_PROJ), D_MODEL ** -0.5),
        'rwkv_mu': jax.random.uniform(ks[7], (L, RWKV_PROJ), f32),
        'rwkv_w0': -1.0 + nrm(ks[8], (L, RWKV_WIDTH), 0.3),
        'rwkv_w2': nrm(ks[9], (L, RWKV_DECAY_LORA, RWKV_WIDTH), 0.1),
        'rwkv_a0': nrm(ks[10], (L, RWKV_WIDTH), 0.1),
        'rwkv_a2': nrm(ks[11], (L, RWKV_A_LORA, RWKV_WIDTH), RWKV_A_LORA ** -0.5),
        'rwkv_g2': nrm(ks[12], (L, RWKV_GATE_LORA, RWKV_WIDTH), RWKV_GATE_LORA ** -0.5),
        'rwkv_k_k': 0.85 + nrm(ks[13], (L, RWKV_WIDTH), 0.02),
        'rwkv_k_a': 1.0 + nrm(ks[14], (L, RWKV_WIDTH), 0.02),
        'rwkv_r_k': nrm(ks[15], (L, RWKV_HEADS, RWKV_HEAD), 0.1),
        'rwkv_ln_g': 1.0 + nrm(ks[16], (L, RWKV_WIDTH), 0.02),
        'rwkv_ln_b': nrm(ks[17], (L, RWKV_WIDTH), 0.02),
        'w_up_a': nrm(ks[18], (L, RWKV_WIDTH, D_MODEL), RWKV_WIDTH ** -0.5),
        'hgrn_lb': nrm(ks[19], (L + 1, HGRN_WIDTH), 0.1),
        'hgrn_norm_g': 1.0 + nrm(ks[20], (L, HGRN_WIDTH), 0.02),
        'w_up_b': nrm(ks[21], (L, HGRN_WIDTH, D_MODEL), HGRN_WIDTH ** -0.5),
        'w_out': nrm(ks[22], (L, D_MODEL, D_MODEL), D_MODEL ** -0.5),
        'norm_ffn_g': 1.0 + nrm(ks[23], (L, D_MODEL), 0.02),
        'w_ffn_gate': nrm(ks[24], (L, D_MODEL, D_FF), D_MODEL ** -0.5),
        'w_ffn_up': nrm(ks[25], (L, D_MODEL, D_FF), D_MODEL ** -0.5),
        'w_ffn_down': nrm(ks[26], (L, D_FF, D_MODEL), D_FF ** -0.5),
        'norm_final_g': 1.0 + nrm(ks[27], (D_MODEL,), 0.02),
    }


def reference(x_prompt, x_sample, state_rwkv_wkv, state_rwkv_shift, state_hgrn,
              norm_mix_g, w_in, rwkv_mu, rwkv_w0, rwkv_w2, rwkv_a0, rwkv_a2, rwkv_g2,
              rwkv_k_k, rwkv_k_a, rwkv_r_k, rwkv_ln_g, rwkv_ln_b, w_up_a, hgrn_lb,
              hgrn_norm_g, w_up_b, w_out, norm_ffn_g, w_ffn_gate, w_ffn_up, w_ffn_down,
              norm_final_g):
    prm = dict(norm_mix_g=norm_mix_g, w_in=w_in, rwkv_mu=rwkv_mu, rwkv_w0=rwkv_w0,
               rwkv_w2=rwkv_w2, rwkv_a0=rwkv_a0, rwkv_a2=rwkv_a2, rwkv_g2=rwkv_g2,
               rwkv_k_k=rwkv_k_k, rwkv_k_a=rwkv_k_a, rwkv_r_k=rwkv_r_k, rwkv_ln_g=rwkv_ln_g,
               rwkv_ln_b=rwkv_ln_b, w_up_a=w_up_a, hgrn_lb=hgrn_lb, hgrn_norm_g=hgrn_norm_g,
               w_up_b=w_up_b, w_out=w_out, norm_ffn_g=norm_ffn_g, w_ffn_gate=w_ffn_gate,
               w_ffn_up=w_ffn_up, w_ffn_down=w_ffn_down, norm_final_g=norm_final_g)
    bp = x_prompt.shape[0]
    dt = x_prompt.dtype
    wkv_zero = jnp.zeros((DEPTH, bp, RWKV_HEADS, RWKV_HEAD, RWKV_HEAD), dt)
    shift_zero = jnp.zeros((DEPTH, bp, RWKV_PROJ), dt)
    hgrn_zero = jnp.zeros((DEPTH, bp, HGRN_HEADS, HGRN_HEAD, HGRN_HEAD), dt)
    y_prompt, wkv_p, shift_p, hgrn_p = _trunk(x_prompt, wkv_zero, shift_zero, hgrn_zero, prm)
    y_sample, wkv_s, shift_s, hgrn_s = _trunk(x_sample, state_rwkv_wkv, state_rwkv_shift, state_hgrn, prm)
    return (y_prompt, y_sample, wkv_p, shift_p, hgrn_p, wkv_s, shift_s, hgrn_s)
```

```python
import functools

import jax
import jax.numpy as jnp
from jax import lax
from jax.experimental import pallas as pl
from jax.experimental.pallas import tpu as pltpu

F32 = jnp.float32
BF16 = jnp.bfloat16

D_MODEL = 1024
RWKV_WIDTH = 512
RWKV_HEAD = 64
RWKV_HEADS = 8
RWKV_DECAY_LORA = 64
RWKV_A_LORA = 64
RWKV_GATE_LORA = 128
RWKV_PROJ = 3 * RWKV_WIDTH + RWKV_DECAY_LORA + RWKV_A_LORA + RWKV_GATE_LORA
RWKV_GN_EPS = 64e-5
L2_EPS = 1e-12
HGRN_WIDTH = 512
HGRN_HEADS = 4
HGRN_HEAD = 128
HGRN_PROJ = 4 * HGRN_WIDTH
GATE_PROJ = 2 * D_MODEL
D_FF = 2816
RMS_EPS = 1e-6

CHUNK = 64
HGRN_SUB = 16
EXP_CLAMP = 80.0
VMEM_LIMIT = 56 * 1024 * 1024


def _dot(a, b):
    return jnp.dot(a.astype(BF16), b.astype(BF16), preferred_element_type=F32)


def _dot_nt(a, b):
    return lax.dot_general(a.astype(BF16), b.astype(BF16), (((1,), (1,)), ((), ())),
                           preferred_element_type=F32)


def _dot_tn(a, b):
    return lax.dot_general(a.astype(BF16), b.astype(BF16), (((0,), (0,)), ((), ())),
                           preferred_element_type=F32)


def _split2(x):
    hi = x.astype(BF16)
    lo = (x - hi.astype(F32)).astype(BF16)
    return hi, lo


def _split3(x):
    hi = x.astype(BF16)
    r1 = x - hi.astype(F32)
    mid = r1.astype(BF16)
    lo = (r1 - mid.astype(F32)).astype(BF16)
    return hi, mid, lo


def _dot_x3(a, b):
    ah, al = _split2(a)
    bh, bl = _split2(b)
    return _dot(ah, bh) + (_dot(ah, bl) + _dot(al, bh))


def _dot_exact_lhs(a_bf16, b):
    bh, bm, bl = _split3(b)
    return _dot(a_bf16, bh) + (_dot(a_bf16, bm) + _dot(a_bf16, bl))


def _dot_exact_rhs(a, b_bf16):
    ah, am, al = _split3(a)
    return _dot(ah, b_bf16) + (_dot(am, b_bf16) + _dot(al, b_bf16))


def _sigmoid(x):
    return 1.0 / (1.0 + jnp.exp(-x))


def _rms_norm(x, g):
    return x * lax.rsqrt(jnp.mean(x * x, axis=-1, keepdims=True) + RMS_EPS) * g


def _const_spec(shape):
    nd = len(shape)
    return pl.BlockSpec(shape, lambda *_: (0,) * nd, pipeline_mode=pl.Buffered(1))


def _params(n_grid):
    return pltpu.CompilerParams(dimension_semantics=("arbitrary",) * n_grid,
                                vmem_limit_bytes=VMEM_LIMIT)


def _norm_proj_kernel(x_ref, g_ref, wa_ref, wb_ref, wc_ref, oa_ref, ob_ref, oc_ref):
    h = _rms_norm(x_ref[...], g_ref[...]).astype(BF16)
    oa_ref[...] = jnp.dot(h, wa_ref[...], preferred_element_type=F32)
    ob_ref[...] = jnp.dot(h, wb_ref[...], preferred_element_type=F32)
    oc_ref[...] = jnp.dot(h, wc_ref[...], preferred_element_type=F32)


def _norm_proj(x, g, wa, wb, wc, tm):
    m = x.shape[0]
    row = lambda n: pl.BlockSpec((tm, n), lambda i: (i, 0))
    return pl.pallas_call(
        _norm_proj_kernel,
        grid=(m // tm,),
        in_specs=[row(D_MODEL), _const_spec((1, D_MODEL)), _const_spec(wa.shape),
                  _const_spec(wb.shape), _const_spec(wc.shape)],
        out_specs=[row(RWKV_PROJ), row(HGRN_PROJ), row(GATE_PROJ)],
        out_shape=[jax.ShapeDtypeStruct((m, RWKV_PROJ), F32),
                   jax.ShapeDtypeStruct((m, HGRN_PROJ), F32),
                   jax.ShapeDtypeStruct((m, GATE_PROJ), F32)],
        compiler_params=_params(1),
        name="norm_proj",
    )(x, g, wa, wb, wc)


def _head_sum(x, ones_bd):
    hi, lo = _split2(x)
    return _dot(hi, ones_bd) + _dot(lo, ones_bd)


def _rwkv_prep_kernel(p_ref, pp_ref, mu_ref, w0_ref, w2_ref, a0_ref, a2_ref, g2_ref, kk_ref,
                      ka_ref, rk_ref, ones_ref,
                      r_out, lw_out, k_out, v_out, kk_out, a_out, g_out, bonus_out):
    W = RWKV_WIDTH
    p = p_ref[...]
    ps = p + mu_ref[...] * (pp_ref[...] - p)
    r = ps[:, 0:W]
    k = ps[:, W:2 * W]
    v = ps[:, 2 * W:3 * W]
    o0 = 3 * W
    wd = ps[:, o0:o0 + RWKV_DECAY_LORA]
    ad = ps[:, o0 + RWKV_DECAY_LORA:o0 + RWKV_DECAY_LORA + RWKV_A_LORA]
    gd = ps[:, o0 + RWKV_DECAY_LORA + RWKV_A_LORA:]
    z = -(w0_ref[...] + _dot_x3(jnp.tanh(wd), w2_ref[...]))
    softplus = jnp.maximum(z, 0.0) + jnp.log(1.0 + jnp.exp(-jnp.abs(z)))
    lw = -jnp.exp(-softplus - 0.5)
    a = _sigmoid(a0_ref[...] + _dot_x3(ad, a2_ref[...]))
    g = _dot_x3(_sigmoid(gd), g2_ref[...])
    ones_bd = ones_ref[...]
    kk = k * kk_ref[...]
    norm = jnp.sqrt(_head_sum(kk * kk, ones_bd))
    kk = kk / jnp.maximum(norm, L2_EPS)
    k = k * (1.0 + (a - 1.0) * ka_ref[...])
    bonus = _head_sum(r * k * rk_ref[...], ones_bd) * v
    r_out[...] = r
    lw_out[...] = lw
    k_out[...] = k
    v_out[...] = v
    kk_out[...] = kk
    a_out[...] = a
    g_out[...] = g
    bonus_out[...] = bonus


def _rwkv_prep(p, pprev, mu, w0, w2, a0, a2, g2, k_k, k_a, r_k, ones_bd, tm):
    m = p.shape[0]
    row = lambda n: pl.BlockSpec((tm, n), lambda i: (i, 0))
    consts = [mu, w0, w2, a0, a2, g2, k_k, k_a, r_k, ones_bd]
    return pl.pallas_call(
        _rwkv_prep_kernel,
        grid=(m // tm,),
        in_specs=[row(RWKV_PROJ), row(RWKV_PROJ)] + [_const_spec(c.shape) for c in consts],
        out_specs=[row(RWKV_WIDTH)] * 8,
        out_shape=[jax.ShapeDtypeStruct((m, RWKV_WIDTH), F32)] * 8,
        compiler_params=_params(1),
        name="rwkv_prep",
    )(p, pprev, *consts)


def _rwkv_chunk_kernel(r_ref, lw_ref, k_ref, v_ref, kk_ref, a_ref, s0_ref, y_ref, sT_ref, s_scr,
                       *, nch):
    C = CHUNK
    tb = pl.program_id(1)

    @pl.when(tb == 0)
    def _():
        s_scr[...] = s0_ref[0]

    row = lax.broadcasted_iota(jnp.int32, (C, C), 0)
    col = lax.broadcasted_iota(jnp.int32, (C, C), 1)
    tril_incl = (col <= row).astype(BF16)
    eye = (col == row).astype(F32)
    row2 = lax.broadcasted_iota(jnp.int32, (2 * C, 2 * C), 0)
    col2 = lax.broadcasted_iota(jnp.int32, (2 * C, 2 * C), 1)
    t_idx = row2 & (C - 1)
    s_idx = col2 & (C - 1)
    keep = s_idx < t_idx + jnp.where(row2 < C, 0, 1)

    def chunk(c, carry):
        sl = pl.ds(pl.multiple_of(c * C, C), C)
        r = r_ref[0, sl, :]
        lw = lw_ref[0, sl, :]
        k = k_ref[0, sl, :]
        v = v_ref[0, sl, :]
        kk = kk_ref[0, sl, :]
        kb = kk * a_ref[0, sl, :]
        cum = _dot_exact_lhs(tril_incl, lw)
        cl = cum[C - 1:C, :]
        eneg = jnp.exp(-cum)
        elast = jnp.exp(cl - cum)
        glast = jnp.exp(cl)
        at = -kk * jnp.exp(cum - lw)
        rt = r * jnp.exp(cum)
        bt = kb * eneg
        kt = k * eneg
        bh = kb * elast
        kh = k * elast
        for h in range(RWKV_HEADS):
            hs = slice(h * RWKV_HEAD, (h + 1) * RWKV_HEAD)
            S = s_scr[h]
            G = jnp.concatenate([at[:, hs], rt[:, hs]], axis=0)
            Kb = jnp.concatenate([bt[:, hs], kt[:, hs]], axis=0)
            P = jnp.where(keep, _dot_nt(G, Kb), 0.0)
            A_ab = P[:C, :C]
            A_ak = P[:C, C:]
            A_r = P[C:, :]
            X = A_ab
            Tm = eye + X
            for _ in range(5):
                X = _dot_x3(X, X)
                Tm = Tm + _dot_x3(X, Tm)
            vh = v[:, hs]
            X2 = _dot_nt(G, S)
            U = _dot_x3(Tm, X2[:C] + _dot(A_ak, vh))
            UV = jnp.concatenate([U, vh], axis=0)
            y_ref[0, sl, hs] = X2[C:] + _dot(A_r, UV)
            BK = jnp.concatenate([bh[:, hs], kh[:, hs]], axis=0)
            s_scr[h] = S * glast[:, hs] + _dot_tn(UV, BK)
        return carry

    lax.fori_loop(0, nch, chunk, 0)

    @pl.when(tb == pl.num_programs(1) - 1)
    def _():
        sT_ref[0] = s_scr[...]


def _rwkv_chunk(r, lw, k, v, kk, a, s0, tt):
    b, t, w = r.shape
    seq = pl.BlockSpec((1, tt, w), lambda i, j: (i, j, 0))
    st = pl.BlockSpec((1, RWKV_HEADS, RWKV_HEAD, RWKV_HEAD), lambda i, j: (i, 0, 0, 0))
    return pl.pallas_call(
        functools.partial(_rwkv_chunk_kernel, nch=tt // CHUNK),
        grid=(b, t // tt),
        in_specs=[seq] * 6 + [st],
        out_specs=[seq, st],
        out_shape=[jax.ShapeDtypeStruct((b, t, w), F32), jax.ShapeDtypeStruct(s0.shape, F32)],
        scratch_shapes=[pltpu.VMEM((RWKV_HEADS, RWKV_HEAD, RWKV_HEAD), F32)],
        compiler_params=_params(2),
        name="rwkv_chunk",
    )(r, lw, k, v, kk, a, s0)


def _rwkv_step_kernel(r_ref, lw_ref, k_ref, kk_ref, a_ref, v_ref, s_ref, y_ref, so_ref):
    H = s_ref[0]
    kk = kk_ref[0]
    sa = jnp.sum(H * (-kk), axis=0)
    Hn = H * jnp.exp(lw_ref[0]) + (kk * a_ref[0]) * sa[None] + k_ref[0] * v_ref[...][None]
    so_ref[0] = Hn
    y_ref[...] = jnp.sum(Hn * r_ref[0], axis=0)


def _rwkv_step(r, lw, k, kk, a, v, s):
    nb = s.shape[-1]
    kvec = pl.BlockSpec((1, RWKV_HEAD, 1, nb), lambda h: (h, 0, 0, 0))
    vvec = pl.BlockSpec((RWKV_HEAD, nb), lambda h: (h, 0))
    st = pl.BlockSpec((1, RWKV_HEAD, RWKV_HEAD, nb), lambda h: (h, 0, 0, 0))
    return pl.pallas_call(
        _rwkv_step_kernel,
        grid=(RWKV_HEADS,),
        in_specs=[kvec] * 5 + [vvec, st],
        out_specs=[vvec, st],
        out_shape=[jax.ShapeDtypeStruct((RWKV_WIDTH, nb), F32), jax.ShapeDtypeStruct(s.shape, F32)],
        compiler_params=_params(1),
        name="rwkv_step",
    )(r, lw, k, kk, a, v, s)


def _hgrn_chunk_kernel(p_ref, lb_ref, ng_ref, s0_ref, o_ref, sT_ref, s_scr, *, nch):
    C = CHUNK
    SB = HGRN_SUB
    W = HGRN_WIDTH
    tb = pl.program_id(1)

    @pl.when(tb == 0)
    def _():
        s_scr[...] = s0_ref[0]

    row = lax.broadcasted_iota(jnp.int32, (C, C), 0)
    col = lax.broadcasted_iota(jnp.int32, (C, C), 1)
    tril_incl = (col <= row).astype(BF16)
    lb = lb_ref[...]
    ng = ng_ref[...]

    def chunk(c, carry):
        sl = pl.ds(pl.multiple_of(c * C, C), C)
        q = p_ref[0, sl, 0:W]
        q = q * _sigmoid(q)
        f = lb + (1.0 - lb) * _sigmoid(p_ref[0, sl, W:2 * W])
        kx = 1.0 - f
        iv = p_ref[0, sl, 2 * W:3 * W]
        og = p_ref[0, sl, 3 * W:4 * W]
        bc = _dot_exact_lhs(tril_incl, jnp.log(f))
        bl = bc[C - 1:C, :]
        qe = q * jnp.exp(bc)
        kend = kx * jnp.exp(bl - bc)
        glast = jnp.exp(bl)
        for h in range(HGRN_HEADS):
            hs = slice(h * HGRN_HEAD, (h + 1) * HGRN_HEAD)
            St = s_scr[h]
            bch = bc[:, hs]
            qh = q[:, hs]
            kh = kx[:, hs]
            ih = iv[:, hs]
            parts = []
            for i in range(C // SB):
                lo, n = i * SB, (i + 1) * SB
                if i == 0:
                    qi = qh[:n] * jnp.exp(bch[:n])
                    ks = kh[:n] * jnp.exp(jnp.minimum(-bch[:n], EXP_CLAMP))
                else:
                    beta = bch[lo - 1:lo]
                    qi = qh[lo:n] * jnp.exp(bch[lo:n] - beta)
                    ks = kh[:n] * jnp.exp(jnp.minimum(beta - bch[:n], EXP_CLAMP))
                sc = _dot_nt(qi, ks)
                rowi = lax.broadcasted_iota(jnp.int32, (SB, n), 0) + lo
                coli = lax.broadcasted_iota(jnp.int32, (SB, n), 1)
                parts.append(_dot(jnp.where(coli <= rowi, sc, 0.0), ih[:n]))
            o = _dot_nt(qe[:, hs], St) + jnp.concatenate(parts, axis=0)
            o = o * lax.rsqrt(jnp.mean(o * o, axis=-1, keepdims=True) + RMS_EPS)
            o_ref[0, sl, hs] = o * ng[:, hs] * _sigmoid(og[:, hs])
            s_scr[h] = St * glast[:, hs] + _dot_tn(ih, kend[:, hs])
        return carry

    lax.fori_loop(0, nch, chunk, 0)

    @pl.when(tb == pl.num_programs(1) - 1)
    def _():
        sT_ref[0] = s_scr[...]


def _hgrn_chunk(p, lb, ng, s0, tt):
    b, t, _ = p.shape
    st = pl.BlockSpec((1, HGRN_HEADS, HGRN_HEAD, HGRN_HEAD), lambda i, j: (i, 0, 0, 0))
    return pl.pallas_call(
        functools.partial(_hgrn_chunk_kernel, nch=tt // CHUNK),
        grid=(b, t // tt),
        in_specs=[pl.BlockSpec((1, tt, HGRN_PROJ), lambda i, j: (i, j, 0)),
                  _const_spec((1, HGRN_WIDTH)), _const_spec((1, HGRN_WIDTH)), st],
        out_specs=[pl.BlockSpec((1, tt, HGRN_WIDTH), lambda i, j: (i, j, 0)), st],
        out_shape=[jax.ShapeDtypeStruct((b, t, HGRN_WIDTH), F32),
                   jax.ShapeDtypeStruct(s0.shape, F32)],
        scratch_shapes=[pltpu.VMEM((HGRN_HEADS, HGRN_HEAD, HGRN_HEAD), F32)],
        compiler_params=_params(2),
        name="hgrn_chunk",
    )(p, lb, ng, s0)


def _hgrn_step_kernel(q_ref, fl_ref, i_ref, og_ref, lb_ref, ng_ref, s_ref, o_ref, so_ref):
    q = q_ref[0]
    q = q * _sigmoid(q)
    lb = lb_ref[0]
    f = lb + (1.0 - lb) * _sigmoid(fl_ref[0])
    Sn = s_ref[0] * f + (1.0 - f) * i_ref[...][None]
    so_ref[0] = Sn
    o = jnp.sum(Sn * q, axis=0)
    o = o * lax.rsqrt(jnp.mean(o * o, axis=0, keepdims=True) + RMS_EPS)
    o_ref[...] = o * ng_ref[...] * _sigmoid(og_ref[...])


def _hgrn_step(q, fl, iv, og, lb, ng, s):
    nb = s.shape[-1]
    kvec = pl.BlockSpec((1, HGRN_HEAD, 1, nb), lambda h: (h, 0, 0, 0))
    vvec = pl.BlockSpec((HGRN_HEAD, nb), lambda h: (h, 0))
    st = pl.BlockSpec((1, HGRN_HEAD, HGRN_HEAD, nb), lambda h: (h, 0, 0, 0))
    return pl.pallas_call(
        _hgrn_step_kernel,
        grid=(HGRN_HEADS,),
        in_specs=[kvec, kvec, vvec, vvec,
                  pl.BlockSpec((1, HGRN_HEAD, 1, 1), lambda h: (h, 0, 0, 0)),
                  pl.BlockSpec((HGRN_HEAD, 1), lambda h: (h, 0)), st],
        out_specs=[vvec, st],
        out_shape=[jax.ShapeDtypeStruct((HGRN_WIDTH, nb), F32), jax.ShapeDtypeStruct(s.shape, F32)],
        compiler_params=_params(1),
        name="hgrn_step",
    )(q, fl, iv, og, lb, ng, s)


def _merge_ffn_kernel(x_ref, y_ref, bonus_ref, g_ref, ob_ref, pg_ref, lng_ref, lnb_ref, ones_ref,
                      wa_ref, wb_ref, wo_ref, nf_ref, wg_ref, wu_ref, wd_ref, nfin_ref, out_ref):
    ones_bd = ones_ref[...]
    y = y_ref[...]
    inv_n = 1.0 / RWKV_HEAD
    d = y - _head_sum(y, ones_bd) * inv_n
    var = _head_sum(d * d, ones_bd) * inv_n
    o_a = (d * lax.rsqrt(var + RWKV_GN_EPS) * lng_ref[...] + lnb_ref[...] + bonus_ref[...]) * g_ref[...]
    pg = pg_ref[...]
    merged = (_sigmoid(pg[:, :D_MODEL]) * _dot(o_a, wa_ref[...])
              + _sigmoid(pg[:, D_MODEL:]) * _dot(ob_ref[...], wb_ref[...]))
    x = x_ref[...] + _dot(merged, wo_ref[...])
    h = _rms_norm(x, nf_ref[...]).astype(BF16)
    gate = jnp.dot(h, wg_ref[...], preferred_element_type=F32)
    up = jnp.dot(h, wu_ref[...], preferred_element_type=F32)
    x = x + _dot(gate * _sigmoid(gate) * up, wd_ref[...])
    out_ref[...] = _rms_norm(x, nfin_ref[...])


def _merge_ffn(x, y, bonus, g, ob, pg, consts, tm):
    m = x.shape[0]
    row = lambda n: pl.BlockSpec((tm, n), lambda i: (i, 0))
    return pl.pallas_call(
        _merge_ffn_kernel,
        grid=(m // tm,),
        in_specs=[row(D_MODEL), row(RWKV_WIDTH), row(RWKV_WIDTH), row(RWKV_WIDTH), row(HGRN_WIDTH),
                  row(GATE_PROJ)] + [_const_spec(c.shape) for c in consts],
        out_specs=row(D_MODEL),
        out_shape=jax.ShapeDtypeStruct((m, D_MODEL), F32),
        compiler_params=_params(1),
        name="merge_ffn",
    )(x, y, bonus, g, ob, pg, *consts)


def kernel(x_prompt, x_sample, state_rwkv_wkv, state_rwkv_shift, state_hgrn, norm_mix_g, w_in, rwkv_mu, rwkv_w0, rwkv_w2, rwkv_a0, rwkv_a2, rwkv_g2, rwkv_k_k, rwkv_k_a, rwkv_r_k, rwkv_ln_g, rwkv_ln_b, w_up_a, hgrn_lb, hgrn_norm_g, w_up_b, w_out, norm_ffn_g, w_ffn_gate, w_ffn_up, w_ffn_down, norm_final_g):
    bp, tp, _ = x_prompt.shape
    bs = x_sample.shape[0]
    mp = bp * tp

    w_in0 = w_in[0]
    wa_in = w_in0[:, :RWKV_PROJ].astype(BF16)
    wb_in = w_in0[:, RWKV_PROJ:RWKV_PROJ + HGRN_PROJ].astype(BF16)
    wc_in = w_in0[:, RWKV_PROJ + HGRN_PROJ:].astype(BF16)
    vec = lambda p: p.reshape(1, -1).astype(F32)
    head_id = jnp.arange(RWKV_WIDTH, dtype=jnp.int32) // RWKV_HEAD
    ones_bd = (head_id[:, None] == head_id[None, :]).astype(BF16)
    prep_consts = (vec(rwkv_mu[0]), vec(rwkv_w0[0]), rwkv_w2[0], vec(rwkv_a0[0]), rwkv_a2[0],
                   rwkv_g2[0], vec(rwkv_k_k[0]), vec(rwkv_k_a[0]), vec(rwkv_r_k[0]), ones_bd)
    lb = jnp.cumsum(jax.nn.softmax(hgrn_lb.astype(F32), axis=0), axis=0)[0]
    ffn_consts = (vec(rwkv_ln_g[0]), vec(rwkv_ln_b[0]), ones_bd, w_up_a[0].astype(BF16),
                  w_up_b[0].astype(BF16), w_out[0].astype(BF16), vec(norm_ffn_g[0]),
                  w_ffn_gate[0].astype(BF16), w_ffn_up[0].astype(BF16),
                  w_ffn_down[0].astype(BF16), vec(norm_final_g))
    g_mix = vec(norm_mix_g[0])

    xp = x_prompt.reshape(mp, D_MODEL)
    p_rwkv, p_hgrn, p_gate = _norm_proj(xp, g_mix, wa_in, wb_in, wc_in, tm=256)
    p3 = p_rwkv.reshape(bp, tp, RWKV_PROJ)
    pprev = jnp.concatenate([jnp.zeros((bp, 1, RWKV_PROJ), F32), p3[:, :-1]], axis=1)
    r, lw, k, v, kk, a, g, bonus = _rwkv_prep(p_rwkv, pprev.reshape(mp, RWKV_PROJ), *prep_consts,
                                              tm=256)
    seq = lambda t: t.reshape(bp, tp, RWKV_WIDTH)
    wkv0 = jnp.zeros((bp, RWKV_HEADS, RWKV_HEAD, RWKV_HEAD), F32)
    y, wkv_p = _rwkv_chunk(seq(r), seq(lw), seq(k), seq(v), seq(kk), seq(a), wkv0, tt=256)
    hg0 = jnp.zeros((bp, HGRN_HEADS, HGRN_HEAD, HGRN_HEAD), F32)
    o_b, hgT_p = _hgrn_chunk(p_hgrn.reshape(bp, tp, HGRN_PROJ), vec(lb), vec(hgrn_norm_g[0]), hg0,
                             tt=256)
    y_prompt = _merge_ffn(xp, y.reshape(mp, RWKV_WIDTH), bonus, g, o_b.reshape(mp, HGRN_WIDTH),
                          p_gate, ffn_consts, tm=256).reshape(bp, tp, D_MODEL)
    shift_p = p3[:, -1]
    hgrn_p = jnp.swapaxes(hgT_p, -1, -2)

    xs = x_sample.reshape(bs, D_MODEL)
    s_rwkv, s_hgrn, s_gate = _norm_proj(xs, g_mix, wa_in, wb_in, wc_in, tm=bs)
    r, lw, k, v, kk, a, g, bonus = _rwkv_prep(s_rwkv, state_rwkv_shift[0], *prep_consts, tm=bs)
    kcol = lambda t: t.T.reshape(RWKV_HEADS, RWKV_HEAD, 1, bs)
    wkv_t = jnp.transpose(state_rwkv_wkv[0], (1, 3, 2, 0))
    yT, wkv_nt = _rwkv_step(kcol(r), kcol(lw), kcol(k), kcol(kk), kcol(a), v.T, wkv_t)
    wkv_s = jnp.transpose(wkv_nt, (3, 0, 2, 1))
    hT = s_hgrn.T
    hcol = lambda t: t.reshape(HGRN_HEADS, HGRN_HEAD, 1, bs)
    hg_t = jnp.transpose(state_hgrn[0], (1, 2, 3, 0))
    obT, hg_nt = _hgrn_step(hcol(hT[:HGRN_WIDTH]), hcol(hT[HGRN_WIDTH:2 * HGRN_WIDTH]),
                            hT[2 * HGRN_WIDTH:3 * HGRN_WIDTH], hT[3 * HGRN_WIDTH:],
                            lb.reshape(HGRN_HEADS, HGRN_HEAD, 1, 1),
                            hgrn_norm_g[0].reshape(HGRN_WIDTH, 1), hg_t)
    hgrn_s = jnp.transpose(hg_nt, (3, 0, 1, 2))
    y_sample = _merge_ffn(xs, yT.T, bonus, g, obT.T, s_gate, ffn_consts, tm=bs).reshape(bs, 1, D_MODEL)

    return (y_prompt, y_sample, wkv_p[None], shift_p[None], hgrn_p[None],
            wkv_s[None], s_rwkv[None], hgrn_s[None])
```

```python
import functools

import jax
import jax.numpy as jnp
from jax import lax
from jax.experimental import pallas as pl
from jax.experimental.pallas import tpu as pltpu

F32 = jnp.float32
BF16 = jnp.bfloat16

D_MODEL = 1024
RWKV_WIDTH = 512
RWKV_HEAD = 64
RWKV_HEADS = 8
RWKV_DECAY_LORA = 64
RWKV_A_LORA = 64
RWKV_GATE_LORA = 128
RWKV_PROJ = 3 * RWKV_WIDTH + RWKV_DECAY_LORA + RWKV_A_LORA + RWKV_GATE_LORA
RWKV_GN_EPS = 64e-5
L2_EPS = 1e-12
HGRN_WIDTH = 512
HGRN_HEADS = 4
HGRN_HEAD = 128
HGRN_PROJ = 4 * HGRN_WIDTH
GATE_PROJ = 2 * D_MODEL
D_FF = 2816
RMS_EPS = 1e-6

CHUNK = 64
HGRN_SUB = 16
EXP_CLAMP = 80.0
VMEM_LIMIT = 56 * 1024 * 1024


def _dot(a, b):
    return jnp.dot(a.astype(BF16), b.astype(BF16), preferred_element_type=F32)


def _dot_nt(a, b):
    return lax.dot_general(a.astype(BF16), b.astype(BF16), (((1,), (1,)), ((), ())),
                           preferred_element_type=F32)


def _dot_tn(a, b):
    return lax.dot_general(a.astype(BF16), b.astype(BF16), (((0,), (0,)), ((), ())),
                           preferred_element_type=F32)


def _split2(x):
    hi = x.astype(BF16)
    lo = (x - hi.astype(F32)).astype(BF16)
    return hi, lo


def _split3(x):
    hi = x.astype(BF16)
    r1 = x - hi.astype(F32)
    mid = r1.astype(BF16)
    lo = (r1 - mid.astype(F32)).astype(BF16)
    return hi, mid, lo


def _dot_x3(a, b):
    ah, al = _split2(a)
    bh, bl = _split2(b)
    return _dot(ah, bh) + (_dot(ah, bl) + _dot(al, bh))


def _dot_exact_lhs(a_bf16, b):
    bh, bm, bl = _split3(b)
    return _dot(a_bf16, bh) + (_dot(a_bf16, bm) + _dot(a_bf16, bl))


def _sigmoid(x):
    return 1.0 / (1.0 + jnp.exp(-x))


def _rms_norm(x, g):
    return x * lax.rsqrt(jnp.mean(x * x, axis=-1, keepdims=True) + RMS_EPS) * g


def _const_spec(shape):
    nd = len(shape)
    return pl.BlockSpec(shape, lambda *_: (0,) * nd, pipeline_mode=pl.Buffered(1))


def _params(n_grid):
    return pltpu.CompilerParams(dimension_semantics=("arbitrary",) * n_grid,
                                vmem_limit_bytes=VMEM_LIMIT)


def _norm_proj_kernel(x_ref, g_ref, wa_ref, wb_ref, wc_ref, oa_ref, ob_ref, oc_ref):
    h = _rms_norm(x_ref[...], g_ref[...]).astype(BF16)
    oa_ref[...] = jnp.dot(h, wa_ref[...], preferred_element_type=F32)
    ob_ref[...] = jnp.dot(h, wb_ref[...], preferred_element_type=F32)
    oc_ref[...] = jnp.dot(h, wc_ref[...], preferred_element_type=F32)


def _norm_proj(x, g, wa, wb, wc, tm):
    m = x.shape[0]
    row = lambda n: pl.BlockSpec((tm, n), lambda i: (i, 0))
    return pl.pallas_call(
        _norm_proj_kernel,
        grid=(m // tm,),
        in_specs=[row(D_MODEL), _const_spec((1, D_MODEL)), _const_spec(wa.shape),
                  _const_spec(wb.shape), _const_spec(wc.shape)],
        out_specs=[row(RWKV_PROJ), row(HGRN_PROJ), row(GATE_PROJ)],
        out_shape=[jax.ShapeDtypeStruct((m, RWKV_PROJ), F32),
                   jax.ShapeDtypeStruct((m, HGRN_PROJ), F32),
                   jax.ShapeDtypeStruct((m, GATE_PROJ), F32)],
        compiler_params=_params(1),
        name="norm_proj",
    )(x, g, wa, wb, wc)


def _head_sum(x, ones_bd):
    hi, lo = _split2(x)
    return _dot(hi, ones_bd) + _dot(lo, ones_bd)


def _rwkv_prep_kernel(p_ref, pp_ref, mu_ref, w0_ref, w2_ref, a0_ref, a2_ref, g2_ref, kk_ref,
                      ka_ref, rk_ref, ones_ref,
                      r_out, lw_out, k_out, v_out, kk_out, a_out, g_out, bonus_out):
    W = RWKV_WIDTH
    p = p_ref[...]
    ps = p + mu_ref[...] * (pp_ref[...] - p)
    r = ps[:, 0:W]
    k = ps[:, W:2 * W]
    v = ps[:, 2 * W:3 * W]
    o0 = 3 * W
    wd = ps[:, o0:o0 + RWKV_DECAY_LORA]
    ad = ps[:, o0 + RWKV_DECAY_LORA:o0 + RWKV_DECAY_LORA + RWKV_A_LORA]
    gd = ps[:, o0 + RWKV_DECAY_LORA + RWKV_A_LORA:]
    z = -(w0_ref[...] + _dot_x3(jnp.tanh(wd), w2_ref[...]))
    softplus = jnp.maximum(z, 0.0) + jnp.log(1.0 + jnp.exp(-jnp.abs(z)))
    lw = -jnp.exp(-softplus - 0.5)
    a = _sigmoid(a0_ref[...] + _dot_x3(ad, a2_ref[...]))
    g = _dot_x3(_sigmoid(gd), g2_ref[...])
    ones_bd = ones_ref[...]
    kk = k * kk_ref[...]
    norm = jnp.sqrt(_head_sum(kk * kk, ones_bd))
    kk = kk / jnp.maximum(norm, L2_EPS)
    k = k * (1.0 + (a - 1.0) * ka_ref[...])
    bonus = _head_sum(r * k * rk_ref[...], ones_bd) * v
    r_out[...] = r
    lw_out[...] = lw
    k_out[...] = k
    v_out[...] = v
    kk_out[...] = kk
    a_out[...] = a
    g_out[...] = g
    bonus_out[...] = bonus


def _rwkv_prep(p, pprev, mu, w0, w2, a0, a2, g2, k_k, k_a, r_k, ones_bd, tm):
    m = p.shape[0]
    row = lambda n: pl.BlockSpec((tm, n), lambda i: (i, 0))
    consts = [mu, w0, w2, a0, a2, g2, k_k, k_a, r_k, ones_bd]
    return pl.pallas_call(
        _rwkv_prep_kernel,
        grid=(m // tm,),
        in_specs=[row(RWKV_PROJ), row(RWKV_PROJ)] + [_const_spec(c.shape) for c in consts],
        out_specs=[row(RWKV_WIDTH)] * 8,
        out_shape=[jax.ShapeDtypeStruct((m, RWKV_WIDTH), F32)] * 8,
        compiler_params=_params(1),
        name="rwkv_prep",
    )(p, pprev, *consts)


def _rwkv_chunk_kernel(r_ref, lw_ref, k_ref, v_ref, kk_ref, a_ref, s0_ref, y_ref, sT_ref, s_scr,
                       *, nch):
    C = CHUNK
    tb = pl.program_id(1)

    @pl.when(tb == 0)
    def _():
        s_scr[...] = s0_ref[0]

    row = lax.broadcasted_iota(jnp.int32, (C, C), 0)
    col = lax.broadcasted_iota(jnp.int32, (C, C), 1)
    tril_incl = (col <= row).astype(BF16)
    eye = (col == row).astype(F32)
    row2 = lax.broadcasted_iota(jnp.int32, (2 * C, 2 * C), 0)
    col2 = lax.broadcasted_iota(jnp.int32, (2 * C, 2 * C), 1)
    t_idx = row2 & (C - 1)
    s_idx = col2 & (C - 1)
    keep = s_idx < t_idx + jnp.where(row2 < C, 0, 1)

    def chunk(c, carry):
        sl = pl.ds(pl.multiple_of(c * C, C), C)
        r = r_ref[0, sl, :]
        lw = lw_ref[0, sl, :]
        k = k_ref[0, sl, :]
        v = v_ref[0, sl, :]
        kk = kk_ref[0, sl, :]
        kb = kk * a_ref[0, sl, :]
        cum = _dot_exact_lhs(tril_incl, lw)
        cl = cum[C - 1:C, :]
        eneg = jnp.exp(-cum)
        elast = jnp.exp(cl - cum)
        glast = jnp.exp(cl)
        at = -kk * jnp.exp(cum - lw)
        rt = r * jnp.exp(cum)
        bt = kb * eneg
        kt = k * eneg
        bh = kb * elast
        kh = k * elast
        heads = range(RWKV_HEADS)
        hsl = [slice(h * RWKV_HEAD, (h + 1) * RWKV_HEAD) for h in heads]
        G = [jnp.concatenate([at[:, hs], rt[:, hs]], axis=0) for hs in hsl]
        Kb = [jnp.concatenate([bt[:, hs], kt[:, hs]], axis=0) for hs in hsl]
        P = [jnp.where(keep, _dot_nt(G[h], Kb[h]), 0.0) for h in heads]
        X = [P[h][:C, :C] for h in heads]
        Tm = [eye + X[h] for h in heads]
        for _ in range(5):
            X = [_dot_x3(X[h], X[h]) for h in heads]
            Tm = [Tm[h] + _dot_x3(X[h], Tm[h]) for h in heads]
        S = [s_scr[h] for h in heads]
        X2 = [_dot_nt(G[h], S[h]) for h in heads]
        rhs = [X2[h][:C] + _dot(P[h][:C, C:], v[:, hsl[h]]) for h in heads]
        U = [_dot_x3(Tm[h], rhs[h]) for h in heads]
        UV = [jnp.concatenate([U[h], v[:, hsl[h]]], axis=0) for h in heads]
        for h in heads:
            y_ref[0, sl, hsl[h]] = X2[h][C:] + _dot(P[h][C:, :], UV[h])
        for h in heads:
            BK = jnp.concatenate([bh[:, hsl[h]], kh[:, hsl[h]]], axis=0)
            s_scr[h] = S[h] * glast[:, hsl[h]] + _dot_tn(UV[h], BK)
        return carry

    lax.fori_loop(0, nch, chunk, 0)

    @pl.when(tb == pl.num_programs(1) - 1)
    def _():
        sT_ref[0] = s_scr[...]


def _rwkv_chunk(r, lw, k, v, kk, a, s0, tt):
    b, t, w = r.shape
    seq = pl.BlockSpec((1, tt, w), lambda i, j: (i, j, 0))
    st = pl.BlockSpec((1, RWKV_HEADS, RWKV_HEAD, RWKV_HEAD), lambda i, j: (i, 0, 0, 0))
    return pl.pallas_call(
        functools.partial(_rwkv_chunk_kernel, nch=tt // CHUNK),
        grid=(b, t // tt),
        in_specs=[seq] * 6 + [st],
        out_specs=[seq, st],
        out_shape=[jax.ShapeDtypeStruct((b, t, w), F32), jax.ShapeDtypeStruct(s0.shape, F32)],
        scratch_shapes=[pltpu.VMEM((RWKV_HEADS, RWKV_HEAD, RWKV_HEAD), F32)],
        compiler_params=_params(2),
        name="rwkv_chunk",
    )(r, lw, k, v, kk, a, s0)


def _rwkv_step_kernel(r_ref, lw_ref, k_ref, kk_ref, a_ref, v_ref, s_ref, y_ref, so_ref):
    H = s_ref[0]
    kk = kk_ref[0]
    sa = jnp.sum(H * (-kk), axis=0)
    Hn = H * jnp.exp(lw_ref[0]) + (kk * a_ref[0]) * sa[None] + k_ref[0] * v_ref[...][None]
    so_ref[0] = Hn
    y_ref[...] = jnp.sum(Hn * r_ref[0], axis=0)


def _rwkv_step(r, lw, k, kk, a, v, s):
    nb = s.shape[-1]
    kvec = pl.BlockSpec((1, RWKV_HEAD, 1, nb), lambda h: (h, 0, 0, 0))
    vvec = pl.BlockSpec((RWKV_HEAD, nb), lambda h: (h, 0))
    st = pl.BlockSpec((1, RWKV_HEAD, RWKV_HEAD, nb), lambda h: (h, 0, 0, 0))
    return pl.pallas_call(
        _rwkv_step_kernel,
        grid=(RWKV_HEADS,),
        in_specs=[kvec] * 5 + [vvec, st],
        out_specs=[vvec, st],
        out_shape=[jax.ShapeDtypeStruct((RWKV_WIDTH, nb), F32), jax.ShapeDtypeStruct(s.shape, F32)],
        compiler_params=_params(1),
        name="rwkv_step",
    )(r, lw, k, kk, a, v, s)


def _hgrn_chunk_kernel(p_ref, lb_ref, ng_ref, s0_ref, o_ref, sT_ref, s_scr, *, nch):
    C = CHUNK
    SB = HGRN_SUB
    W = HGRN_WIDTH
    tb = pl.program_id(1)

    @pl.when(tb == 0)
    def _():
        s_scr[...] = s0_ref[0]

    row = lax.broadcasted_iota(jnp.int32, (C, C), 0)
    col = lax.broadcasted_iota(jnp.int32, (C, C), 1)
    tril_incl = (col <= row).astype(BF16)
    lb = lb_ref[...]
    ng = ng_ref[...]

    def chunk(c, carry):
        sl = pl.ds(pl.multiple_of(c * C, C), C)
        q = p_ref[0, sl, 0:W]
        q = q * _sigmoid(q)
        f = lb + (1.0 - lb) * _sigmoid(p_ref[0, sl, W:2 * W])
        kx = 1.0 - f
        iv = p_ref[0, sl, 2 * W:3 * W]
        og = p_ref[0, sl, 3 * W:4 * W]
        bc = _dot_exact_lhs(tril_incl, jnp.log(f))
        bl = bc[C - 1:C, :]
        qe = q * jnp.exp(bc)
        kend = kx * jnp.exp(bl - bc)
        glast = jnp.exp(bl)
        heads = range(HGRN_HEADS)
        subs = range(C // SB)
        hsl = [slice(h * HGRN_HEAD, (h + 1) * HGRN_HEAD) for h in heads]
        St = [s_scr[h] for h in heads]
        sc = [[None] * len(subs) for _ in heads]
        for i in subs:
            lo, n = i * SB, (i + 1) * SB
            if i == 0:
                eq = jnp.exp(bc[:n])
                ek = jnp.exp(jnp.minimum(-bc[:n], EXP_CLAMP))
            else:
                beta = bc[lo - 1:lo]
                eq = jnp.exp(bc[lo:n] - beta)
                ek = jnp.exp(jnp.minimum(beta - bc[:n], EXP_CLAMP))
            qi = q[lo:n] * eq
            ks = kx[:n] * ek
            keep = (lax.broadcasted_iota(jnp.int32, (SB, n), 1)
                    <= lax.broadcasted_iota(jnp.int32, (SB, n), 0) + lo)
            for h in heads:
                sc[h][i] = jnp.where(keep, _dot_nt(qi[:, hsl[h]], ks[:, hsl[h]]), 0.0)
        inter = [_dot_nt(qe[:, hsl[h]], St[h]) for h in heads]
        intra = [jnp.concatenate([_dot(sc[h][i], iv[:(i + 1) * SB, hsl[h]]) for i in subs], axis=0)
                 for h in heads]
        upd = [_dot_tn(iv[:, hsl[h]], kend[:, hsl[h]]) for h in heads]
        for h in heads:
            o = inter[h] + intra[h]
            o = o * lax.rsqrt(jnp.mean(o * o, axis=-1, keepdims=True) + RMS_EPS)
            o_ref[0, sl, hsl[h]] = o * ng[:, hsl[h]] * _sigmoid(og[:, hsl[h]])
            s_scr[h] = St[h] * glast[:, hsl[h]] + upd[h]
        return carry

    lax.fori_loop(0, nch, chunk, 0)

    @pl.when(tb == pl.num_programs(1) - 1)
    def _():
        sT_ref[0] = s_scr[...]


def _hgrn_chunk(p, lb, ng, s0, tt):
    b, t, _ = p.shape
    st = pl.BlockSpec((1, HGRN_HEADS, HGRN_HEAD, HGRN_HEAD), lambda i, j: (i, 0, 0, 0))
    return pl.pallas_call(
        functools.partial(_hgrn_chunk_kernel, nch=tt // CHUNK),
        grid=(b, t // tt),
        in_specs=[pl.BlockSpec((1, tt, HGRN_PROJ), lambda i, j: (i, j, 0)),
                  _const_spec((1, HGRN_WIDTH)), _const_spec((1, HGRN_WIDTH)), st],
        out_specs=[pl.BlockSpec((1, tt, HGRN_WIDTH), lambda i, j: (i, j, 0)), st],
        out_shape=[jax.ShapeDtypeStruct((b, t, HGRN_WIDTH), F32),
                   jax.ShapeDtypeStruct(s0.shape, F32)],
        scratch_shapes=[pltpu.VMEM((HGRN_HEADS, HGRN_HEAD, HGRN_HEAD), F32)],
        compiler_params=_params(2),
        name="hgrn_chunk",
    )(p, lb, ng, s0)


def _hgrn_step_kernel(q_ref, fl_ref, i_ref, og_ref, lb_ref, ng_ref, s_ref, o_ref, so_ref):
    q = q_ref[0]
    q = q * _sigmoid(q)
    lb = lb_ref[0]
    f = lb + (1.0 - lb) * _sigmoid(fl_ref[0])
    Sn = s_ref[0] * f + (1.0 - f) * i_ref[...][None]
    so_ref[0] = Sn
    o = jnp.sum(Sn * q, axis=0)
    o = o * lax.rsqrt(jnp.mean(o * o, axis=0, keepdims=True) + RMS_EPS)
    o_ref[...] = o * ng_ref[...] * _sigmoid(og_ref[...])


def _hgrn_step(q, fl, iv, og, lb, ng, s):
    nb = s.shape[-1]
    kvec = pl.BlockSpec((1, HGRN_HEAD, 1, nb), lambda h: (h, 0, 0, 0))
    vvec = pl.BlockSpec((HGRN_HEAD, nb), lambda h: (h, 0))
    st = pl.BlockSpec((1, HGRN_HEAD, HGRN_HEAD, nb), lambda h: (h, 0, 0, 0))
    return pl.pallas_call(
        _hgrn_step_kernel,
        grid=(HGRN_HEADS,),
        in_specs=[kvec, kvec, vvec, vvec,
                  pl.BlockSpec((1, HGRN_HEAD, 1, 1), lambda h: (h, 0, 0, 0)),
                  pl.BlockSpec((HGRN_HEAD, 1), lambda h: (h, 0)), st],
        out_specs=[vvec, st],
        out_shape=[jax.ShapeDtypeStruct((HGRN_WIDTH, nb), F32), jax.ShapeDtypeStruct(s.shape, F32)],
        compiler_params=_params(1),
        name="hgrn_step",
    )(q, fl, iv, og, lb, ng, s)


def _merge_ffn_kernel(x_ref, y_ref, bonus_ref, g_ref, ob_ref, pg_ref, lng_ref, lnb_ref, ones_ref,
                      wa_ref, wb_ref, wo_ref, nf_ref, wg_ref, wu_ref, wd_ref, nfin_ref, out_ref):
    ones_bd = ones_ref[...]
    y = y_ref[...]
    inv_n = 1.0 / RWKV_HEAD
    d = y - _head_sum(y, ones_bd) * inv_n
    var = _head_sum(d * d, ones_bd) * inv_n
    o_a = (d * lax.rsqrt(var + RWKV_GN_EPS) * lng_ref[...] + lnb_ref[...] + bonus_ref[...]) * g_ref[...]
    pg = pg_ref[...]
    merged = (_sigmoid(pg[:, :D_MODEL]) * _dot(o_a, wa_ref[...])
              + _sigmoid(pg[:, D_MODEL:]) * _dot(ob_ref[...], wb_ref[...]))
    x = x_ref[...] + _dot(merged, wo_ref[...])
    h = _rms_norm(x, nf_ref[...]).astype(BF16)
    gate = jnp.dot(h, wg_ref[...], preferred_element_type=F32)
    up = jnp.dot(h, wu_ref[...], preferred_element_type=F32)
    x = x + _dot(gate * _sigmoid(gate) * up, wd_ref[...])
    out_ref[...] = _rms_norm(x, nfin_ref[...])


def _merge_ffn(x, y, bonus, g, ob, pg, consts, tm):
    m = x.shape[0]
    row = lambda n: pl.BlockSpec((tm, n), lambda i: (i, 0))
    return pl.pallas_call(
        _merge_ffn_kernel,
        grid=(m // tm,),
        in_specs=[row(D_MODEL), row(RWKV_WIDTH), row(RWKV_WIDTH), row(RWKV_WIDTH), row(HGRN_WIDTH),
                  row(GATE_PROJ)] + [_const_spec(c.shape) for c in consts],
        out_specs=row(D_MODEL),
        out_shape=jax.ShapeDtypeStruct((m, D_MODEL), F32),
        compiler_params=_params(1),
        name="merge_ffn",
    )(x, y, bonus, g, ob, pg, *consts)


def kernel(x_prompt, x_sample, state_rwkv_wkv, state_rwkv_shift, state_hgrn, norm_mix_g, w_in, rwkv_mu, rwkv_w0, rwkv_w2, rwkv_a0, rwkv_a2, rwkv_g2, rwkv_k_k, rwkv_k_a, rwkv_r_k, rwkv_ln_g, rwkv_ln_b, w_up_a, hgrn_lb, hgrn_norm_g, w_up_b, w_out, norm_ffn_g, w_ffn_gate, w_ffn_up, w_ffn_down, norm_final_g):
    bp, tp, _ = x_prompt.shape
    bs = x_sample.shape[0]
    mp = bp * tp

    w_in0 = w_in[0]
    wa_in = w_in0[:, :RWKV_PROJ].astype(BF16)
    wb_in = w_in0[:, RWKV_PROJ:RWKV_PROJ + HGRN_PROJ].astype(BF16)
    wc_in = w_in0[:, RWKV_PROJ + HGRN_PROJ:].astype(BF16)
    vec = lambda p: p.reshape(1, -1).astype(F32)
    head_id = jnp.arange(RWKV_WIDTH, dtype=jnp.int32) // RWKV_HEAD
    ones_bd = (head_id[:, None] == head_id[None, :]).astype(BF16)
    prep_consts = (vec(rwkv_mu[0]), vec(rwkv_w0[0]), rwkv_w2[0], vec(rwkv_a0[0]), rwkv_a2[0],
                   rwkv_g2[0], vec(rwkv_k_k[0]), vec(rwkv_k_a[0]), vec(rwkv_r_k[0]), ones_bd)
    lb = jnp.cumsum(jax.nn.softmax(hgrn_lb.astype(F32), axis=0), axis=0)[0]
    ffn_consts = (vec(rwkv_ln_g[0]), vec(rwkv_ln_b[0]), ones_bd, w_up_a[0].astype(BF16),
                  w_up_b[0].astype(BF16), w_out[0].astype(BF16), vec(norm_ffn_g[0]),
                  w_ffn_gate[0].astype(BF16), w_ffn_up[0].astype(BF16),
                  w_ffn_down[0].astype(BF16), vec(norm_final_g))
    g_mix = vec(norm_mix_g[0])

    xp = x_prompt.reshape(mp, D_MODEL)
    p_rwkv, p_hgrn, p_gate = _norm_proj(xp, g_mix, wa_in, wb_in, wc_in, tm=256)
    p3 = p_rwkv.reshape(bp, tp, RWKV_PROJ)
    pprev = jnp.concatenate([jnp.zeros((bp, 1, RWKV_PROJ), F32), p3[:, :-1]], axis=1)
    r, lw, k, v, kk, a, g, bonus = _rwkv_prep(p_rwkv, pprev.reshape(mp, RWKV_PROJ), *prep_consts,
                                              tm=256)
    seq = lambda t: t.reshape(bp, tp, RWKV_WIDTH)
    wkv0 = jnp.zeros((bp, RWKV_HEADS, RWKV_HEAD, RWKV_HEAD), F32)
    y, wkv_p = _rwkv_chunk(seq(r), seq(lw), seq(k), seq(v), seq(kk), seq(a), wkv0, tt=256)
    hg0 = jnp.zeros((bp, HGRN_HEADS, HGRN_HEAD, HGRN_HEAD), F32)
    o_b, hgT_p = _hgrn_chunk(p_hgrn.reshape(bp, tp, HGRN_PROJ), vec(lb), vec(hgrn_norm_g[0]), hg0,
                             tt=256)
    y_prompt = _merge_ffn(xp, y.reshape(mp, RWKV_WIDTH), bonus, g, o_b.reshape(mp, HGRN_WIDTH),
                          p_gate, ffn_consts, tm=256).reshape(bp, tp, D_MODEL)
    shift_p = p3[:, -1]
    hgrn_p = jnp.swapaxes(hgT_p, -1, -2)

    xs = x_sample.reshape(bs, D_MODEL)
    s_rwkv, s_hgrn, s_gate = _norm_proj(xs, g_mix, wa_in, wb_in, wc_in, tm=bs)
    r, lw, k, v, kk, a, g, bonus = _rwkv_prep(s_rwkv, state_rwkv_shift[0], *prep_consts, tm=bs)
    kcol = lambda t: t.T.reshape(RWKV_HEADS, RWKV_HEAD, 1, bs)
    wkv_t = jnp.transpose(state_rwkv_wkv[0], (1, 3, 2, 0))
    yT, wkv_nt = _rwkv_step(kcol(r), kcol(lw), kcol(k), kcol(kk), kcol(a), v.T, wkv_t)
    wkv_s = jnp.transpose(wkv_nt, (3, 0, 2, 1))
    hT = s_hgrn.T
    hcol = lambda t: t.reshape(HGRN_HEADS, HGRN_HEAD, 1, bs)
    hg_t = jnp.transpose(state_hgrn[0], (1, 2, 3, 0))
    obT, hg_nt = _hgrn_step(hcol(hT[:HGRN_WIDTH]), hcol(hT[HGRN_WIDTH:2 * HGRN_WIDTH]),
                            hT[2 * HGRN_WIDTH:3 * HGRN_WIDTH], hT[3 * HGRN_WIDTH:],
                            lb.reshape(HGRN_HEADS, HGRN_HEAD, 1, 1),
                            hgrn_norm_g[0].reshape(HGRN_WIDTH, 1), hg_t)
    hgrn_s = jnp.transpose(hg_nt, (3, 0, 1, 2))
    y_sample = _merge_ffn(xs, yT.T, bonus, g, obT.T, s_gate, ffn_consts, tm=bs).reshape(bs, 1, D_MODEL)

    return (y_prompt, y_sample, wkv_p[None], shift_p[None], hgrn_p[None],
            wkv_s[None], s_rwkv[None], hgrn_s[None])
```

```python
import functools

import jax
import jax.numpy as jnp
from jax import lax
from jax.experimental import pallas as pl
from jax.experimental.pallas import tpu as pltpu

F32 = jnp.float32
BF16 = jnp.bfloat16

D_MODEL = 1024
RWKV_WIDTH = 512
RWKV_HEAD = 64
RWKV_HEADS = 8
RWKV_DECAY_LORA = 64
RWKV_A_LORA = 64
RWKV_GATE_LORA = 128
RWKV_PROJ = 3 * RWKV_WIDTH + RWKV_DECAY_LORA + RWKV_A_LORA + RWKV_GATE_LORA
RWKV_GN_EPS = 64e-5
L2_EPS = 1e-12
HGRN_WIDTH = 512
HGRN_HEADS = 4
HGRN_HEAD = 128
HGRN_PROJ = 4 * HGRN_WIDTH
GATE_PROJ = 2 * D_MODEL
D_FF = 2816
RMS_EPS = 1e-6

CHUNK = 64
HGRN_SUB = 16
EXP_CLAMP = 80.0
VMEM_LIMIT = 56 * 1024 * 1024


def _dot(a, b):
    return jnp.dot(a.astype(BF16), b.astype(BF16), preferred_element_type=F32)


def _dot_nt(a, b):
    return lax.dot_general(a.astype(BF16), b.astype(BF16), (((1,), (1,)), ((), ())),
                           preferred_element_type=F32)


def _dot_tn(a, b):
    return lax.dot_general(a.astype(BF16), b.astype(BF16), (((0,), (0,)), ((), ())),
                           preferred_element_type=F32)


def _split2(x):
    hi = x.astype(BF16)
    lo = (x - hi.astype(F32)).astype(BF16)
    return hi, lo


def _split3(x):
    hi = x.astype(BF16)
    r1 = x - hi.astype(F32)
    mid = r1.astype(BF16)
    lo = (r1 - mid.astype(F32)).astype(BF16)
    return hi, mid, lo


def _dot_x3(a, b):
    ah, al = _split2(a)
    bh, bl = _split2(b)
    return _dot(ah, bh) + (_dot(ah, bl) + _dot(al, bh))


def _dot_exact_lhs(a_bf16, b):
    bh, bm, bl = _split3(b)
    return _dot(a_bf16, bh) + (_dot(a_bf16, bm) + _dot(a_bf16, bl))


def _sigmoid(x):
    return 1.0 / (1.0 + jnp.exp(-x))


def _rms_norm(x, g):
    return x * lax.rsqrt(jnp.mean(x * x, axis=-1, keepdims=True) + RMS_EPS) * g


def _const_spec(shape):
    nd = len(shape)
    return pl.BlockSpec(shape, lambda *_: (0,) * nd, pipeline_mode=pl.Buffered(1))


def _params(n_grid):
    return pltpu.CompilerParams(dimension_semantics=("arbitrary",) * n_grid,
                                vmem_limit_bytes=VMEM_LIMIT)


def _norm_proj_kernel(x_ref, g_ref, wa_ref, wb_ref, wc_ref, oa_ref, ob_ref, oc_ref):
    h = _rms_norm(x_ref[...], g_ref[...]).astype(BF16)
    oa_ref[...] = jnp.dot(h, wa_ref[...], preferred_element_type=F32)
    ob_ref[...] = jnp.dot(h, wb_ref[...], preferred_element_type=F32)
    oc_ref[...] = jnp.dot(h, wc_ref[...], preferred_element_type=F32)


def _norm_proj(x, g, wa, wb, wc, tm):
    m = x.shape[0]
    row = lambda n: pl.BlockSpec((tm, n), lambda i: (i, 0))
    return pl.pallas_call(
        _norm_proj_kernel,
        grid=(m // tm,),
        in_specs=[row(D_MODEL), _const_spec((1, D_MODEL)), _const_spec(wa.shape),
                  _const_spec(wb.shape), _const_spec(wc.shape)],
        out_specs=[row(RWKV_PROJ), row(HGRN_PROJ), row(GATE_PROJ)],
        out_shape=[jax.ShapeDtypeStruct((m, RWKV_PROJ), F32),
                   jax.ShapeDtypeStruct((m, HGRN_PROJ), F32),
                   jax.ShapeDtypeStruct((m, GATE_PROJ), F32)],
        compiler_params=_params(1),
        name="norm_proj",
    )(x, g, wa, wb, wc)


def _head_sum(x, ones_bd):
    hi, lo = _split2(x)
    return _dot(hi, ones_bd) + _dot(lo, ones_bd)


def _rwkv_prep_kernel(p_ref, pp_ref, mu_ref, w0_ref, w2_ref, a0_ref, a2_ref, g2_ref, kk_ref,
                      ka_ref, rk_ref, ones_ref,
                      r_out, lw_out, k_out, v_out, kk_out, a_out, g_out, bonus_out):
    W = RWKV_WIDTH
    p = p_ref[...]
    ps = p + mu_ref[...] * (pp_ref[...] - p)
    r = ps[:, 0:W]
    k = ps[:, W:2 * W]
    v = ps[:, 2 * W:3 * W]
    o0 = 3 * W
    wd = ps[:, o0:o0 + RWKV_DECAY_LORA]
    ad = ps[:, o0 + RWKV_DECAY_LORA:o0 + RWKV_DECAY_LORA + RWKV_A_LORA]
    gd = ps[:, o0 + RWKV_DECAY_LORA + RWKV_A_LORA:]
    z = -(w0_ref[...] + _dot_x3(jnp.tanh(wd), w2_ref[...]))
    softplus = jnp.maximum(z, 0.0) + jnp.log(1.0 + jnp.exp(-jnp.abs(z)))
    lw = -jnp.exp(-softplus - 0.5)
    a = _sigmoid(a0_ref[...] + _dot_x3(ad, a2_ref[...]))
    g = _dot_x3(_sigmoid(gd), g2_ref[...])
    ones_bd = ones_ref[...]
    kk = k * kk_ref[...]
    norm = jnp.sqrt(_head_sum(kk * kk, ones_bd))
    kk = kk / jnp.maximum(norm, L2_EPS)
    k = k * (1.0 + (a - 1.0) * ka_ref[...])
    bonus = _head_sum(r * k * rk_ref[...], ones_bd) * v
    r_out[...] = r
    lw_out[...] = lw
    k_out[...] = k
    v_out[...] = v
    kk_out[...] = kk
    a_out[...] = a
    g_out[...] = g
    bonus_out[...] = bonus


def _rwkv_prep(p, pprev, mu, w0, w2, a0, a2, g2, k_k, k_a, r_k, ones_bd, tm):
    m = p.shape[0]
    row = lambda n: pl.BlockSpec((tm, n), lambda i: (i, 0))
    consts = [mu, w0, w2, a0, a2, g2, k_k, k_a, r_k, ones_bd]
    return pl.pallas_call(
        _rwkv_prep_kernel,
        grid=(m // tm,),
        in_specs=[row(RWKV_PROJ), row(RWKV_PROJ)] + [_const_spec(c.shape) for c in consts],
        out_specs=[row(RWKV_WIDTH)] * 8,
        out_shape=[jax.ShapeDtypeStruct((m, RWKV_WIDTH), F32)] * 8,
        compiler_params=_params(1),
        name="rwkv_prep",
    )(p, pprev, *consts)


def _rwkv_chunk_kernel(r_ref, lw_ref, k_ref, v_ref, kk_ref, a_ref, s0_ref, y_ref, sT_ref, s_scr,
                       *, nch):
    C = CHUNK
    tb = pl.program_id(1)

    @pl.when(tb == 0)
    def _():
        s_scr[...] = s0_ref[0]

    row = lax.broadcasted_iota(jnp.int32, (C, C), 0)
    col = lax.broadcasted_iota(jnp.int32, (C, C), 1)
    tril_incl = (col <= row).astype(BF16)
    eye = (col == row).astype(F32)
    row2 = lax.broadcasted_iota(jnp.int32, (2 * C, 2 * C), 0)
    col2 = lax.broadcasted_iota(jnp.int32, (2 * C, 2 * C), 1)
    t_idx = row2 & (C - 1)
    s_idx = col2 & (C - 1)
    keep = s_idx < t_idx + jnp.where(row2 < C, 0, 1)

    heads = range(RWKV_HEADS)
    hsl = [slice(h * RWKV_HEAD, (h + 1) * RWKV_HEAD) for h in heads]
    ch = [(c, h) for c in range(nch) for h in heads]

    rt, glast, vv, G, Kb, BK = {}, {}, {}, {}, {}, {}
    for c in range(nch):
        sl = slice(c * C, (c + 1) * C)
        r = r_ref[0, sl, :]
        lw = lw_ref[0, sl, :]
        k = k_ref[0, sl, :]
        kk = kk_ref[0, sl, :]
        kb = kk * a_ref[0, sl, :]
        cum = _dot_exact_lhs(tril_incl, lw)
        cl = cum[C - 1:C, :]
        eneg = jnp.exp(-cum)
        elast = jnp.exp(cl - cum)
        glast[c] = jnp.exp(cl)
        at = -kk * jnp.exp(cum - lw)
        rt[c] = r * jnp.exp(cum)
        bt = kb * eneg
        kt = k * eneg
        bh = kb * elast
        kh = k * elast
        vv[c] = v_ref[0, sl, :]
        for h, hs in enumerate(hsl):
            G[c, h] = jnp.concatenate([at[:, hs], rt[c][:, hs]], axis=0)
            Kb[c, h] = jnp.concatenate([bt[:, hs], kt[:, hs]], axis=0)
            BK[c, h] = jnp.concatenate([bh[:, hs], kh[:, hs]], axis=0)
    P = {i: jnp.where(keep, _dot_nt(G[i], Kb[i]), 0.0) for i in ch}
    X = {i: P[i][:C, :C] for i in ch}
    Tm = {i: eye + X[i] for i in ch}
    for _ in range(5):
        X = {i: _dot(X[i], X[i]) for i in ch}
        Tm = {i: Tm[i] + _dot(X[i], Tm[i]) for i in ch}
    AV = {(c, h): _dot(P[c, h][:C, C:], vv[c][:, hsl[h]]) for c, h in ch}
    WU = {i: _dot(Tm[i], jnp.concatenate([G[i][:C], AV[i]], axis=1)) for i in ch}

    S = [s_scr[h] for h in heads]
    for c in range(nch):
        sl = slice(c * C, (c + 1) * C)
        U = [_dot_nt(WU[c, h][:, :RWKV_HEAD], S[h]) + WU[c, h][:, RWKV_HEAD:] for h in heads]
        Yr = [_dot_nt(rt[c][:, hsl[h]], S[h]) for h in heads]
        UV = [jnp.concatenate([U[h], vv[c][:, hsl[h]]], axis=0) for h in heads]
        S = [S[h] * glast[c][:, hsl[h]] + _dot_tn(UV[h], BK[c, h]) for h in heads]
        for h in heads:
            y_ref[0, sl, hsl[h]] = Yr[h] + _dot(P[c, h][C:, :], UV[h])
    for h in heads:
        s_scr[h] = S[h]

    @pl.when(tb == pl.num_programs(1) - 1)
    def _():
        sT_ref[0] = s_scr[...]


def _rwkv_chunk(r, lw, k, v, kk, a, s0, tt):
    b, t, w = r.shape
    seq = pl.BlockSpec((1, tt, w), lambda i, j: (i, j, 0))
    st = pl.BlockSpec((1, RWKV_HEADS, RWKV_HEAD, RWKV_HEAD), lambda i, j: (i, 0, 0, 0))
    return pl.pallas_call(
        functools.partial(_rwkv_chunk_kernel, nch=tt // CHUNK),
        grid=(b, t // tt),
        in_specs=[seq] * 6 + [st],
        out_specs=[seq, st],
        out_shape=[jax.ShapeDtypeStruct((b, t, w), F32), jax.ShapeDtypeStruct(s0.shape, F32)],
        scratch_shapes=[pltpu.VMEM((RWKV_HEADS, RWKV_HEAD, RWKV_HEAD), F32)],
        compiler_params=_params(2),
        name="rwkv_chunk",
    )(r, lw, k, v, kk, a, s0)


def _rwkv_step_kernel(r_ref, lw_ref, k_ref, kk_ref, a_ref, v_ref, s_ref, y_ref, so_ref):
    H = s_ref[0]
    kk = kk_ref[0]
    sa = jnp.sum(H * (-kk), axis=0)
    Hn = H * jnp.exp(lw_ref[0]) + (kk * a_ref[0]) * sa[None] + k_ref[0] * v_ref[...][None]
    so_ref[0] = Hn
    y_ref[...] = jnp.sum(Hn * r_ref[0], axis=0)


def _rwkv_step(r, lw, k, kk, a, v, s):
    nb = s.shape[-1]
    kvec = pl.BlockSpec((1, RWKV_HEAD, 1, nb), lambda h: (h, 0, 0, 0))
    vvec = pl.BlockSpec((RWKV_HEAD, nb), lambda h: (h, 0))
    st = pl.BlockSpec((1, RWKV_HEAD, RWKV_HEAD, nb), lambda h: (h, 0, 0, 0))
    return pl.pallas_call(
        _rwkv_step_kernel,
        grid=(RWKV_HEADS,),
        in_specs=[kvec] * 5 + [vvec, st],
        out_specs=[vvec, st],
        out_shape=[jax.ShapeDtypeStruct((RWKV_WIDTH, nb), F32), jax.ShapeDtypeStruct(s.shape, F32)],
        compiler_params=_params(1),
        name="rwkv_step",
    )(r, lw, k, kk, a, v, s)


def _hgrn_chunk_kernel(p_ref, lb_ref, ng_ref, s0_ref, o_ref, sT_ref, s_scr, *, nch):
    C = CHUNK
    SB = HGRN_SUB
    W = HGRN_WIDTH
    tb = pl.program_id(1)

    @pl.when(tb == 0)
    def _():
        s_scr[...] = s0_ref[0]

    row = lax.broadcasted_iota(jnp.int32, (C, C), 0)
    col = lax.broadcasted_iota(jnp.int32, (C, C), 1)
    tril_incl = (col <= row).astype(BF16)
    lb = lb_ref[...]
    ng = ng_ref[...]

    def chunk(c, carry):
        sl = pl.ds(pl.multiple_of(c * C, C), C)
        q = p_ref[0, sl, 0:W]
        q = q * _sigmoid(q)
        f = lb + (1.0 - lb) * _sigmoid(p_ref[0, sl, W:2 * W])
        kx = 1.0 - f
        iv = p_ref[0, sl, 2 * W:3 * W]
        og = p_ref[0, sl, 3 * W:4 * W]
        bc = _dot_exact_lhs(tril_incl, jnp.log(f))
        bl = bc[C - 1:C, :]
        qe = q * jnp.exp(bc)
        kend = kx * jnp.exp(bl - bc)
        glast = jnp.exp(bl)
        heads = range(HGRN_HEADS)
        subs = range(C // SB)
        hsl = [slice(h * HGRN_HEAD, (h + 1) * HGRN_HEAD) for h in heads]
        St = [s_scr[h] for h in heads]
        sc = [[None] * len(subs) for _ in heads]
        for i in subs:
            lo, n = i * SB, (i + 1) * SB
            if i == 0:
                eq = jnp.exp(bc[:n])
                ek = jnp.exp(jnp.minimum(-bc[:n], EXP_CLAMP))
            else:
                beta = bc[lo - 1:lo]
                eq = jnp.exp(bc[lo:n] - beta)
                ek = jnp.exp(jnp.minimum(beta - bc[:n], EXP_CLAMP))
            qi = q[lo:n] * eq
            ks = kx[:n] * ek
            keep = (lax.broadcasted_iota(jnp.int32, (SB, n), 1)
                    <= lax.broadcasted_iota(jnp.int32, (SB, n), 0) + lo)
            for h in heads:
                sc[h][i] = jnp.where(keep, _dot_nt(qi[:, hsl[h]], ks[:, hsl[h]]), 0.0)
        inter = [_dot_nt(qe[:, hsl[h]], St[h]) for h in heads]
        intra = [jnp.concatenate([_dot(sc[h][i], iv[:(i + 1) * SB, hsl[h]]) for i in subs], axis=0)
                 for h in heads]
        upd = [_dot_tn(iv[:, hsl[h]], kend[:, hsl[h]]) for h in heads]
        for h in heads:
            o = inter[h] + intra[h]
            o = o * lax.rsqrt(jnp.mean(o * o, axis=-1, keepdims=True) + RMS_EPS)
            o_ref[0, sl, hsl[h]] = o * ng[:, hsl[h]] * _sigmoid(og[:, hsl[h]])
            s_scr[h] = St[h] * glast[:, hsl[h]] + upd[h]
        return carry

    lax.fori_loop(0, nch, chunk, 0)

    @pl.when(tb == pl.num_programs(1) - 1)
    def _():
        sT_ref[0] = s_scr[...]


def _hgrn_chunk(p, lb, ng, s0, tt):
    b, t, _ = p.shape
    st = pl.BlockSpec((1, HGRN_HEADS, HGRN_HEAD, HGRN_HEAD), lambda i, j: (i, 0, 0, 0))
    return pl.pallas_call(
        functools.partial(_hgrn_chunk_kernel, nch=tt // CHUNK),
        grid=(b, t // tt),
        in_specs=[pl.BlockSpec((1, tt, HGRN_PROJ), lambda i, j: (i, j, 0)),
                  _const_spec((1, HGRN_WIDTH)), _const_spec((1, HGRN_WIDTH)), st],
        out_specs=[pl.BlockSpec((1, tt, HGRN_WIDTH), lambda i, j: (i, j, 0)), st],
        out_shape=[jax.ShapeDtypeStruct((b, t, HGRN_WIDTH), F32),
                   jax.ShapeDtypeStruct(s0.shape, F32)],
        scratch_shapes=[pltpu.VMEM((HGRN_HEADS, HGRN_HEAD, HGRN_HEAD), F32)],
        compiler_params=_params(2),
        name="hgrn_chunk",
    )(p, lb, ng, s0)


def _hgrn_step_kernel(q_ref, fl_ref, i_ref, og_ref, lb_ref, ng_ref, s_ref, o_ref, so_ref):
    q = q_ref[0]
    q = q * _sigmoid(q)
    lb = lb_ref[0]
    f = lb + (1.0 - lb) * _sigmoid(fl_ref[0])
    Sn = s_ref[0] * f + (1.0 - f) * i_ref[...][None]
    so_ref[0] = Sn
    o = jnp.sum(Sn * q, axis=0)
    o = o * lax.rsqrt(jnp.mean(o * o, axis=0, keepdims=True) + RMS_EPS)
    o_ref[...] = o * ng_ref[...] * _sigmoid(og_ref[...])


def _hgrn_step(q, fl, iv, og, lb, ng, s):
    nb = s.shape[-1]
    kvec = pl.BlockSpec((1, HGRN_HEAD, 1, nb), lambda h: (h, 0, 0, 0))
    vvec = pl.BlockSpec((HGRN_HEAD, nb), lambda h: (h, 0))
    st = pl.BlockSpec((1, HGRN_HEAD, HGRN_HEAD, nb), lambda h: (h, 0, 0, 0))
    return pl.pallas_call(
        _hgrn_step_kernel,
        grid=(HGRN_HEADS,),
        in_specs=[kvec, kvec, vvec, vvec,
                  pl.BlockSpec((1, HGRN_HEAD, 1, 1), lambda h: (h, 0, 0, 0)),
                  pl.BlockSpec((HGRN_HEAD, 1), lambda h: (h, 0)), st],
        out_specs=[vvec, st],
        out_shape=[jax.ShapeDtypeStruct((HGRN_WIDTH, nb), F32), jax.ShapeDtypeStruct(s.shape, F32)],
        compiler_params=_params(1),
        name="hgrn_step",
    )(q, fl, iv, og, lb, ng, s)


def _merge_ffn_kernel(x_ref, y_ref, bonus_ref, g_ref, ob_ref, pg_ref, lng_ref, lnb_ref, ones_ref,
                      wa_ref, wb_ref, wo_ref, nf_ref, wg_ref, wu_ref, wd_ref, nfin_ref, out_ref):
    ones_bd = ones_ref[...]
    y = y_ref[...]
    inv_n = 1.0 / RWKV_HEAD
    d = y - _head_sum(y, ones_bd) * inv_n
    var = _head_sum(d * d, ones_bd) * inv_n
    o_a = (d * lax.rsqrt(var + RWKV_GN_EPS) * lng_ref[...] + lnb_ref[...] + bonus_ref[...]) * g_ref[...]
    pg = pg_ref[...]
    merged = (_sigmoid(pg[:, :D_MODEL]) * _dot(o_a, wa_ref[...])
              + _sigmoid(pg[:, D_MODEL:]) * _dot(ob_ref[...], wb_ref[...]))
    x = x_ref[...] + _dot(merged, wo_ref[...])
    h = _rms_norm(x, nf_ref[...]).astype(BF16)
    gate = jnp.dot(h, wg_ref[...], preferred_element_type=F32)
    up = jnp.dot(h, wu_ref[...], preferred_element_type=F32)
    x = x + _dot(gate * _sigmoid(gate) * up, wd_ref[...])
    out_ref[...] = _rms_norm(x, nfin_ref[...])


def _merge_ffn(x, y, bonus, g, ob, pg, consts, tm):
    m = x.shape[0]
    row = lambda n: pl.BlockSpec((tm, n), lambda i: (i, 0))
    return pl.pallas_call(
        _merge_ffn_kernel,
        grid=(m // tm,),
        in_specs=[row(D_MODEL), row(RWKV_WIDTH), row(RWKV_WIDTH), row(RWKV_WIDTH), row(HGRN_WIDTH),
                  row(GATE_PROJ)] + [_const_spec(c.shape) for c in consts],
        out_specs=row(D_MODEL),
        out_shape=jax.ShapeDtypeStruct((m, D_MODEL), F32),
        compiler_params=_params(1),
        name="merge_ffn",
    )(x, y, bonus, g, ob, pg, *consts)


def kernel(x_prompt, x_sample, state_rwkv_wkv, state_rwkv_shift, state_hgrn, norm_mix_g, w_in, rwkv_mu, rwkv_w0, rwkv_w2, rwkv_a0, rwkv_a2, rwkv_g2, rwkv_k_k, rwkv_k_a, rwkv_r_k, rwkv_ln_g, rwkv_ln_b, w_up_a, hgrn_lb, hgrn_norm_g, w_up_b, w_out, norm_ffn_g, w_ffn_gate, w_ffn_up, w_ffn_down, norm_final_g):
    bp, tp, _ = x_prompt.shape
    bs = x_sample.shape[0]
    mp = bp * tp

    w_in0 = w_in[0]
    wa_in = w_in0[:, :RWKV_PROJ].astype(BF16)
    wb_in = w_in0[:, RWKV_PROJ:RWKV_PROJ + HGRN_PROJ].astype(BF16)
    wc_in = w_in0[:, RWKV_PROJ + HGRN_PROJ:].astype(BF16)
    vec = lambda p: p.reshape(1, -1).astype(F32)
    head_id = jnp.arange(RWKV_WIDTH, dtype=jnp.int32) // RWKV_HEAD
    ones_bd = (head_id[:, None] == head_id[None, :]).astype(BF16)
    prep_consts = (vec(rwkv_mu[0]), vec(rwkv_w0[0]), rwkv_w2[0], vec(rwkv_a0[0]), rwkv_a2[0],
                   rwkv_g2[0], vec(rwkv_k_k[0]), vec(rwkv_k_a[0]), vec(rwkv_r_k[0]), ones_bd)
    lb = jnp.cumsum(jax.nn.softmax(hgrn_lb.astype(F32), axis=0), axis=0)[0]
    ffn_consts = (vec(rwkv_ln_g[0]), vec(rwkv_ln_b[0]), ones_bd, w_up_a[0].astype(BF16),
                  w_up_b[0].astype(BF16), w_out[0].astype(BF16), vec(norm_ffn_g[0]),
                  w_ffn_gate[0].astype(BF16), w_ffn_up[0].astype(BF16),
                  w_ffn_down[0].astype(BF16), vec(norm_final_g))
    g_mix = vec(norm_mix_g[0])

    xp = x_prompt.reshape(mp, D_MODEL)
    p_rwkv, p_hgrn, p_gate = _norm_proj(xp, g_mix, wa_in, wb_in, wc_in, tm=256)
    p3 = p_rwkv.reshape(bp, tp, RWKV_PROJ)
    pprev = jnp.concatenate([jnp.zeros((bp, 1, RWKV_PROJ), F32), p3[:, :-1]], axis=1)
    r, lw, k, v, kk, a, g, bonus = _rwkv_prep(p_rwkv, pprev.reshape(mp, RWKV_PROJ), *prep_consts,
                                              tm=256)
    seq = lambda t: t.reshape(bp, tp, RWKV_WIDTH)
    wkv0 = jnp.zeros((bp, RWKV_HEADS, RWKV_HEAD, RWKV_HEAD), F32)
    y, wkv_p = _rwkv_chunk(seq(r), seq(lw), seq(k), seq(v), seq(kk), seq(a), wkv0, tt=256)
    hg0 = jnp.zeros((bp, HGRN_HEADS, HGRN_HEAD, HGRN_HEAD), F32)
    o_b, hgT_p = _hgrn_chunk(p_hgrn.reshape(bp, tp, HGRN_PROJ), vec(lb), vec(hgrn_norm_g[0]), hg0,
                             tt=256)
    y_prompt = _merge_ffn(xp, y.reshape(mp, RWKV_WIDTH), bonus, g, o_b.reshape(mp, HGRN_WIDTH),
                          p_gate, ffn_consts, tm=256).reshape(bp, tp, D_MODEL)
    shift_p = p3[:, -1]
    hgrn_p = jnp.swapaxes(hgT_p, -1, -2)

    xs = x_sample.reshape(bs, D_MODEL)
    s_rwkv, s_hgrn, s_gate = _norm_proj(xs, g_mix, wa_in, wb_in, wc_in, tm=bs)
    r, lw, k, v, kk, a, g, bonus = _rwkv_prep(s_rwkv, state_rwkv_shift[0], *prep_consts, tm=bs)
    kcol = lambda t: t.T.reshape(RWKV_HEADS, RWKV_HEAD, 1, bs)
    wkv_t = jnp.transpose(state_rwkv_wkv[0], (1, 3, 2, 0))
    yT, wkv_nt = _rwkv_step(kcol(r), kcol(lw), kcol(k), kcol(kk), kcol(a), v.T, wkv_t)
    wkv_s = jnp.transpose(wkv_nt, (3, 0, 2, 1))
    hT = s_hgrn.T
    hcol = lambda t: t.reshape(HGRN_HEADS, HGRN_HEAD, 1, bs)
    hg_t = jnp.transpose(state_hgrn[0], (1, 2, 3, 0))
    obT, hg_nt = _hgrn_step(hcol(hT[:HGRN_WIDTH]), hcol(hT[HGRN_WIDTH:2 * HGRN_WIDTH]),
                            hT[2 * HGRN_WIDTH:3 * HGRN_WIDTH], hT[3 * HGRN_WIDTH:],
                            lb.reshape(HGRN_HEADS, HGRN_HEAD, 1, 1),
                            hgrn_norm_g[0].reshape(HGRN_WIDTH, 1), hg_t)
    hgrn_s = jnp.transpose(hg_nt, (3, 0, 1, 2))
    y_sample = _merge_ffn(xs, yT.T, bonus, g, obT.T, s_gate, ffn_consts, tm=bs).reshape(bs, 1, D_MODEL)

    return (y_prompt, y_sample, wkv_p[None], shift_p[None], hgrn_p[None],
            wkv_s[None], s_rwkv[None], hgrn_s[None])
```

```python
import functools

import jax
import jax.numpy as jnp
from jax import lax
from jax.experimental import pallas as pl
from jax.experimental.pallas import tpu as pltpu

F32 = jnp.float32
BF16 = jnp.bfloat16

D_MODEL = 1024
RWKV_WIDTH = 512
RWKV_HEAD = 64
RWKV_HEADS = 8
RWKV_DECAY_LORA = 64
RWKV_A_LORA = 64
RWKV_GATE_LORA = 128
RWKV_PROJ = 3 * RWKV_WIDTH + RWKV_DECAY_LORA + RWKV_A_LORA + RWKV_GATE_LORA
RWKV_GN_EPS = 64e-5
L2_EPS = 1e-12
HGRN_WIDTH = 512
HGRN_HEADS = 4
HGRN_HEAD = 128
HGRN_PROJ = 4 * HGRN_WIDTH
GATE_PROJ = 2 * D_MODEL
D_FF = 2816
RMS_EPS = 1e-6

CHUNK = 64
HGRN_SUB = 16
EXP_CLAMP = 80.0
VMEM_LIMIT = 56 * 1024 * 1024


def _dot(a, b):
    return jnp.dot(a.astype(BF16), b.astype(BF16), preferred_element_type=F32)


def _dot_nt(a, b):
    return lax.dot_general(a.astype(BF16), b.astype(BF16), (((1,), (1,)), ((), ())),
                           preferred_element_type=F32)


def _dot_tn(a, b):
    return lax.dot_general(a.astype(BF16), b.astype(BF16), (((0,), (0,)), ((), ())),
                           preferred_element_type=F32)


def _split2(x):
    hi = x.astype(BF16)
    lo = (x - hi.astype(F32)).astype(BF16)
    return hi, lo


def _split3(x):
    hi = x.astype(BF16)
    r1 = x - hi.astype(F32)
    mid = r1.astype(BF16)
    lo = (r1 - mid.astype(F32)).astype(BF16)
    return hi, mid, lo


def _dot_x3(a, b):
    ah, al = _split2(a)
    bh, bl = _split2(b)
    return _dot(ah, bh) + (_dot(ah, bl) + _dot(al, bh))


def _dot_exact_lhs(a_bf16, b):
    bh, bm, bl = _split3(b)
    return _dot(a_bf16, bh) + (_dot(a_bf16, bm) + _dot(a_bf16, bl))


def _sigmoid(x):
    return 1.0 / (1.0 + jnp.exp(-x))


def _rms_norm(x, g):
    return x * lax.rsqrt(jnp.mean(x * x, axis=-1, keepdims=True) + RMS_EPS) * g


def _head_sum(x, ones_bd):
    hi, lo = _split2(x)
    return _dot(hi, ones_bd) + _dot(lo, ones_bd)


def _const_spec(shape):
    nd = len(shape)
    return pl.BlockSpec(shape, lambda *_: (0,) * nd, pipeline_mode=pl.Buffered(1))


def _params(n_grid):
    return pltpu.CompilerParams(dimension_semantics=("arbitrary",) * n_grid,
                                vmem_limit_bytes=VMEM_LIMIT)


def _norm_proj_kernel(x_ref, g_ref, wa_ref, wb_ref, wc_ref, oa_ref, ob_ref, oc_ref):
    h = _rms_norm(x_ref[...], g_ref[...]).astype(BF16)
    oa_ref[...] = jnp.dot(h, wa_ref[...], preferred_element_type=F32)
    ob_ref[...] = jnp.dot(h, wb_ref[...], preferred_element_type=F32)
    oc_ref[...] = jnp.dot(h, wc_ref[...], preferred_element_type=F32)


def _norm_proj(x, g, wa, wb, wc, tm):
    m = x.shape[0]
    row = lambda n: pl.BlockSpec((tm, n), lambda i: (i, 0))
    return pl.pallas_call(
        _norm_proj_kernel,
        grid=(m // tm,),
        in_specs=[row(D_MODEL), _const_spec((1, D_MODEL)), _const_spec(wa.shape),
                  _const_spec(wb.shape), _const_spec(wc.shape)],
        out_specs=[row(RWKV_PROJ), row(HGRN_PROJ), row(GATE_PROJ)],
        out_shape=[jax.ShapeDtypeStruct((m, RWKV_PROJ), F32),
                   jax.ShapeDtypeStruct((m, HGRN_PROJ), F32),
                   jax.ShapeDtypeStruct((m, GATE_PROJ), F32)],
        compiler_params=_params(1),
        name="norm_proj",
    )(x, g, wa, wb, wc)


def _rwkv_token_math(p, pprev, mu_ref, w0_ref, w2_ref, a0_ref, a2_ref, g2_ref, kk_ref, ka_ref,
                     rk_ref, ones_ref):
    W = RWKV_WIDTH
    ps = p + mu_ref[...] * (pprev - p)
    r = ps[:, 0:W]
    k = ps[:, W:2 * W]
    v = ps[:, 2 * W:3 * W]
    o0 = 3 * W
    wd = ps[:, o0:o0 + RWKV_DECAY_LORA]
    ad = ps[:, o0 + RWKV_DECAY_LORA:o0 + RWKV_DECAY_LORA + RWKV_A_LORA]
    gd = ps[:, o0 + RWKV_DECAY_LORA + RWKV_A_LORA:]
    z = -(w0_ref[...] + _dot_x3(jnp.tanh(wd), w2_ref[...]))
    softplus = jnp.maximum(z, 0.0) + jnp.log(1.0 + jnp.exp(-jnp.abs(z)))
    lw = -jnp.exp(-softplus - 0.5)
    a = _sigmoid(a0_ref[...] + _dot_x3(ad, a2_ref[...]))
    g = _dot_x3(_sigmoid(gd), g2_ref[...])
    ones_bd = ones_ref[...]
    kk = k * kk_ref[...]
    norm = jnp.sqrt(_head_sum(kk * kk, ones_bd))
    kk = kk / jnp.maximum(norm, L2_EPS)
    k = k * (1.0 + (a - 1.0) * ka_ref[...])
    bonus = _head_sum(r * k * rk_ref[...], ones_bd) * v
    return r, lw, k, v, kk, a, g, bonus


def _rwkv_prep_kernel(p_ref, pp_ref, *refs):
    consts, outs = refs[:10], refs[10:]
    for o_ref, val in zip(outs, _rwkv_token_math(p_ref[...], pp_ref[...], *consts)):
        o_ref[...] = val


def _rwkv_prep(p, pprev, consts, tm):
    m = p.shape[0]
    row = lambda n: pl.BlockSpec((tm, n), lambda i: (i, 0))
    return pl.pallas_call(
        _rwkv_prep_kernel,
        grid=(m // tm,),
        in_specs=[row(RWKV_PROJ), row(RWKV_PROJ)] + [_const_spec(c.shape) for c in consts],
        out_specs=[row(RWKV_WIDTH)] * 8,
        out_shape=[jax.ShapeDtypeStruct((m, RWKV_WIDTH), F32)] * 8,
        compiler_params=_params(1),
        name="rwkv_prep",
    )(p, pprev, *consts)


def _rwkv_mix_kernel(p_ref, sh0_ref, mu_ref, w0_ref, w2_ref, a0_ref, a2_ref, g2_ref, kk_ref, ka_ref,
                     rk_ref, ones_ref, lng_ref, lnb_ref, s0_ref, o_ref, sT_ref,
                     s_scr, prev_scr, y_scr, *, nch):
    C = CHUNK
    tb = pl.program_id(1)

    @pl.when(tb == 0)
    def _():
        s_scr[...] = s0_ref[0]
        prev_scr[...] = sh0_ref[0]

    p = p_ref[0]
    rows = lax.broadcasted_iota(jnp.int32, p.shape, 0)
    pprev = jnp.where(rows == 0, prev_scr[...], pltpu.roll(p, 1, 0))
    prev_scr[...] = p[p.shape[0] - 1:, :]
    consts = (mu_ref, w0_ref, w2_ref, a0_ref, a2_ref, g2_ref, kk_ref, ka_ref, rk_ref, ones_ref)
    r_all, lw_all, k_all, v_all, kk_all, a_all, g_all, bonus_all = _rwkv_token_math(p, pprev, *consts)
    kb_all = kk_all * a_all

    row = lax.broadcasted_iota(jnp.int32, (C, C), 0)
    col = lax.broadcasted_iota(jnp.int32, (C, C), 1)
    tril_incl = (col <= row).astype(BF16)
    eye = (col == row).astype(F32)
    row2 = lax.broadcasted_iota(jnp.int32, (2 * C, 2 * C), 0)
    col2 = lax.broadcasted_iota(jnp.int32, (2 * C, 2 * C), 1)
    t_idx = row2 & (C - 1)
    s_idx = col2 & (C - 1)
    keep = s_idx < t_idx + jnp.where(row2 < C, 0, 1)

    heads = range(RWKV_HEADS)
    hsl = [slice(h * RWKV_HEAD, (h + 1) * RWKV_HEAD) for h in heads]
    ch = [(c, h) for c in range(nch) for h in heads]

    rt, glast, vv, G, Kb, BK = {}, {}, {}, {}, {}, {}
    for c in range(nch):
        sl = slice(c * C, (c + 1) * C)
        lw = lw_all[sl]
        k = k_all[sl]
        kk = kk_all[sl]
        kb = kb_all[sl]
        cum = _dot_exact_lhs(tril_incl, lw)
        cl = cum[C - 1:C, :]
        eneg = jnp.exp(-cum)
        elast = jnp.exp(cl - cum)
        glast[c] = jnp.exp(cl)
        at = -kk * jnp.exp(cum - lw)
        rt[c] = r_all[sl] * jnp.exp(cum)
        bt = kb * eneg
        kt = k * eneg
        bh = kb * elast
        kh = k * elast
        vv[c] = v_all[sl]
        for h, hs in enumerate(hsl):
            G[c, h] = jnp.concatenate([at[:, hs], rt[c][:, hs]], axis=0)
            Kb[c, h] = jnp.concatenate([bt[:, hs], kt[:, hs]], axis=0)
            BK[c, h] = jnp.concatenate([bh[:, hs], kh[:, hs]], axis=0)
    P = {i: jnp.where(keep, _dot_nt(G[i], Kb[i]), 0.0) for i in ch}
    X = {i: P[i][:C, :C] for i in ch}
    Tm = {i: eye + X[i] for i in ch}
    for _ in range(5):
        X = {i: _dot(X[i], X[i]) for i in ch}
        Tm = {i: Tm[i] + _dot(X[i], Tm[i]) for i in ch}
    AV = {(c, h): _dot(P[c, h][:C, C:], vv[c][:, hsl[h]]) for c, h in ch}
    WU = {i: _dot(Tm[i], jnp.concatenate([G[i][:C], AV[i]], axis=1)) for i in ch}

    S = [s_scr[h] for h in heads]
    for c in range(nch):
        sl = slice(c * C, (c + 1) * C)
        U = [_dot_nt(WU[c, h][:, :RWKV_HEAD], S[h]) + WU[c, h][:, RWKV_HEAD:] for h in heads]
        Yr = [_dot_nt(rt[c][:, hsl[h]], S[h]) for h in heads]
        UV = [jnp.concatenate([U[h], vv[c][:, hsl[h]]], axis=0) for h in heads]
        S = [S[h] * glast[c][:, hsl[h]] + _dot_tn(UV[h], BK[c, h]) for h in heads]
        for h in heads:
            y_scr[sl, hsl[h]] = Yr[h] + _dot(P[c, h][C:, :], UV[h])
    for h in heads:
        s_scr[h] = S[h]

    ones_bd = ones_ref[...]
    y = y_scr[...]
    inv_n = 1.0 / RWKV_HEAD
    d = y - _head_sum(y, ones_bd) * inv_n
    var = _head_sum(d * d, ones_bd) * inv_n
    o_ref[0] = (d * lax.rsqrt(var + RWKV_GN_EPS) * lng_ref[...] + lnb_ref[...] + bonus_all) * g_all

    @pl.when(tb == pl.num_programs(1) - 1)
    def _():
        sT_ref[0] = s_scr[...]


def _rwkv_mix(p, shift0, consts, s0, tt):
    b, t, _ = p.shape
    st = pl.BlockSpec((1, RWKV_HEADS, RWKV_HEAD, RWKV_HEAD), lambda i, j: (i, 0, 0, 0))
    return pl.pallas_call(
        functools.partial(_rwkv_mix_kernel, nch=tt // CHUNK),
        grid=(b, t // tt),
        in_specs=[pl.BlockSpec((1, tt, RWKV_PROJ), lambda i, j: (i, j, 0)),
                  pl.BlockSpec((1, 1, RWKV_PROJ), lambda i, j: (i, 0, 0))]
                 + [_const_spec(c.shape) for c in consts] + [st],
        out_specs=[pl.BlockSpec((1, tt, RWKV_WIDTH), lambda i, j: (i, j, 0)), st],
        out_shape=[jax.ShapeDtypeStruct((b, t, RWKV_WIDTH), F32), jax.ShapeDtypeStruct(s0.shape, F32)],
        scratch_shapes=[pltpu.VMEM((RWKV_HEADS, RWKV_HEAD, RWKV_HEAD), F32),
                        pltpu.VMEM((1, RWKV_PROJ), F32),
                        pltpu.VMEM((tt, RWKV_WIDTH), F32)],
        compiler_params=_params(2),
        name="rwkv_mix",
    )(p, shift0, *consts, s0)


def _rwkv_step_kernel(r_ref, lw_ref, k_ref, kk_ref, a_ref, v_ref, bonus_ref, g_ref, lng_ref, lnb_ref,
                      s_ref, o_ref, so_ref):
    H = s_ref[0]
    kk = kk_ref[0]
    sa = jnp.sum(H * (-kk), axis=0)
    Hn = H * jnp.exp(lw_ref[0]) + (kk * a_ref[0]) * sa[None] + k_ref[0] * v_ref[...][None]
    so_ref[0] = Hn
    y = jnp.sum(Hn * r_ref[0], axis=0)
    d = y - jnp.mean(y, axis=0, keepdims=True)
    var = jnp.mean(d * d, axis=0, keepdims=True)
    o_ref[...] = (d * lax.rsqrt(var + RWKV_GN_EPS) * lng_ref[...] + lnb_ref[...]
                  + bonus_ref[...]) * g_ref[...]


def _rwkv_step(r, lw, k, kk, a, v, bonus, g, lng, lnb, s):
    nb = s.shape[-1]
    kvec = pl.BlockSpec((1, RWKV_HEAD, 1, nb), lambda h: (h, 0, 0, 0))
    vvec = pl.BlockSpec((RWKV_HEAD, nb), lambda h: (h, 0))
    col = pl.BlockSpec((RWKV_HEAD, 1), lambda h: (h, 0))
    st = pl.BlockSpec((1, RWKV_HEAD, RWKV_HEAD, nb), lambda h: (h, 0, 0, 0))
    return pl.pallas_call(
        _rwkv_step_kernel,
        grid=(RWKV_HEADS,),
        in_specs=[kvec] * 5 + [vvec] * 3 + [col, col, st],
        out_specs=[vvec, st],
        out_shape=[jax.ShapeDtypeStruct((RWKV_WIDTH, nb), F32), jax.ShapeDtypeStruct(s.shape, F32)],
        compiler_params=_params(1),
        name="rwkv_step",
    )(r, lw, k, kk, a, v, bonus, g, lng, lnb, s)


def _hgrn_chunk_kernel(p_ref, lb_ref, ng_ref, s0_ref, o_ref, sT_ref, s_scr, *, nch):
    C = CHUNK
    SB = HGRN_SUB
    W = HGRN_WIDTH
    tb = pl.program_id(1)

    @pl.when(tb == 0)
    def _():
        s_scr[...] = s0_ref[0]

    row = lax.broadcasted_iota(jnp.int32, (C, C), 0)
    col = lax.broadcasted_iota(jnp.int32, (C, C), 1)
    tril_incl = (col <= row).astype(BF16)
    lb = lb_ref[...]
    ng = ng_ref[...]

    def chunk(c, carry):
        sl = pl.ds(pl.multiple_of(c * C, C), C)
        q = p_ref[0, sl, 0:W]
        q = q * _sigmoid(q)
        f = lb + (1.0 - lb) * _sigmoid(p_ref[0, sl, W:2 * W])
        kx = 1.0 - f
        iv = p_ref[0, sl, 2 * W:3 * W]
        og = p_ref[0, sl, 3 * W:4 * W]
        bc = _dot_exact_lhs(tril_incl, jnp.log(f))
        bl = bc[C - 1:C, :]
        qe = q * jnp.exp(bc)
        kend = kx * jnp.exp(bl - bc)
        glast = jnp.exp(bl)
        heads = range(HGRN_HEADS)
        subs = range(C // SB)
        hsl = [slice(h * HGRN_HEAD, (h + 1) * HGRN_HEAD) for h in heads]
        St = [s_scr[h] for h in heads]
        sc = [[None] * len(subs) for _ in heads]
        for i in subs:
            lo, n = i * SB, (i + 1) * SB
            if i == 0:
                eq = jnp.exp(bc[:n])
                ek = jnp.exp(jnp.minimum(-bc[:n], EXP_CLAMP))
            else:
                beta = bc[lo - 1:lo]
                eq = jnp.exp(bc[lo:n] - beta)
                ek = jnp.exp(jnp.minimum(beta - bc[:n], EXP_CLAMP))
            qi = q[lo:n] * eq
            ks = kx[:n] * ek
            keep = (lax.broadcasted_iota(jnp.int32, (SB, n), 1)
                    <= lax.broadcasted_iota(jnp.int32, (SB, n), 0) + lo)
            for h in heads:
                sc[h][i] = jnp.where(keep, _dot_nt(qi[:, hsl[h]], ks[:, hsl[h]]), 0.0)
        inter = [_dot_nt(qe[:, hsl[h]], St[h]) for h in heads]
        intra = [jnp.concatenate([_dot(sc[h][i], iv[:(i + 1) * SB, hsl[h]]) for i in subs], axis=0)
                 for h in heads]
        upd = [_dot_tn(iv[:, hsl[h]], kend[:, hsl[h]]) for h in heads]
        for h in heads:
            o = inter[h] + intra[h]
            o = o * lax.rsqrt(jnp.mean(o * o, axis=-1, keepdims=True) + RMS_EPS)
            o_ref[0, sl, hsl[h]] = o * ng[:, hsl[h]] * _sigmoid(og[:, hsl[h]])
            s_scr[h] = St[h] * glast[:, hsl[h]] + upd[h]
        return carry

    lax.fori_loop(0, nch, chunk, 0)

    @pl.when(tb == pl.num_programs(1) - 1)
    def _():
        sT_ref[0] = s_scr[...]


def _hgrn_chunk(p, lb, ng, s0, tt):
    b, t, _ = p.shape
    st = pl.BlockSpec((1, HGRN_HEADS, HGRN_HEAD, HGRN_HEAD), lambda i, j: (i, 0, 0, 0))
    return pl.pallas_call(
        functools.partial(_hgrn_chunk_kernel, nch=tt // CHUNK),
        grid=(b, t // tt),
        in_specs=[pl.BlockSpec((1, tt, HGRN_PROJ), lambda i, j: (i, j, 0)),
                  _const_spec((1, HGRN_WIDTH)), _const_spec((1, HGRN_WIDTH)), st],
        out_specs=[pl.BlockSpec((1, tt, HGRN_WIDTH), lambda i, j: (i, j, 0)), st],
        out_shape=[jax.ShapeDtypeStruct((b, t, HGRN_WIDTH), F32),
                   jax.ShapeDtypeStruct(s0.shape, F32)],
        scratch_shapes=[pltpu.VMEM((HGRN_HEADS, HGRN_HEAD, HGRN_HEAD), F32)],
        compiler_params=_params(2),
        name="hgrn_chunk",
    )(p, lb, ng, s0)


def _hgrn_step_kernel(q_ref, fl_ref, i_ref, og_ref, lb_ref, ng_ref, s_ref, o_ref, so_ref):
    q = q_ref[0]
    q = q * _sigmoid(q)
    lb = lb_ref[0]
    f = lb + (1.0 - lb) * _sigmoid(fl_ref[0])
    Sn = s_ref[0] * f + (1.0 - f) * i_ref[...][None]
    so_ref[0] = Sn
    o = jnp.sum(Sn * q, axis=0)
    o = o * lax.rsqrt(jnp.mean(o * o, axis=0, keepdims=True) + RMS_EPS)
    o_ref[...] = o * ng_ref[...] * _sigmoid(og_ref[...])


def _hgrn_step(q, fl, iv, og, lb, ng, s):
    nb = s.shape[-1]
    kvec = pl.BlockSpec((1, HGRN_HEAD, 1, nb), lambda h: (h, 0, 0, 0))
    vvec = pl.BlockSpec((HGRN_HEAD, nb), lambda h: (h, 0))
    st = pl.BlockSpec((1, HGRN_HEAD, HGRN_HEAD, nb), lambda h: (h, 0, 0, 0))
    return pl.pallas_call(
        _hgrn_step_kernel,
        grid=(HGRN_HEADS,),
        in_specs=[kvec, kvec, vvec, vvec,
                  pl.BlockSpec((1, HGRN_HEAD, 1, 1), lambda h: (h, 0, 0, 0)),
                  pl.BlockSpec((HGRN_HEAD, 1), lambda h: (h, 0)), st],
        out_specs=[vvec, st],
        out_shape=[jax.ShapeDtypeStruct((HGRN_WIDTH, nb), F32), jax.ShapeDtypeStruct(s.shape, F32)],
        compiler_params=_params(1),
        name="hgrn_step",
    )(q, fl, iv, og, lb, ng, s)


def _merge_ffn_kernel(x_ref, oa_ref, ob_ref, pg_ref, wa_ref, wb_ref, wo_ref, nf_ref, wg_ref, wu_ref,
                      wd_ref, nfin_ref, out_ref):
    pg = pg_ref[...]
    merged = (_sigmoid(pg[:, :D_MODEL]) * _dot(oa_ref[...], wa_ref[...])
              + _sigmoid(pg[:, D_MODEL:]) * _dot(ob_ref[...], wb_ref[...]))
    x = x_ref[...] + _dot(merged, wo_ref[...])
    h = _rms_norm(x, nf_ref[...]).astype(BF16)
    gate = jnp.dot(h, wg_ref[...], preferred_element_type=F32)
    up = jnp.dot(h, wu_ref[...], preferred_element_type=F32)
    x = x + _dot(gate * _sigmoid(gate) * up, wd_ref[...])
    out_ref[...] = _rms_norm(x, nfin_ref[...])


def _merge_ffn(x, oa, ob, pg, consts, tm):
    m = x.shape[0]
    row = lambda n: pl.BlockSpec((tm, n), lambda i: (i, 0))
    return pl.pallas_call(
        _merge_ffn_kernel,
        grid=(m // tm,),
        in_specs=[row(D_MODEL), row(RWKV_WIDTH), row(HGRN_WIDTH), row(GATE_PROJ)]
                 + [_const_spec(c.shape) for c in consts],
        out_specs=row(D_MODEL),
        out_shape=jax.ShapeDtypeStruct((m, D_MODEL), F32),
        compiler_params=_params(1),
        name="merge_ffn",
    )(x, oa, ob, pg, *consts)


def kernel(x_prompt, x_sample, state_rwkv_wkv, state_rwkv_shift, state_hgrn, norm_mix_g, w_in, rwkv_mu, rwkv_w0, rwkv_w2, rwkv_a0, rwkv_a2, rwkv_g2, rwkv_k_k, rwkv_k_a, rwkv_r_k, rwkv_ln_g, rwkv_ln_b, w_up_a, hgrn_lb, hgrn_norm_g, w_up_b, w_out, norm_ffn_g, w_ffn_gate, w_ffn_up, w_ffn_down, norm_final_g):
    bp, tp, _ = x_prompt.shape
    bs = x_sample.shape[0]
    mp = bp * tp

    w_in0 = w_in[0]
    wa_in = w_in0[:, :RWKV_PROJ].astype(BF16)
    wb_in = w_in0[:, RWKV_PROJ:RWKV_PROJ + HGRN_PROJ].astype(BF16)
    wc_in = w_in0[:, RWKV_PROJ + HGRN_PROJ:].astype(BF16)
    vec = lambda p: p.reshape(1, -1).astype(F32)
    head_id = jnp.arange(RWKV_WIDTH, dtype=jnp.int32) // RWKV_HEAD
    ones_bd = (head_id[:, None] == head_id[None, :]).astype(BF16)
    prep_consts = (vec(rwkv_mu[0]), vec(rwkv_w0[0]), rwkv_w2[0], vec(rwkv_a0[0]), rwkv_a2[0],
                   rwkv_g2[0], vec(rwkv_k_k[0]), vec(rwkv_k_a[0]), vec(rwkv_r_k[0]), ones_bd)
    lb = jnp.cumsum(jax.nn.softmax(hgrn_lb.astype(F32), axis=0), axis=0)[0]
    ffn_consts = (w_up_a[0].astype(BF16), w_up_b[0].astype(BF16), w_out[0].astype(BF16),
                  vec(norm_ffn_g[0]), w_ffn_gate[0].astype(BF16), w_ffn_up[0].astype(BF16),
                  w_ffn_down[0].astype(BF16), vec(norm_final_g))
    g_mix = vec(norm_mix_g[0])

    xp = x_prompt.reshape(mp, D_MODEL)
    p_rwkv, p_hgrn, p_gate = _norm_proj(xp, g_mix, wa_in, wb_in, wc_in, tm=256)
    p3 = p_rwkv.reshape(bp, tp, RWKV_PROJ)
    o_a, wkv_p = _rwkv_mix(p3, jnp.zeros((bp, 1, RWKV_PROJ), F32),
                           prep_consts + (vec(rwkv_ln_g[0]), vec(rwkv_ln_b[0])),
                           jnp.zeros((bp, RWKV_HEADS, RWKV_HEAD, RWKV_HEAD), F32), tt=256)
    hg0 = jnp.zeros((bp, HGRN_HEADS, HGRN_HEAD, HGRN_HEAD), F32)
    o_b, hgT_p = _hgrn_chunk(p_hgrn.reshape(bp, tp, HGRN_PROJ), vec(lb), vec(hgrn_norm_g[0]), hg0,
                             tt=256)
    y_prompt = _merge_ffn(xp, o_a.reshape(mp, RWKV_WIDTH), o_b.reshape(mp, HGRN_WIDTH), p_gate,
                          ffn_consts, tm=256).reshape(bp, tp, D_MODEL)
    shift_p = p3[:, -1]
    hgrn_p = jnp.swapaxes(hgT_p, -1, -2)

    xs = x_sample.reshape(bs, D_MODEL)
    s_rwkv, s_hgrn, s_gate = _norm_proj(xs, g_mix, wa_in, wb_in, wc_in, tm=bs)
    r, lw, k, v, kk, a, g, bonus = _rwkv_prep(s_rwkv, state_rwkv_shift[0], prep_consts, tm=bs)
    kcol = lambda t: t.T.reshape(RWKV_HEADS, RWKV_HEAD, 1, bs)
    wkv_t = jnp.transpose(state_rwkv_wkv[0], (1, 3, 2, 0))
    oaT, wkv_nt = _rwkv_step(kcol(r), kcol(lw), kcol(k), kcol(kk), kcol(a), v.T, bonus.T, g.T,
                             rwkv_ln_g[0].reshape(RWKV_WIDTH, 1), rwkv_ln_b[0].reshape(RWKV_WIDTH, 1),
                             wkv_t)
    wkv_s = jnp.transpose(wkv_nt, (3, 0, 2, 1))
    hT = s_hgrn.T
    hcol = lambda t: t.reshape(HGRN_HEADS, HGRN_HEAD, 1, bs)
    hg_t = jnp.transpose(state_hgrn[0], (1, 2, 3, 0))
    obT, hg_nt = _hgrn_step(hcol(hT[:HGRN_WIDTH]), hcol(hT[HGRN_WIDTH:2 * HGRN_WIDTH]),
                            hT[2 * HGRN_WIDTH:3 * HGRN_WIDTH], hT[3 * HGRN_WIDTH:],
                            lb.reshape(HGRN_HEADS, HGRN_HEAD, 1, 1),
                            hgrn_norm_g[0].reshape(HGRN_WIDTH, 1), hg_t)
    hgrn_s = jnp.transpose(hg_nt, (3, 0, 1, 2))
    y_sample = _merge_ffn(xs, oaT.T, obT.T, s_gate, ffn_consts, tm=bs).reshape(bs, 1, D_MODEL)

    return (y_prompt, y_sample, wkv_p[None], shift_p[None], hgrn_p[None],
            wkv_s[None], s_rwkv[None], hgrn_s[None])
```

```python
import functools

import jax
import jax.numpy as jnp
from jax import lax
from jax.experimental import pallas as pl
from jax.experimental.pallas import tpu as pltpu

F32 = jnp.float32
BF16 = jnp.bfloat16

D_MODEL = 1024
RWKV_WIDTH = 512
RWKV_HEAD = 64
RWKV_HEADS = 8
RWKV_DECAY_LORA = 64
RWKV_A_LORA = 64
RWKV_GATE_LORA = 128
RWKV_PROJ = 3 * RWKV_WIDTH + RWKV_DECAY_LORA + RWKV_A_LORA + RWKV_GATE_LORA
RWKV_GN_EPS = 64e-5
L2_EPS = 1e-12
HGRN_WIDTH = 512
HGRN_HEADS = 4
HGRN_HEAD = 128
HGRN_PROJ = 4 * HGRN_WIDTH
GATE_PROJ = 2 * D_MODEL
D_FF = 2816
RMS_EPS = 1e-6

CHUNK = 64
HGRN_SUB = 16
EXP_CLAMP = 80.0
VMEM_LIMIT = 56 * 1024 * 1024


def _dot(a, b):
    return jnp.dot(a.astype(BF16), b.astype(BF16), preferred_element_type=F32)


def _dot_nt(a, b):
    return lax.dot_general(a.astype(BF16), b.astype(BF16), (((1,), (1,)), ((), ())),
                           preferred_element_type=F32)


def _dot_tn(a, b):
    return lax.dot_general(a.astype(BF16), b.astype(BF16), (((0,), (0,)), ((), ())),
                           preferred_element_type=F32)


def _split2(x):
    hi = x.astype(BF16)
    lo = (x - hi.astype(F32)).astype(BF16)
    return hi, lo


def _split3(x):
    hi = x.astype(BF16)
    r1 = x - hi.astype(F32)
    mid = r1.astype(BF16)
    lo = (r1 - mid.astype(F32)).astype(BF16)
    return hi, mid, lo


def _dot_x3(a, b):
    ah, al = _split2(a)
    bh, bl = _split2(b)
    return _dot(ah, bh) + (_dot(ah, bl) + _dot(al, bh))


def _dot_exact_lhs(a_bf16, b):
    bh, bm, bl = _split3(b)
    return _dot(a_bf16, bh) + (_dot(a_bf16, bm) + _dot(a_bf16, bl))


def _sigmoid(x):
    return 1.0 / (1.0 + jnp.exp(-x))


def _rms_norm(x, g):
    return x * lax.rsqrt(jnp.mean(x * x, axis=-1, keepdims=True) + RMS_EPS) * g


def _head_sum(x, ones_bd):
    hi, lo = _split2(x)
    return _dot(hi, ones_bd) + _dot(lo, ones_bd)


def _const_spec(shape):
    nd = len(shape)
    return pl.BlockSpec(shape, lambda *_: (0,) * nd, pipeline_mode=pl.Buffered(1))


def _params(n_grid):
    return pltpu.CompilerParams(dimension_semantics=("arbitrary",) * n_grid,
                                vmem_limit_bytes=VMEM_LIMIT)


def _norm_proj_kernel(x_ref, g_ref, wa_ref, wb_ref, wc_ref, oa_ref, ob_ref, oc_ref):
    h = _rms_norm(x_ref[...], g_ref[...]).astype(BF16)
    oa_ref[...] = jnp.dot(h, wa_ref[...], preferred_element_type=F32)
    ob_ref[...] = jnp.dot(h, wb_ref[...], preferred_element_type=F32)
    oc_ref[...] = jnp.dot(h, wc_ref[...], preferred_element_type=F32)


def _norm_proj(x, g, wa, wb, wc, tm):
    m = x.shape[0]
    row = lambda n: pl.BlockSpec((tm, n), lambda i: (i, 0))
    return pl.pallas_call(
        _norm_proj_kernel,
        grid=(m // tm,),
        in_specs=[row(D_MODEL), _const_spec((1, D_MODEL)), _const_spec(wa.shape),
                  _const_spec(wb.shape), _const_spec(wc.shape)],
        out_specs=[row(RWKV_PROJ), row(HGRN_PROJ), row(GATE_PROJ)],
        out_shape=[jax.ShapeDtypeStruct((m, RWKV_PROJ), F32),
                   jax.ShapeDtypeStruct((m, HGRN_PROJ), F32),
                   jax.ShapeDtypeStruct((m, GATE_PROJ), F32)],
        compiler_params=_params(1),
        name="norm_proj",
    )(x, g, wa, wb, wc)


def _rwkv_token_math(p, pprev, mu_ref, w0_ref, w2_ref, a0_ref, a2_ref, g2_ref, kk_ref, ka_ref,
                     rk_ref, ones_ref):
    W = RWKV_WIDTH
    ps = p + mu_ref[...] * (pprev - p)
    r = ps[:, 0:W]
    k = ps[:, W:2 * W]
    v = ps[:, 2 * W:3 * W]
    o0 = 3 * W
    wd = ps[:, o0:o0 + RWKV_DECAY_LORA]
    ad = ps[:, o0 + RWKV_DECAY_LORA:o0 + RWKV_DECAY_LORA + RWKV_A_LORA]
    gd = ps[:, o0 + RWKV_DECAY_LORA + RWKV_A_LORA:]
    z = -(w0_ref[...] + _dot_x3(jnp.tanh(wd), w2_ref[...]))
    softplus = jnp.maximum(z, 0.0) + jnp.log(1.0 + jnp.exp(-jnp.abs(z)))
    lw = -jnp.exp(-softplus - 0.5)
    a = _sigmoid(a0_ref[...] + _dot_x3(ad, a2_ref[...]))
    g = _dot_x3(_sigmoid(gd), g2_ref[...])
    ones_bd = ones_ref[...]
    kk = k * kk_ref[...]
    norm = jnp.sqrt(_head_sum(kk * kk, ones_bd))
    kk = kk / jnp.maximum(norm, L2_EPS)
    k = k * (1.0 + (a - 1.0) * ka_ref[...])
    bonus = _head_sum(r * k * rk_ref[...], ones_bd) * v
    return r, lw, k, v, kk, a, g, bonus


def _rwkv_prep_kernel(p_ref, pp_ref, *refs):
    consts, outs = refs[:10], refs[10:]
    for o_ref, val in zip(outs, _rwkv_token_math(p_ref[...], pp_ref[...], *consts)):
        o_ref[...] = val


def _rwkv_prep(p, pprev, consts, tm):
    m = p.shape[0]
    row = lambda n: pl.BlockSpec((tm, n), lambda i: (i, 0))
    return pl.pallas_call(
        _rwkv_prep_kernel,
        grid=(m // tm,),
        in_specs=[row(RWKV_PROJ), row(RWKV_PROJ)] + [_const_spec(c.shape) for c in consts],
        out_specs=[row(RWKV_WIDTH)] * 8,
        out_shape=[jax.ShapeDtypeStruct((m, RWKV_WIDTH), F32)] * 8,
        compiler_params=_params(1),
        name="rwkv_prep",
    )(p, pprev, *consts)


def _rwkv_mix_kernel(p_ref, sh0_ref, mu_ref, w0_ref, w2_ref, a0_ref, a2_ref, g2_ref, kk_ref, ka_ref,
                     rk_ref, ones_ref, lng_ref, lnb_ref, s0_ref, o_ref, sT_ref,
                     s_scr, prev_scr, y_scr, *, nch):
    C = CHUNK
    tb = pl.program_id(1)

    @pl.when(tb == 0)
    def _():
        s_scr[...] = s0_ref[0]
        prev_scr[...] = sh0_ref[0]

    p = p_ref[0]
    rows = lax.broadcasted_iota(jnp.int32, p.shape, 0)
    pprev = jnp.where(rows == 0, prev_scr[...], pltpu.roll(p, 1, 0))
    prev_scr[...] = p[p.shape[0] - 1:, :]
    consts = (mu_ref, w0_ref, w2_ref, a0_ref, a2_ref, g2_ref, kk_ref, ka_ref, rk_ref, ones_ref)
    r_all, lw_all, k_all, v_all, kk_all, a_all, g_all, bonus_all = _rwkv_token_math(p, pprev, *consts)
    kb_all = kk_all * a_all

    row = lax.broadcasted_iota(jnp.int32, (C, C), 0)
    col = lax.broadcasted_iota(jnp.int32, (C, C), 1)
    tril_incl = (col <= row).astype(BF16)
    eye = (col == row).astype(F32)
    row2 = lax.broadcasted_iota(jnp.int32, (2 * C, 2 * C), 0)
    col2 = lax.broadcasted_iota(jnp.int32, (2 * C, 2 * C), 1)
    t_idx = row2 & (C - 1)
    s_idx = col2 & (C - 1)
    keep = s_idx < t_idx + jnp.where(row2 < C, 0, 1)

    heads = range(RWKV_HEADS)
    hsl = [slice(h * RWKV_HEAD, (h + 1) * RWKV_HEAD) for h in heads]
    ch = [(c, h) for c in range(nch) for h in heads]

    rt, glast, vv, G, Kb, BK = {}, {}, {}, {}, {}, {}
    for c in range(nch):
        sl = slice(c * C, (c + 1) * C)
        lw = lw_all[sl]
        k = k_all[sl]
        kk = kk_all[sl]
        kb = kb_all[sl]
        cum = _dot_exact_lhs(tril_incl, lw)
        cl = cum[C - 1:C, :]
        eneg = jnp.exp(-cum)
        elast = jnp.exp(cl - cum)
        glast[c] = jnp.exp(cl)
        at = -kk * jnp.exp(cum - lw)
        rt[c] = r_all[sl] * jnp.exp(cum)
        bt = kb * eneg
        kt = k * eneg
        bh = kb * elast
        kh = k * elast
        vv[c] = v_all[sl]
        for h, hs in enumerate(hsl):
            G[c, h] = jnp.concatenate([at[:, hs], rt[c][:, hs]], axis=0)
            Kb[c, h] = jnp.concatenate([bt[:, hs], kt[:, hs]], axis=0)
            BK[c, h] = jnp.concatenate([bh[:, hs], kh[:, hs]], axis=0)
    P = {i: jnp.where(keep, _dot_nt(G[i], Kb[i]), 0.0) for i in ch}
    X = {i: P[i][:C, :C] for i in ch}
    Tm = {i: eye + X[i] for i in ch}
    for _ in range(5):
        X = {i: _dot(X[i], X[i]) for i in ch}
        Tm = {i: Tm[i] + _dot(X[i], Tm[i]) for i in ch}
    AV = {(c, h): _dot(P[c, h][:C, C:], vv[c][:, hsl[h]]) for c, h in ch}
    WU = {i: _dot(Tm[i], jnp.concatenate([G[i][:C], AV[i]], axis=1)) for i in ch}

    S = [s_scr[h] for h in heads]
    for c in range(nch):
        sl = slice(c * C, (c + 1) * C)
        U = [_dot_nt(WU[c, h][:, :RWKV_HEAD], S[h]) + WU[c, h][:, RWKV_HEAD:] for h in heads]
        Yr = [_dot_nt(rt[c][:, hsl[h]], S[h]) for h in heads]
        UV = [jnp.concatenate([U[h], vv[c][:, hsl[h]]], axis=0) for h in heads]
        S = [S[h] * glast[c][:, hsl[h]] + _dot_tn(UV[h], BK[c, h]) for h in heads]
        for h in heads:
            y_scr[sl, hsl[h]] = Yr[h] + _dot(P[c, h][C:, :], UV[h])
    for h in heads:
        s_scr[h] = S[h]

    ones_bd = ones_ref[...]
    y = y_scr[...]
    inv_n = 1.0 / RWKV_HEAD
    d = y - _head_sum(y, ones_bd) * inv_n
    var = _head_sum(d * d, ones_bd) * inv_n
    o_ref[0] = (d * lax.rsqrt(var + RWKV_GN_EPS) * lng_ref[...] + lnb_ref[...] + bonus_all) * g_all

    @pl.when(tb == pl.num_programs(1) - 1)
    def _():
        sT_ref[0] = s_scr[...]


def _rwkv_mix(p, shift0, consts, s0, tt):
    b, t, _ = p.shape
    st = pl.BlockSpec((1, RWKV_HEADS, RWKV_HEAD, RWKV_HEAD), lambda i, j: (i, 0, 0, 0))
    return pl.pallas_call(
        functools.partial(_rwkv_mix_kernel, nch=tt // CHUNK),
        grid=(b, t // tt),
        in_specs=[pl.BlockSpec((1, tt, RWKV_PROJ), lambda i, j: (i, j, 0)),
                  pl.BlockSpec((1, 1, RWKV_PROJ), lambda i, j: (i, 0, 0))]
                 + [_const_spec(c.shape) for c in consts] + [st],
        out_specs=[pl.BlockSpec((1, tt, RWKV_WIDTH), lambda i, j: (i, j, 0)), st],
        out_shape=[jax.ShapeDtypeStruct((b, t, RWKV_WIDTH), F32), jax.ShapeDtypeStruct(s0.shape, F32)],
        scratch_shapes=[pltpu.VMEM((RWKV_HEADS, RWKV_HEAD, RWKV_HEAD), F32),
                        pltpu.VMEM((1, RWKV_PROJ), F32),
                        pltpu.VMEM((tt, RWKV_WIDTH), F32)],
        compiler_params=_params(2),
        name="rwkv_mix",
    )(p, shift0, *consts, s0)


def _rwkv_step_kernel(r_ref, lw_ref, k_ref, kk_ref, a_ref, v_ref, bonus_ref, g_ref, lng_ref, lnb_ref,
                      s_ref, o_ref, so_ref):
    H = s_ref[0]
    kk = kk_ref[0]
    sa = jnp.sum(H * (-kk), axis=0)
    Hn = H * jnp.exp(lw_ref[0]) + (kk * a_ref[0]) * sa[None] + k_ref[0] * v_ref[...][None]
    so_ref[0] = Hn
    y = jnp.sum(Hn * r_ref[0], axis=0)
    d = y - jnp.mean(y, axis=0, keepdims=True)
    var = jnp.mean(d * d, axis=0, keepdims=True)
    o_ref[...] = (d * lax.rsqrt(var + RWKV_GN_EPS) * lng_ref[...] + lnb_ref[...]
                  + bonus_ref[...]) * g_ref[...]


def _rwkv_step(r, lw, k, kk, a, v, bonus, g, lng, lnb, s):
    nb = s.shape[-1]
    kvec = pl.BlockSpec((1, RWKV_HEAD, 1, nb), lambda h: (h, 0, 0, 0))
    vvec = pl.BlockSpec((RWKV_HEAD, nb), lambda h: (h, 0))
    col = pl.BlockSpec((RWKV_HEAD, 1), lambda h: (h, 0))
    st = pl.BlockSpec((1, RWKV_HEAD, RWKV_HEAD, nb), lambda h: (h, 0, 0, 0))
    return pl.pallas_call(
        _rwkv_step_kernel,
        grid=(RWKV_HEADS,),
        in_specs=[kvec] * 5 + [vvec] * 3 + [col, col, st],
        out_specs=[vvec, st],
        out_shape=[jax.ShapeDtypeStruct((RWKV_WIDTH, nb), F32), jax.ShapeDtypeStruct(s.shape, F32)],
        compiler_params=_params(1),
        name="rwkv_step",
    )(r, lw, k, kk, a, v, bonus, g, lng, lnb, s)


def _hgrn_chunk_kernel(p_ref, lb_ref, ng_ref, s0_ref, o_ref, sT_ref, s_scr, *, nch):
    C = CHUNK
    SB = HGRN_SUB
    W = HGRN_WIDTH
    tb = pl.program_id(1)

    @pl.when(tb == 0)
    def _():
        s_scr[...] = s0_ref[0]

    tt = nch * C
    row = lax.broadcasted_iota(jnp.int32, (tt, tt), 0)
    col = lax.broadcasted_iota(jnp.int32, (tt, tt), 1)
    tril_bd = ((col <= row) & (col >= (row & -C))).astype(BF16)
    lb = lb_ref[...]
    ng = ng_ref[...]
    heads = range(HGRN_HEADS)
    chunks = range(nch)
    subs = range(C // SB)
    hsl = [slice(h * HGRN_HEAD, (h + 1) * HGRN_HEAD) for h in heads]

    q = p_ref[0, :, 0:W]
    q = q * _sigmoid(q)
    f = lb + (1.0 - lb) * _sigmoid(p_ref[0, :, W:2 * W])
    kx = 1.0 - f
    iv = p_ref[0, :, 2 * W:3 * W]
    bc = _dot_exact_lhs(tril_bd, jnp.log(f))
    qe = q * jnp.exp(bc)

    sc, kend, glast = {}, {}, {}
    for c in chunks:
        c0 = c * C
        bl = bc[c0 + C - 1:c0 + C, :]
        glast[c] = jnp.exp(bl)
        kend[c] = kx[c0:c0 + C] * jnp.exp(bl - bc[c0:c0 + C])
        for i in subs:
            lo, n = c0 + i * SB, c0 + (i + 1) * SB
            if i == 0:
                eq = jnp.exp(bc[lo:n])
                ek = jnp.exp(jnp.minimum(-bc[c0:n], EXP_CLAMP))
            else:
                beta = bc[lo - 1:lo]
                eq = jnp.exp(bc[lo:n] - beta)
                ek = jnp.exp(jnp.minimum(beta - bc[c0:n], EXP_CLAMP))
            qi = q[lo:n] * eq
            ks = kx[c0:n] * ek
            keep = (lax.broadcasted_iota(jnp.int32, (SB, n - c0), 1)
                    <= lax.broadcasted_iota(jnp.int32, (SB, n - c0), 0) + i * SB)
            for h in heads:
                sc[c, h, i] = jnp.where(keep, _dot_nt(qi[:, hsl[h]], ks[:, hsl[h]]), 0.0)
    intra = {(c, h): jnp.concatenate(
        [_dot(sc[c, h, i], iv[c * C:c * C + (i + 1) * SB, hsl[h]]) for i in subs], axis=0)
        for c in chunks for h in heads}
    upd = {(c, h): _dot_tn(iv[c * C:(c + 1) * C, hsl[h]], kend[c][:, hsl[h]])
           for c in chunks for h in heads}
    St = [s_scr[h] for h in heads]
    for c in chunks:
        sl = slice(c * C, (c + 1) * C)
        for h in heads:
            o = _dot_nt(qe[sl, hsl[h]], St[h]) + intra[c, h]
            o = o * lax.rsqrt(jnp.mean(o * o, axis=-1, keepdims=True) + RMS_EPS)
            o_ref[0, sl, hsl[h]] = o * ng[:, hsl[h]] * _sigmoid(p_ref[0, sl, 3 * W + h * HGRN_HEAD:
                                                                  3 * W + (h + 1) * HGRN_HEAD])
        St = [St[h] * glast[c][:, hsl[h]] + upd[c, h] for h in heads]
    for h in heads:
        s_scr[h] = St[h]

    @pl.when(tb == pl.num_programs(1) - 1)
    def _():
        sT_ref[0] = s_scr[...]


def _hgrn_chunk(p, lb, ng, s0, tt):
    b, t, _ = p.shape
    st = pl.BlockSpec((1, HGRN_HEADS, HGRN_HEAD, HGRN_HEAD), lambda i, j: (i, 0, 0, 0))
    return pl.pallas_call(
        functools.partial(_hgrn_chunk_kernel, nch=tt // CHUNK),
        grid=(b, t // tt),
        in_specs=[pl.BlockSpec((1, tt, HGRN_PROJ), lambda i, j: (i, j, 0)),
                  _const_spec((1, HGRN_WIDTH)), _const_spec((1, HGRN_WIDTH)), st],
        out_specs=[pl.BlockSpec((1, tt, HGRN_WIDTH), lambda i, j: (i, j, 0)), st],
        out_shape=[jax.ShapeDtypeStruct((b, t, HGRN_WIDTH), F32),
                   jax.ShapeDtypeStruct(s0.shape, F32)],
        scratch_shapes=[pltpu.VMEM((HGRN_HEADS, HGRN_HEAD, HGRN_HEAD), F32)],
        compiler_params=_params(2),
        name="hgrn_chunk",
    )(p, lb, ng, s0)


def _hgrn_step_kernel(q_ref, fl_ref, i_ref, og_ref, lb_ref, ng_ref, s_ref, o_ref, so_ref):
    q = q_ref[0]
    q = q * _sigmoid(q)
    lb = lb_ref[0]
    f = lb + (1.0 - lb) * _sigmoid(fl_ref[0])
    Sn = s_ref[0] * f + (1.0 - f) * i_ref[...][None]
    so_ref[0] = Sn
    o = jnp.sum(Sn * q, axis=0)
    o = o * lax.rsqrt(jnp.mean(o * o, axis=0, keepdims=True) + RMS_EPS)
    o_ref[...] = o * ng_ref[...] * _sigmoid(og_ref[...])


def _hgrn_step(q, fl, iv, og, lb, ng, s):
    nb = s.shape[-1]
    kvec = pl.BlockSpec((1, HGRN_HEAD, 1, nb), lambda h: (h, 0, 0, 0))
    vvec = pl.BlockSpec((HGRN_HEAD, nb), lambda h: (h, 0))
    st = pl.BlockSpec((1, HGRN_HEAD, HGRN_HEAD, nb), lambda h: (h, 0, 0, 0))
    return pl.pallas_call(
        _hgrn_step_kernel,
        grid=(HGRN_HEADS,),
        in_specs=[kvec, kvec, vvec, vvec,
                  pl.BlockSpec((1, HGRN_HEAD, 1, 1), lambda h: (h, 0, 0, 0)),
                  pl.BlockSpec((HGRN_HEAD, 1), lambda h: (h, 0)), st],
        out_specs=[vvec, st],
        out_shape=[jax.ShapeDtypeStruct((HGRN_WIDTH, nb), F32), jax.ShapeDtypeStruct(s.shape, F32)],
        compiler_params=_params(1),
        name="hgrn_step",
    )(q, fl, iv, og, lb, ng, s)


def _merge_ffn_kernel(x_ref, oa_ref, ob_ref, pg_ref, wa_ref, wb_ref, wo_ref, nf_ref, wg_ref, wu_ref,
                      wd_ref, nfin_ref, out_ref):
    pg = pg_ref[...]
    merged = (_sigmoid(pg[:, :D_MODEL]) * _dot(oa_ref[...], wa_ref[...])
              + _sigmoid(pg[:, D_MODEL:]) * _dot(ob_ref[...], wb_ref[...]))
    x = x_ref[...] + _dot(merged, wo_ref[...])
    h = _rms_norm(x, nf_ref[...]).astype(BF16)
    gate = jnp.dot(h, wg_ref[...], preferred_element_type=F32)
    up = jnp.dot(h, wu_ref[...], preferred_element_type=F32)
    x = x + _dot(gate * _sigmoid(gate) * up, wd_ref[...])
    out_ref[...] = _rms_norm(x, nfin_ref[...])


def _merge_ffn(x, oa, ob, pg, consts, tm):
    m = x.shape[0]
    row = lambda n: pl.BlockSpec((tm, n), lambda i: (i, 0))
    return pl.pallas_call(
        _merge_ffn_kernel,
        grid=(m // tm,),
        in_specs=[row(D_MODEL), row(RWKV_WIDTH), row(HGRN_WIDTH), row(GATE_PROJ)]
                 + [_const_spec(c.shape) for c in consts],
        out_specs=row(D_MODEL),
        out_shape=jax.ShapeDtypeStruct((m, D_MODEL), F32),
        compiler_params=_params(1),
        name="merge_ffn",
    )(x, oa, ob, pg, *consts)


def kernel(x_prompt, x_sample, state_rwkv_wkv, state_rwkv_shift, state_hgrn, norm_mix_g, w_in, rwkv_mu, rwkv_w0, rwkv_w2, rwkv_a0, rwkv_a2, rwkv_g2, rwkv_k_k, rwkv_k_a, rwkv_r_k, rwkv_ln_g, rwkv_ln_b, w_up_a, hgrn_lb, hgrn_norm_g, w_up_b, w_out, norm_ffn_g, w_ffn_gate, w_ffn_up, w_ffn_down, norm_final_g):
    bp, tp, _ = x_prompt.shape
    bs = x_sample.shape[0]
    mp = bp * tp

    w_in0 = w_in[0]
    wa_in = w_in0[:, :RWKV_PROJ].astype(BF16)
    wb_in = w_in0[:, RWKV_PROJ:RWKV_PROJ + HGRN_PROJ].astype(BF16)
    wc_in = w_in0[:, RWKV_PROJ + HGRN_PROJ:].astype(BF16)
    vec = lambda p: p.reshape(1, -1).astype(F32)
    head_id = jnp.arange(RWKV_WIDTH, dtype=jnp.int32) // RWKV_HEAD
    ones_bd = (head_id[:, None] == head_id[None, :]).astype(BF16)
    prep_consts = (vec(rwkv_mu[0]), vec(rwkv_w0[0]), rwkv_w2[0], vec(rwkv_a0[0]), rwkv_a2[0],
                   rwkv_g2[0], vec(rwkv_k_k[0]), vec(rwkv_k_a[0]), vec(rwkv_r_k[0]), ones_bd)
    lb = jnp.cumsum(jax.nn.softmax(hgrn_lb.astype(F32), axis=0), axis=0)[0]
    ffn_consts = (w_up_a[0].astype(BF16), w_up_b[0].astype(BF16), w_out[0].astype(BF16),
                  vec(norm_ffn_g[0]), w_ffn_gate[0].astype(BF16), w_ffn_up[0].astype(BF16),
                  w_ffn_down[0].astype(BF16), vec(norm_final_g))
    g_mix = vec(norm_mix_g[0])

    xp = x_prompt.reshape(mp, D_MODEL)
    p_rwkv, p_hgrn, p_gate = _norm_proj(xp, g_mix, wa_in, wb_in, wc_in, tm=512)
    p3 = p_rwkv.reshape(bp, tp, RWKV_PROJ)
    o_a, wkv_p = _rwkv_mix(p3, jnp.zeros((bp, 1, RWKV_PROJ), F32),
                           prep_consts + (vec(rwkv_ln_g[0]), vec(rwkv_ln_b[0])),
                           jnp.zeros((bp, RWKV_HEADS, RWKV_HEAD, RWKV_HEAD), F32), tt=256)
    hg0 = jnp.zeros((bp, HGRN_HEADS, HGRN_HEAD, HGRN_HEAD), F32)
    o_b, hgT_p = _hgrn_chunk(p_hgrn.reshape(bp, tp, HGRN_PROJ), vec(lb), vec(hgrn_norm_g[0]), hg0,
                             tt=256)
    y_prompt = _merge_ffn(xp, o_a.reshape(mp, RWKV_WIDTH), o_b.reshape(mp, HGRN_WIDTH), p_gate,
                          ffn_consts, tm=512).reshape(bp, tp, D_MODEL)
    shift_p = p3[:, -1]
    hgrn_p = jnp.swapaxes(hgT_p, -1, -2)

    xs = x_sample.reshape(bs, D_MODEL)
    s_rwkv, s_hgrn, s_gate = _norm_proj(xs, g_mix, wa_in, wb_in, wc_in, tm=bs)
    r, lw, k, v, kk, a, g, bonus = _rwkv_prep(s_rwkv, state_rwkv_shift[0], prep_consts, tm=bs)
    kcol = lambda t: t.T.reshape(RWKV_HEADS, RWKV_HEAD, 1, bs)
    wkv_t = jnp.transpose(state_rwkv_wkv[0], (1, 3, 2, 0))
    oaT, wkv_nt = _rwkv_step(kcol(r), kcol(lw), kcol(k), kcol(kk), kcol(a), v.T, bonus.T, g.T,
                             rwkv_ln_g[0].reshape(RWKV_WIDTH, 1), rwkv_ln_b[0].reshape(RWKV_WIDTH, 1),
                             wkv_t)
    wkv_s = jnp.transpose(wkv_nt, (3, 0, 2, 1))
    hT = s_hgrn.T
    hcol = lambda t: t.reshape(HGRN_HEADS, HGRN_HEAD, 1, bs)
    hg_t = jnp.transpose(state_hgrn[0], (1, 2, 3, 0))
    obT, hg_nt = _hgrn_step(hcol(hT[:HGRN_WIDTH]), hcol(hT[HGRN_WIDTH:2 * HGRN_WIDTH]),
                            hT[2 * HGRN_WIDTH:3 * HGRN_WIDTH], hT[3 * HGRN_WIDTH:],
                            lb.reshape(HGRN_HEADS, HGRN_HEAD, 1, 1),
                            hgrn_norm_g[0].reshape(HGRN_WIDTH, 1), hg_t)
    hgrn_s = jnp.transpose(hg_nt, (3, 0, 1, 2))
    y_sample = _merge_ffn(xs, oaT.T, obT.T, s_gate, ffn_consts, tm=bs).reshape(bs, 1, D_MODEL)

    return (y_prompt, y_sample, wkv_p[None], shift_p[None], hgrn_p[None],
            wkv_s[None], s_rwkv[None], hgrn_s[None])
```

```python
import functools

import jax
import jax.numpy as jnp
from jax import lax
from jax.experimental import pallas as pl
from jax.experimental.pallas import tpu as pltpu

F32 = jnp.float32
BF16 = jnp.bfloat16

D_MODEL = 1024
RWKV_WIDTH = 512
RWKV_HEAD = 64
RWKV_HEADS = 8
RWKV_DECAY_LORA = 64
RWKV_A_LORA = 64
RWKV_GATE_LORA = 128
RWKV_PROJ = 3 * RWKV_WIDTH + RWKV_DECAY_LORA + RWKV_A_LORA + RWKV_GATE_LORA
RWKV_GN_EPS = 64e-5
L2_EPS = 1e-12
HGRN_WIDTH = 512
HGRN_HEADS = 4
HGRN_HEAD = 128
HGRN_PROJ = 4 * HGRN_WIDTH
GATE_PROJ = 2 * D_MODEL
D_FF = 2816
RMS_EPS = 1e-6

MXU_WIDTH = 256
CHUNK = 64
HGRN_SUB = 16
EXP_CLAMP = 80.0
VMEM_LIMIT = 56 * 1024 * 1024


def _dot(a, b):
    return jnp.dot(a.astype(BF16), b.astype(BF16), preferred_element_type=F32)


def _dot_nt(a, b):
    return lax.dot_general(a.astype(BF16), b.astype(BF16), (((1,), (1,)), ((), ())),
                           preferred_element_type=F32)


def _dot_tn(a, b):
    return lax.dot_general(a.astype(BF16), b.astype(BF16), (((0,), (0,)), ((), ())),
                           preferred_element_type=F32)


def _split2(x):
    hi = x.astype(BF16)
    lo = (x - hi.astype(F32)).astype(BF16)
    return hi, lo


def _split3(x):
    hi = x.astype(BF16)
    r1 = x - hi.astype(F32)
    mid = r1.astype(BF16)
    lo = (r1 - mid.astype(F32)).astype(BF16)
    return hi, mid, lo


def _dot_x3(a, b):
    ah, al = _split2(a)
    bh, bl = _split2(b)
    return _dot(ah, bh) + (_dot(ah, bl) + _dot(al, bh))


def _dot_exact_lhs(a_bf16, b):
    bh, bm, bl = _split3(b)
    return _dot(a_bf16, bh) + (_dot(a_bf16, bm) + _dot(a_bf16, bl))


def _sigmoid(x):
    return 1.0 / (1.0 + jnp.exp(-x))


def _rms_norm(x, g):
    return x * lax.rsqrt(jnp.mean(x * x, axis=-1, keepdims=True) + RMS_EPS) * g


def _head_sum(x, ones_bd):
    n = ones_bd.shape[0]
    xb = x.astype(BF16)
    return jnp.concatenate([jnp.dot(xb[:, j:j + n], ones_bd, preferred_element_type=F32)
                            for j in range(0, x.shape[1], n)], axis=1)


def _const_spec(shape):
    nd = len(shape)
    return pl.BlockSpec(shape, lambda *_: (0,) * nd, pipeline_mode=pl.Buffered(1))


def _params(n_grid):
    return pltpu.CompilerParams(dimension_semantics=("arbitrary",) * n_grid,
                                vmem_limit_bytes=VMEM_LIMIT)


def _norm_proj_kernel(x_ref, g_ref, wa_ref, wb_ref, wc_ref, oa_ref, ob_ref, oc_ref):
    h = _rms_norm(x_ref[...], g_ref[...]).astype(BF16)
    oa_ref[...] = jnp.dot(h, wa_ref[...], preferred_element_type=F32)
    ob_ref[...] = jnp.dot(h, wb_ref[...], preferred_element_type=F32)
    oc_ref[...] = jnp.dot(h, wc_ref[...], preferred_element_type=F32)


def _norm_proj(x, g, wa, wb, wc, tm):
    m = x.shape[0]
    row = lambda n: pl.BlockSpec((tm, n), lambda i: (i, 0))
    return pl.pallas_call(
        _norm_proj_kernel,
        grid=(m // tm,),
        in_specs=[row(D_MODEL), _const_spec((1, D_MODEL)), _const_spec(wa.shape),
                  _const_spec(wb.shape), _const_spec(wc.shape)],
        out_specs=[row(RWKV_PROJ), row(HGRN_PROJ), row(GATE_PROJ)],
        out_shape=[jax.ShapeDtypeStruct((m, RWKV_PROJ), F32),
                   jax.ShapeDtypeStruct((m, HGRN_PROJ), F32),
                   jax.ShapeDtypeStruct((m, GATE_PROJ), F32)],
        compiler_params=_params(1),
        name="norm_proj",
    )(x, g, wa, wb, wc)


def _rwkv_token_math(p, pprev, consts):
    mu_ref, w0_ref, w2_ref, a0_ref, a2_ref, g2_ref, kk_ref, ka_ref, rk_ref, ones_ref = consts
    W = RWKV_WIDTH
    ps = p + mu_ref[...] * (pprev - p)
    r = ps[:, 0:W]
    k = ps[:, W:2 * W]
    v = ps[:, 2 * W:3 * W]
    o0 = 3 * W
    wd = ps[:, o0:o0 + RWKV_DECAY_LORA]
    ad = ps[:, o0 + RWKV_DECAY_LORA:o0 + RWKV_DECAY_LORA + RWKV_A_LORA]
    gd = ps[:, o0 + RWKV_DECAY_LORA + RWKV_A_LORA:]
    z = -(w0_ref[...] + _dot_x3(jnp.tanh(wd), w2_ref[...]))
    softplus = jnp.maximum(z, 0.0) + jnp.log(1.0 + jnp.exp(-jnp.abs(z)))
    lw = -jnp.exp(-softplus - 0.5)
    a = _sigmoid(a0_ref[...] + _dot_x3(ad, a2_ref[...]))
    g = _dot_x3(_sigmoid(gd), g2_ref[...])
    ones_bd = ones_ref[...]
    kk = k * kk_ref[...]
    norm = jnp.sqrt(_head_sum(kk * kk, ones_bd))
    kk = kk / jnp.maximum(norm, L2_EPS)
    k = k * (1.0 + (a - 1.0) * ka_ref[...])
    bonus = _head_sum(r * k * rk_ref[...], ones_bd) * v
    return r, lw, k, v, kk, a, g, bonus


def _rwkv_prep_kernel(p_ref, pp_ref, *refs):
    consts, outs = refs[:10], refs[10:]
    for o_ref, val in zip(outs, _rwkv_token_math(p_ref[...], pp_ref[...], consts)):
        o_ref[...] = val


def _rwkv_prep(p, pprev, consts, tm):
    m = p.shape[0]
    row = lambda n: pl.BlockSpec((tm, n), lambda i: (i, 0))
    return pl.pallas_call(
        _rwkv_prep_kernel,
        grid=(m // tm,),
        in_specs=[row(RWKV_PROJ), row(RWKV_PROJ)] + [_const_spec(c.shape) for c in consts],
        out_specs=[row(RWKV_WIDTH)] * 8,
        out_shape=[jax.ShapeDtypeStruct((m, RWKV_WIDTH), F32)] * 8,
        compiler_params=_params(1),
        name="rwkv_prep",
    )(p, pprev, *consts)


def _rwkv_mix_kernel(p_ref, sh0_ref, mu_ref, w0_ref, w2_ref, a0_ref, a2_ref, g2_ref, kk_ref, ka_ref,
                     rk_ref, ones_ref, lng_ref, lnb_ref, s0_ref, o_ref, sT_ref,
                     s_scr, prev_scr, y_scr, *, nch):
    C = CHUNK
    tb = pl.program_id(1)

    @pl.when(tb == 0)
    def _():
        s_scr[...] = s0_ref[0]
        prev_scr[...] = sh0_ref[0]

    row = lax.broadcasted_iota(jnp.int32, (C, C), 0)
    col = lax.broadcasted_iota(jnp.int32, (C, C), 1)
    tril_incl = (col <= row).astype(BF16)
    PL = 2 * RWKV_HEAD
    t_row = lax.broadcasted_iota(jnp.int32, (C, 2 * PL), 0)
    s_col = lax.broadcasted_iota(jnp.int32, (C, 2 * PL), 1) & (RWKV_HEAD - 1)
    strict = (s_col < t_row)[:, :PL]
    incl = s_col <= t_row
    eye = jnp.where((s_col == t_row)[:, :PL], 1.0, 0.0)
    low_head = lax.broadcasted_iota(jnp.int32, (C, PL), 1) < RWKV_HEAD

    def bd(y):
        return jnp.concatenate([jnp.where(low_head, y, 0.0), jnp.where(low_head, 0.0, y)], axis=0)

    pairs = range(RWKV_HEADS // 2)
    psl = [slice(j * PL, (j + 1) * PL) for j in pairs]
    ch = [(c, j) for c in range(nch) for j in pairs]
    consts = (mu_ref, w0_ref, w2_ref, a0_ref, a2_ref, g2_ref, kk_ref, ka_ref, rk_ref, ones_ref)

    p = p_ref[0]
    rows = lax.broadcasted_iota(jnp.int32, p.shape, 0)
    pprev = jnp.where(rows == 0, prev_scr[...], pltpu.roll(p, 1, 0))
    prev_scr[...] = p[p.shape[0] - 1:, :]
    r_all, lw_all, k_all, v_all, kk_all, a_all, g_all, bonus_all = _rwkv_token_math(p, pprev, consts)
    kb_all = kk_all * a_all

    glast, AT, RT, VV, L, R, BK = {}, {}, {}, {}, {}, {}, {}
    for c in range(nch):
        sl = slice(c * C, (c + 1) * C)
        lw = lw_all[sl]
        k = k_all[sl]
        kk = kk_all[sl]
        kb = kb_all[sl]
        cum = _dot_exact_lhs(tril_incl, lw)
        cl = cum[C - 1:C, :]
        eneg = jnp.exp(-cum)
        elast = jnp.exp(cl - cum)
        glast[c] = jnp.exp(cl)
        at = -kk * jnp.exp(cum - lw)
        rt = r_all[sl] * jnp.exp(cum)
        bt = kb * eneg
        kt = k * eneg
        bh = kb * elast
        kh = k * elast
        v = v_all[sl]
        for j, ps in enumerate(psl):
            AT[c, j], RT[c, j], VV[c, j] = at[:, ps], rt[:, ps], v[:, ps]
            L[c, j] = jnp.concatenate([at[:, ps], rt[:, ps]], axis=0)
            R[c, j] = jnp.concatenate([bd(bt[:, ps]), bd(kt[:, ps])], axis=0)
            BK[c, j] = jnp.concatenate([bh[:, ps], kh[:, ps]], axis=0)
    P = {i: _dot_nt(L[i], R[i]) for i in ch}
    A = {i: jnp.where(strict, P[i][:C, :PL], 0.0) for i in ch}
    Aak = {i: jnp.where(strict, P[i][:C, PL:], 0.0) for i in ch}
    Ar = {i: jnp.where(incl, P[i][C:, :], 0.0) for i in ch}
    X = {i: _dot(A[i], bd(A[i])) for i in ch}
    Tm = {i: eye + A[i] for i in ch}
    for _ in range(4):
        Z = {i: _dot(X[i], jnp.concatenate([bd(X[i]), bd(Tm[i])], axis=1)) for i in ch}
        Tm = {i: Tm[i] + Z[i][:, PL:] for i in ch}
        X = {i: Z[i][:, :PL] for i in ch}
    AV = {i: _dot(Aak[i], bd(VV[i])) for i in ch}
    M = {i: _dot(Tm[i], jnp.concatenate([bd(AT[i]), bd(AV[i])], axis=1)) for i in ch}
    WU = {i: M[i] + _dot(X[i], jnp.concatenate([bd(M[i][:, :PL]), bd(M[i][:, PL:])], axis=1))
          for i in ch}
    WR = {i: jnp.concatenate([WU[i][:, :PL], RT[i]], axis=0) for i in ch}

    S = [s_scr[j] for j in pairs]
    for c in range(nch):
        sl = slice(c * C, (c + 1) * C)
        WRS = [_dot_nt(WR[c, j], bd(S[j])) for j in pairs]
        U = [WRS[j][:C] + WU[c, j][:, PL:] for j in pairs]
        UV = [jnp.concatenate([U[j], VV[c, j]], axis=0) for j in pairs]
        full = [_dot_tn(UV[j], BK[c, j]) for j in pairs]
        S = [S[j] * glast[c][:, psl[j]] + jnp.where(low_head, full[j][:C], full[j][C:]) for j in pairs]
        for j in pairs:
            y_scr[sl, psl[j]] = WRS[j][C:] + _dot(
                Ar[c, j], jnp.concatenate([bd(U[j]), bd(VV[c, j])], axis=0))
    for j in pairs:
        s_scr[j] = S[j]

    ones_bd = ones_ref[...]
    y = y_scr[...]
    inv_n = 1.0 / RWKV_HEAD
    d = y - _head_sum(y, ones_bd) * inv_n
    var = _head_sum(d * d, ones_bd) * inv_n
    o_ref[0] = (d * lax.rsqrt(var + RWKV_GN_EPS) * lng_ref[...] + lnb_ref[...] + bonus_all) * g_all

    @pl.when(tb == pl.num_programs(1) - 1)
    def _():
        sT_ref[0] = s_scr[...]


def _rwkv_mix(p, shift0, consts, s0, tt):
    b, t, _ = p.shape
    st = pl.BlockSpec((1,) + s0.shape[1:], lambda i, j: (i, 0, 0, 0))
    return pl.pallas_call(
        functools.partial(_rwkv_mix_kernel, nch=tt // CHUNK),
        grid=(b, t // tt),
        in_specs=[pl.BlockSpec((1, tt, RWKV_PROJ), lambda i, j: (i, j, 0)),
                  pl.BlockSpec((1, 1, RWKV_PROJ), lambda i, j: (i, 0, 0))]
                 + [_const_spec(c.shape) for c in consts] + [st],
        out_specs=[pl.BlockSpec((1, tt, RWKV_WIDTH), lambda i, j: (i, j, 0)), st],
        out_shape=[jax.ShapeDtypeStruct((b, t, RWKV_WIDTH), F32), jax.ShapeDtypeStruct(s0.shape, F32)],
        scratch_shapes=[pltpu.VMEM(s0.shape[1:], F32),
                        pltpu.VMEM((1, RWKV_PROJ), F32),
                        pltpu.VMEM((tt, RWKV_WIDTH), F32)],
        compiler_params=_params(2),
        name="rwkv_mix",
    )(p, shift0, *consts, s0)


def _rwkv_step_kernel(r_ref, lw_ref, k_ref, kk_ref, a_ref, v_ref, bonus_ref, g_ref, lng_ref, lnb_ref,
                      s_ref, o_ref, so_ref):
    H = s_ref[0]
    kk = kk_ref[0]
    sa = jnp.sum(H * (-kk), axis=0)
    Hn = H * jnp.exp(lw_ref[0]) + (kk * a_ref[0]) * sa[None] + k_ref[0] * v_ref[...][None]
    so_ref[0] = Hn
    y = jnp.sum(Hn * r_ref[0], axis=0)
    d = y - jnp.mean(y, axis=0, keepdims=True)
    var = jnp.mean(d * d, axis=0, keepdims=True)
    o_ref[...] = (d * lax.rsqrt(var + RWKV_GN_EPS) * lng_ref[...] + lnb_ref[...]
                  + bonus_ref[...]) * g_ref[...]


def _rwkv_step(r, lw, k, kk, a, v, bonus, g, lng, lnb, s):
    nb = s.shape[-1]
    kvec = pl.BlockSpec((1, RWKV_HEAD, 1, nb), lambda h: (h, 0, 0, 0))
    vvec = pl.BlockSpec((RWKV_HEAD, nb), lambda h: (h, 0))
    col = pl.BlockSpec((RWKV_HEAD, 1), lambda h: (h, 0))
    st = pl.BlockSpec((1, RWKV_HEAD, RWKV_HEAD, nb), lambda h: (h, 0, 0, 0))
    return pl.pallas_call(
        _rwkv_step_kernel,
        grid=(RWKV_HEADS,),
        in_specs=[kvec] * 5 + [vvec] * 3 + [col, col, st],
        out_specs=[vvec, st],
        out_shape=[jax.ShapeDtypeStruct((RWKV_WIDTH, nb), F32), jax.ShapeDtypeStruct(s.shape, F32)],
        compiler_params=_params(1),
        name="rwkv_step",
    )(r, lw, k, kk, a, v, bonus, g, lng, lnb, s)


def _hgrn_chunk_kernel(p_ref, lb_ref, ng_ref, s0_ref, o_ref, sT_ref, s_scr, *, nch):
    C = CHUNK
    SB = HGRN_SUB
    W = HGRN_WIDTH
    tb = pl.program_id(1)

    @pl.when(tb == 0)
    def _():
        s_scr[...] = s0_ref[0]

    tt = nch * C
    row = lax.broadcasted_iota(jnp.int32, (tt, tt), 0)
    col = lax.broadcasted_iota(jnp.int32, (tt, tt), 1)
    tril_bd = ((col <= row) & (col >= (row & -C))).astype(BF16)
    lb = lb_ref[...]
    ng = ng_ref[...]
    heads = range(HGRN_HEADS)
    chunks = range(nch)
    subs = range(C // SB)
    hsl = [slice(h * HGRN_HEAD, (h + 1) * HGRN_HEAD) for h in heads]

    q = p_ref[0, :, 0:W]
    q = q * _sigmoid(q)
    f = lb + (1.0 - lb) * _sigmoid(p_ref[0, :, W:2 * W])
    kx = 1.0 - f
    iv = p_ref[0, :, 2 * W:3 * W]
    bc = _dot_exact_lhs(tril_bd, jnp.log(f))
    qe = q * jnp.exp(bc)

    sc, kend, glast = {}, {}, {}
    for c in chunks:
        c0 = c * C
        bl = bc[c0 + C - 1:c0 + C, :]
        glast[c] = jnp.exp(bl)
        kend[c] = kx[c0:c0 + C] * jnp.exp(bl - bc[c0:c0 + C])
        for i in subs:
            lo, n = c0 + i * SB, c0 + (i + 1) * SB
            if i == 0:
                eq = jnp.exp(bc[lo:n])
                ek = jnp.exp(jnp.minimum(-bc[c0:n], EXP_CLAMP))
            else:
                beta = bc[lo - 1:lo]
                eq = jnp.exp(bc[lo:n] - beta)
                ek = jnp.exp(jnp.minimum(beta - bc[c0:n], EXP_CLAMP))
            qi = q[lo:n] * eq
            ks = kx[c0:n] * ek
            keep = (lax.broadcasted_iota(jnp.int32, (SB, n - c0), 1)
                    <= lax.broadcasted_iota(jnp.int32, (SB, n - c0), 0) + i * SB)
            for h in heads:
                sc[c, h, i] = jnp.where(keep, _dot_nt(qi[:, hsl[h]], ks[:, hsl[h]]), 0.0)
    intra = {(c, h): jnp.concatenate(
        [_dot(sc[c, h, i], iv[c * C:c * C + (i + 1) * SB, hsl[h]]) for i in subs], axis=0)
        for c in chunks for h in heads}
    upd = {(c, h): _dot_tn(iv[c * C:(c + 1) * C, hsl[h]], kend[c][:, hsl[h]])
           for c in chunks for h in heads}
    St = [s_scr[h] for h in heads]
    for c in chunks:
        sl = slice(c * C, (c + 1) * C)
        for h in heads:
            o = _dot_nt(qe[sl, hsl[h]], St[h]) + intra[c, h]
            o = o * lax.rsqrt(jnp.mean(o * o, axis=-1, keepdims=True) + RMS_EPS)
            o_ref[0, sl, hsl[h]] = o * ng[:, hsl[h]] * _sigmoid(p_ref[0, sl, 3 * W + h * HGRN_HEAD:
                                                                  3 * W + (h + 1) * HGRN_HEAD])
        St = [St[h] * glast[c][:, hsl[h]] + upd[c, h] for h in heads]
    for h in heads:
        s_scr[h] = St[h]

    @pl.when(tb == pl.num_programs(1) - 1)
    def _():
        sT_ref[0] = s_scr[...]


def _hgrn_chunk(p, lb, ng, s0, tt):
    b, t, _ = p.shape
    st = pl.BlockSpec((1, HGRN_HEADS, HGRN_HEAD, HGRN_HEAD), lambda i, j: (i, 0, 0, 0))
    return pl.pallas_call(
        functools.partial(_hgrn_chunk_kernel, nch=tt // CHUNK),
        grid=(b, t // tt),
        in_specs=[pl.BlockSpec((1, tt, HGRN_PROJ), lambda i, j: (i, j, 0)),
                  _const_spec((1, HGRN_WIDTH)), _const_spec((1, HGRN_WIDTH)), st],
        out_specs=[pl.BlockSpec((1, tt, HGRN_WIDTH), lambda i, j: (i, j, 0)), st],
        out_shape=[jax.ShapeDtypeStruct((b, t, HGRN_WIDTH), F32),
                   jax.ShapeDtypeStruct(s0.shape, F32)],
        scratch_shapes=[pltpu.VMEM((HGRN_HEADS, HGRN_HEAD, HGRN_HEAD), F32)],
        compiler_params=_params(2),
        name="hgrn_chunk",
    )(p, lb, ng, s0)


def _hgrn_step_kernel(q_ref, fl_ref, i_ref, og_ref, lb_ref, ng_ref, s_ref, o_ref, so_ref):
    q = q_ref[0]
    q = q * _sigmoid(q)
    lb = lb_ref[0]
    f = lb + (1.0 - lb) * _sigmoid(fl_ref[0])
    Sn = s_ref[0] * f + (1.0 - f) * i_ref[...][None]
    so_ref[0] = Sn
    o = jnp.sum(Sn * q, axis=0)
    o = o * lax.rsqrt(jnp.mean(o * o, axis=0, keepdims=True) + RMS_EPS)
    o_ref[...] = o * ng_ref[...] * _sigmoid(og_ref[...])


def _hgrn_step(q, fl, iv, og, lb, ng, s):
    nb = s.shape[-1]
    kvec = pl.BlockSpec((1, HGRN_HEAD, 1, nb), lambda h: (h, 0, 0, 0))
    vvec = pl.BlockSpec((HGRN_HEAD, nb), lambda h: (h, 0))
    st = pl.BlockSpec((1, HGRN_HEAD, HGRN_HEAD, nb), lambda h: (h, 0, 0, 0))
    return pl.pallas_call(
        _hgrn_step_kernel,
        grid=(HGRN_HEADS,),
        in_specs=[kvec, kvec, vvec, vvec,
                  pl.BlockSpec((1, HGRN_HEAD, 1, 1), lambda h: (h, 0, 0, 0)),
                  pl.BlockSpec((HGRN_HEAD, 1), lambda h: (h, 0)), st],
        out_specs=[vvec, st],
        out_shape=[jax.ShapeDtypeStruct((HGRN_WIDTH, nb), F32), jax.ShapeDtypeStruct(s.shape, F32)],
        compiler_params=_params(1),
        name="hgrn_step",
    )(q, fl, iv, og, lb, ng, s)


def _merge_ffn_kernel(x_ref, oa_ref, ob_ref, pg_ref, wa_ref, wb_ref, wo_ref, nf_ref, wg_ref, wu_ref,
                      wd_ref, nfin_ref, out_ref):
    pg = pg_ref[...]
    merged = (_sigmoid(pg[:, :D_MODEL]) * _dot(oa_ref[...], wa_ref[...])
              + _sigmoid(pg[:, D_MODEL:]) * _dot(ob_ref[...], wb_ref[...]))
    x = x_ref[...] + _dot(merged, wo_ref[...])
    h = _rms_norm(x, nf_ref[...]).astype(BF16)
    gate = jnp.dot(h, wg_ref[...], preferred_element_type=F32)
    up = jnp.dot(h, wu_ref[...], preferred_element_type=F32)
    x = x + _dot(gate * _sigmoid(gate) * up, wd_ref[...])
    out_ref[...] = _rms_norm(x, nfin_ref[...])


def _merge_ffn(x, oa, ob, pg, consts, tm):
    m = x.shape[0]
    row = lambda n: pl.BlockSpec((tm, n), lambda i: (i, 0))
    return pl.pallas_call(
        _merge_ffn_kernel,
        grid=(m // tm,),
        in_specs=[row(D_MODEL), row(RWKV_WIDTH), row(HGRN_WIDTH), row(GATE_PROJ)]
                 + [_const_spec(c.shape) for c in consts],
        out_specs=row(D_MODEL),
        out_shape=jax.ShapeDtypeStruct((m, D_MODEL), F32),
        compiler_params=_params(1),
        name="merge_ffn",
    )(x, oa, ob, pg, *consts)


def kernel(x_prompt, x_sample, state_rwkv_wkv, state_rwkv_shift, state_hgrn, norm_mix_g, w_in, rwkv_mu, rwkv_w0, rwkv_w2, rwkv_a0, rwkv_a2, rwkv_g2, rwkv_k_k, rwkv_k_a, rwkv_r_k, rwkv_ln_g, rwkv_ln_b, w_up_a, hgrn_lb, hgrn_norm_g, w_up_b, w_out, norm_ffn_g, w_ffn_gate, w_ffn_up, w_ffn_down, norm_final_g):
    bp, tp, _ = x_prompt.shape
    bs = x_sample.shape[0]
    mp = bp * tp

    w_in0 = w_in[0]
    wa_in = w_in0[:, :RWKV_PROJ].astype(BF16)
    wb_in = w_in0[:, RWKV_PROJ:RWKV_PROJ + HGRN_PROJ].astype(BF16)
    wc_in = w_in0[:, RWKV_PROJ + HGRN_PROJ:].astype(BF16)
    vec = lambda p: p.reshape(1, -1).astype(F32)
    head_id = jnp.arange(MXU_WIDTH, dtype=jnp.int32) // RWKV_HEAD
    ones_bd = (head_id[:, None] == head_id[None, :]).astype(BF16)
    prep_consts = (vec(rwkv_mu[0]), vec(rwkv_w0[0]), rwkv_w2[0], vec(rwkv_a0[0]), rwkv_a2[0],
                   rwkv_g2[0], vec(rwkv_k_k[0]), vec(rwkv_k_a[0]), vec(rwkv_r_k[0]), ones_bd)
    lb = jnp.cumsum(jax.nn.softmax(hgrn_lb.astype(F32), axis=0), axis=0)[0]
    ffn_consts = (w_up_a[0].astype(BF16), w_up_b[0].astype(BF16), w_out[0].astype(BF16),
                  vec(norm_ffn_g[0]), w_ffn_gate[0].astype(BF16), w_ffn_up[0].astype(BF16),
                  w_ffn_down[0].astype(BF16), vec(norm_final_g))
    g_mix = vec(norm_mix_g[0])

    xp = x_prompt.reshape(mp, D_MODEL)
    p_rwkv, p_hgrn, p_gate = _norm_proj(xp, g_mix, wa_in, wb_in, wc_in, tm=512)
    p3 = p_rwkv.reshape(bp, tp, RWKV_PROJ)
    o_a, wkv_pp = _rwkv_mix(p3, jnp.zeros((bp, 1, RWKV_PROJ), F32),
                            prep_consts + (vec(rwkv_ln_g[0]), vec(rwkv_ln_b[0])),
                            jnp.zeros((bp, RWKV_HEADS // 2, RWKV_HEAD, 2 * RWKV_HEAD), F32), tt=256)
    wkv_p = (wkv_pp.reshape(bp, RWKV_HEADS // 2, RWKV_HEAD, 2, RWKV_HEAD)
             .transpose(0, 1, 3, 2, 4).reshape(bp, RWKV_HEADS, RWKV_HEAD, RWKV_HEAD))
    hg0 = jnp.zeros((bp, HGRN_HEADS, HGRN_HEAD, HGRN_HEAD), F32)
    o_b, hgT_p = _hgrn_chunk(p_hgrn.reshape(bp, tp, HGRN_PROJ), vec(lb), vec(hgrn_norm_g[0]), hg0,
                             tt=256)
    y_prompt = _merge_ffn(xp, o_a.reshape(mp, RWKV_WIDTH), o_b.reshape(mp, HGRN_WIDTH), p_gate,
                          ffn_consts, tm=512).reshape(bp, tp, D_MODEL)
    shift_p = p3[:, -1]
    hgrn_p = jnp.swapaxes(hgT_p, -1, -2)

    xs = x_sample.reshape(bs, D_MODEL)
    s_rwkv, s_hgrn, s_gate = _norm_proj(xs, g_mix, wa_in, wb_in, wc_in, tm=bs)
    r, lw, k, v, kk, a, g, bonus = _rwkv_prep(s_rwkv, state_rwkv_shift[0], prep_consts, tm=bs)
    kcol = lambda t: t.T.reshape(RWKV_HEADS, RWKV_HEAD, 1, bs)
    wkv_t = jnp.transpose(state_rwkv_wkv[0], (1, 3, 2, 0))
    oaT, wkv_nt = _rwkv_step(kcol(r), kcol(lw), kcol(k), kcol(kk), kcol(a), v.T, bonus.T, g.T,
                             rwkv_ln_g[0].reshape(RWKV_WIDTH, 1), rwkv_ln_b[0].reshape(RWKV_WIDTH, 1),
                             wkv_t)
    wkv_s = jnp.transpose(wkv_nt, (3, 0, 2, 1))
    hT = s_hgrn.T
    hcol = lambda t: t.reshape(HGRN_HEADS, HGRN_HEAD, 1, bs)
    hg_t = jnp.transpose(state_hgrn[0], (1, 2, 3, 0))
    obT, hg_nt = _hgrn_step(hcol(hT[:HGRN_WIDTH]), hcol(hT[HGRN_WIDTH:2 * HGRN_WIDTH]),
                            hT[2 * HGRN_WIDTH:3 * HGRN_WIDTH], hT[3 * HGRN_WIDTH:],
                            lb.reshape(HGRN_HEADS, HGRN_HEAD, 1, 1),
                            hgrn_norm_g[0].reshape(HGRN_WIDTH, 1), hg_t)
    hgrn_s = jnp.transpose(hg_nt, (3, 0, 1, 2))
    y_sample = _merge_ffn(xs, oaT.T, obT.T, s_gate, ffn_consts, tm=bs).reshape(bs, 1, D_MODEL)

    return (y_prompt, y_sample, wkv_p[None], shift_p[None], hgrn_p[None],
            wkv_s[None], s_rwkv[None], hgrn_s[None])
```

```python
import functools

import jax
import jax.numpy as jnp
from jax import lax
from jax.experimental import pallas as pl
from jax.experimental.pallas import tpu as pltpu

F32 = jnp.float32
BF16 = jnp.bfloat16

D_MODEL = 1024
RWKV_WIDTH = 512
RWKV_HEAD = 64
RWKV_HEADS = 8
RWKV_DECAY_LORA = 64
RWKV_A_LORA = 64
RWKV_GATE_LORA = 128
RWKV_PROJ = 3 * RWKV_WIDTH + RWKV_DECAY_LORA + RWKV_A_LORA + RWKV_GATE_LORA
RWKV_GN_EPS = 64e-5
L2_EPS = 1e-12
HGRN_WIDTH = 512
HGRN_HEADS = 4
HGRN_HEAD = 128
HGRN_PROJ = 4 * HGRN_WIDTH
GATE_PROJ = 2 * D_MODEL
D_FF = 2816
RMS_EPS = 1e-6

MXU_WIDTH = 256
CHUNK = 64
HGRN_SUB = 16
EXP_CLAMP = 80.0
VMEM_LIMIT = 56 * 1024 * 1024


def _dot(a, b):
    return jnp.dot(a.astype(BF16), b.astype(BF16), preferred_element_type=F32)


def _dot_nt(a, b):
    return lax.dot_general(a.astype(BF16), b.astype(BF16), (((1,), (1,)), ((), ())),
                           preferred_element_type=F32)


def _dot_tn(a, b):
    return lax.dot_general(a.astype(BF16), b.astype(BF16), (((0,), (0,)), ((), ())),
                           preferred_element_type=F32)


def _split2(x):
    hi = x.astype(BF16)
    lo = (x - hi.astype(F32)).astype(BF16)
    return hi, lo


def _split3(x):
    hi = x.astype(BF16)
    r1 = x - hi.astype(F32)
    mid = r1.astype(BF16)
    lo = (r1 - mid.astype(F32)).astype(BF16)
    return hi, mid, lo


def _dot_x3(a, b):
    ah, al = _split2(a)
    bh, bl = _split2(b)
    return _dot(ah, bh) + (_dot(ah, bl) + _dot(al, bh))


def _dot_exact_lhs(a_bf16, b):
    bh, bm, bl = _split3(b)
    return _dot(a_bf16, bh) + (_dot(a_bf16, bm) + _dot(a_bf16, bl))


def _sigmoid(x):
    return 1.0 / (1.0 + jnp.exp(-x))


def _rms_norm(x, g):
    return x * lax.rsqrt(jnp.mean(x * x, axis=-1, keepdims=True) + RMS_EPS) * g


def _head_sum(x, ones_bd):
    n = ones_bd.shape[0]
    xb = x.astype(BF16)
    return jnp.concatenate([jnp.dot(xb[:, j:j + n], ones_bd, preferred_element_type=F32)
                            for j in range(0, x.shape[1], n)], axis=1)


def _const_spec(shape):
    nd = len(shape)
    return pl.BlockSpec(shape, lambda *_: (0,) * nd, pipeline_mode=pl.Buffered(1))


def _params(n_grid):
    return pltpu.CompilerParams(dimension_semantics=("arbitrary",) * n_grid,
                                vmem_limit_bytes=VMEM_LIMIT)


def _norm_proj_kernel(x_ref, g_ref, wa_ref, wb_ref, wc_ref, oa_ref, ob_ref, oc_ref):
    h = _rms_norm(x_ref[...], g_ref[...]).astype(BF16)
    oa_ref[...] = jnp.dot(h, wa_ref[...], preferred_element_type=F32)
    ob_ref[...] = jnp.dot(h, wb_ref[...], preferred_element_type=F32)
    oc_ref[...] = jnp.dot(h, wc_ref[...], preferred_element_type=F32)


def _norm_proj(x, g, wa, wb, wc, tm):
    m = x.shape[0]
    row = lambda n: pl.BlockSpec((tm, n), lambda i: (i, 0))
    return pl.pallas_call(
        _norm_proj_kernel,
        grid=(m // tm,),
        in_specs=[row(D_MODEL), _const_spec((1, D_MODEL)), _const_spec(wa.shape),
                  _const_spec(wb.shape), _const_spec(wc.shape)],
        out_specs=[row(RWKV_PROJ), row(HGRN_PROJ), row(GATE_PROJ)],
        out_shape=[jax.ShapeDtypeStruct((m, RWKV_PROJ), F32),
                   jax.ShapeDtypeStruct((m, HGRN_PROJ), F32),
                   jax.ShapeDtypeStruct((m, GATE_PROJ), F32)],
        compiler_params=_params(1),
        name="norm_proj",
    )(x, g, wa, wb, wc)


def _rwkv_token_math(p, pprev, consts):
    mu_ref, w0_ref, w2_ref, a0_ref, a2_ref, g2_ref, kk_ref, ka_ref, rk_ref, ones_ref = consts
    W = RWKV_WIDTH
    ps = p + mu_ref[...] * (pprev - p)
    r = ps[:, 0:W]
    k = ps[:, W:2 * W]
    v = ps[:, 2 * W:3 * W]
    o0 = 3 * W
    wd = ps[:, o0:o0 + RWKV_DECAY_LORA]
    ad = ps[:, o0 + RWKV_DECAY_LORA:o0 + RWKV_DECAY_LORA + RWKV_A_LORA]
    gd = ps[:, o0 + RWKV_DECAY_LORA + RWKV_A_LORA:]
    z = -(w0_ref[...] + _dot_x3(jnp.tanh(wd), w2_ref[...]))
    softplus = jnp.maximum(z, 0.0) + jnp.log(1.0 + jnp.exp(-jnp.abs(z)))
    lw = -jnp.exp(-softplus - 0.5)
    a = _sigmoid(a0_ref[...] + _dot_x3(ad, a2_ref[...]))
    g = _dot_x3(_sigmoid(gd), g2_ref[...])
    ones_bd = ones_ref[...]
    kk = k * kk_ref[...]
    norm = jnp.sqrt(_head_sum(kk * kk, ones_bd))
    kk = kk / jnp.maximum(norm, L2_EPS)
    k = k * (1.0 + (a - 1.0) * ka_ref[...])
    bonus = _head_sum(r * k * rk_ref[...], ones_bd) * v
    return r, lw, k, v, kk, a, g, bonus


def _rwkv_prep_kernel(p_ref, pp_ref, *refs):
    consts, outs = refs[:10], refs[10:]
    for o_ref, val in zip(outs, _rwkv_token_math(p_ref[...], pp_ref[...], consts)):
        o_ref[...] = val


def _rwkv_prep(p, pprev, consts, tm):
    m = p.shape[0]
    row = lambda n: pl.BlockSpec((tm, n), lambda i: (i, 0))
    return pl.pallas_call(
        _rwkv_prep_kernel,
        grid=(m // tm,),
        in_specs=[row(RWKV_PROJ), row(RWKV_PROJ)] + [_const_spec(c.shape) for c in consts],
        out_specs=[row(RWKV_WIDTH)] * 8,
        out_shape=[jax.ShapeDtypeStruct((m, RWKV_WIDTH), F32)] * 8,
        compiler_params=_params(1),
        name="rwkv_prep",
    )(p, pprev, *consts)


def _rwkv_mix_kernel(p_ref, sh0_ref, mu_ref, w0_ref, w2_ref, a0_ref, a2_ref, g2_ref, kk_ref, ka_ref,
                     rk_ref, ones_ref, lng_ref, lnb_ref, s0_ref, o_ref, sT_ref,
                     s_scr, prev_scr, y_scr, *, nch):
    C = CHUNK
    tb = pl.program_id(1)

    @pl.when(tb == 0)
    def _():
        s_scr[...] = s0_ref[...]
        prev_scr[...] = sh0_ref[...]

    row = lax.broadcasted_iota(jnp.int32, (C, C), 0)
    col = lax.broadcasted_iota(jnp.int32, (C, C), 1)
    tril_incl = (col <= row).astype(BF16)
    PL = 2 * RWKV_HEAD
    t_row = lax.broadcasted_iota(jnp.int32, (C, 2 * PL), 0)
    s_col = lax.broadcasted_iota(jnp.int32, (C, 2 * PL), 1) & (RWKV_HEAD - 1)
    strict = (s_col < t_row)[:, :PL]
    incl = s_col <= t_row
    eye = jnp.where((s_col == t_row)[:, :PL], 1.0, 0.0)
    low_head = lax.broadcasted_iota(jnp.int32, (C, PL), 1) < RWKV_HEAD

    def bd(y):
        return jnp.concatenate([jnp.where(low_head, y, 0.0), jnp.where(low_head, 0.0, y)], axis=0)

    pairs = range(RWKV_HEADS // 2)
    psl = [slice(j * PL, (j + 1) * PL) for j in pairs]
    seqs = range(p_ref.shape[0])
    ch = [(b, c, j) for b in seqs for c in range(nch) for j in pairs]
    consts = (mu_ref, w0_ref, w2_ref, a0_ref, a2_ref, g2_ref, kk_ref, ka_ref, rk_ref, ones_ref)

    glast, AT, RT, VV, L, R, BK, gate, bonus = {}, {}, {}, {}, {}, {}, {}, {}, {}
    for b in seqs:
        p = p_ref[b]
        rows = lax.broadcasted_iota(jnp.int32, p.shape, 0)
        pprev = jnp.where(rows == 0, prev_scr[b], pltpu.roll(p, 1, 0))
        prev_scr[b] = p[p.shape[0] - 1:, :]
        r_all, lw_all, k_all, v_all, kk_all, a_all, gate[b], bonus[b] = _rwkv_token_math(p, pprev, consts)
        kb_all = kk_all * a_all
        for c in range(nch):
            sl = slice(c * C, (c + 1) * C)
            lw = lw_all[sl]
            k = k_all[sl]
            kk = kk_all[sl]
            kb = kb_all[sl]
            cum = _dot_exact_lhs(tril_incl, lw)
            cl = cum[C - 1:C, :]
            eneg = jnp.exp(-cum)
            elast = jnp.exp(cl - cum)
            glast[b, c] = jnp.exp(cl)
            at = -kk * jnp.exp(cum - lw)
            rt = r_all[sl] * jnp.exp(cum)
            bt = kb * eneg
            kt = k * eneg
            bh = kb * elast
            kh = k * elast
            v = v_all[sl]
            for j, ps in enumerate(psl):
                AT[b, c, j], RT[b, c, j], VV[b, c, j] = at[:, ps], rt[:, ps], v[:, ps]
                L[b, c, j] = jnp.concatenate([at[:, ps], rt[:, ps]], axis=0)
                R[b, c, j] = jnp.concatenate([bd(bt[:, ps]), bd(kt[:, ps])], axis=0)
                BK[b, c, j] = jnp.concatenate([bh[:, ps], kh[:, ps]], axis=0)
    P = {i: _dot_nt(L[i], R[i]) for i in ch}
    A = {i: jnp.where(strict, P[i][:C, :PL], 0.0) for i in ch}
    Aak = {i: jnp.where(strict, P[i][:C, PL:], 0.0) for i in ch}
    Ar = {i: jnp.where(incl, P[i][C:, :], 0.0) for i in ch}
    X = {i: _dot(A[i], bd(A[i])) for i in ch}
    Tm = {i: eye + A[i] for i in ch}
    for _ in range(4):
        Z = {i: _dot(X[i], jnp.concatenate([bd(X[i]), bd(Tm[i])], axis=1)) for i in ch}
        Tm = {i: Tm[i] + Z[i][:, PL:] for i in ch}
        X = {i: Z[i][:, :PL] for i in ch}
    AV = {i: _dot(Aak[i], bd(VV[i])) for i in ch}
    M = {i: _dot(Tm[i], jnp.concatenate([bd(AT[i]), bd(AV[i])], axis=1)) for i in ch}
    WU = {i: M[i] + _dot(X[i], jnp.concatenate([bd(M[i][:, :PL]), bd(M[i][:, PL:])], axis=1))
          for i in ch}
    WR = {i: jnp.concatenate([WU[i][:, :PL], RT[i]], axis=0) for i in ch}

    bj = [(b, j) for b in seqs for j in pairs]
    S = {(b, j): s_scr[b, j] for b, j in bj}
    for c in range(nch):
        sl = slice(c * C, (c + 1) * C)
        WRS = {(b, j): _dot_nt(WR[b, c, j], bd(S[b, j])) for b, j in bj}
        U = {(b, j): WRS[b, j][:C] + WU[b, c, j][:, PL:] for b, j in bj}
        UV = {(b, j): jnp.concatenate([U[b, j], VV[b, c, j]], axis=0) for b, j in bj}
        full = {(b, j): _dot_tn(UV[b, j], BK[b, c, j]) for b, j in bj}
        S = {(b, j): S[b, j] * glast[b, c][:, psl[j]]
             + jnp.where(low_head, full[b, j][:C], full[b, j][C:]) for b, j in bj}
        for b, j in bj:
            y_scr[b, sl, psl[j]] = WRS[b, j][C:] + _dot(
                Ar[b, c, j], jnp.concatenate([bd(U[b, j]), bd(VV[b, c, j])], axis=0))
    for b, j in bj:
        s_scr[b, j] = S[b, j]

    ones_bd = ones_ref[...]
    inv_n = 1.0 / RWKV_HEAD
    for b in seqs:
        y = y_scr[b]
        d = y - _head_sum(y, ones_bd) * inv_n
        var = _head_sum(d * d, ones_bd) * inv_n
        o_ref[b] = (d * lax.rsqrt(var + RWKV_GN_EPS) * lng_ref[...] + lnb_ref[...] + bonus[b]) * gate[b]

    @pl.when(tb == pl.num_programs(1) - 1)
    def _():
        sT_ref[...] = s_scr[...]


def _rwkv_mix(p, shift0, consts, s0, tt, nb):
    b, t, _ = p.shape
    st = pl.BlockSpec((nb,) + s0.shape[1:], lambda i, j: (i, 0, 0, 0))
    return pl.pallas_call(
        functools.partial(_rwkv_mix_kernel, nch=tt // CHUNK),
        grid=(b // nb, t // tt),
        in_specs=[pl.BlockSpec((nb, tt, RWKV_PROJ), lambda i, j: (i, j, 0)),
                  pl.BlockSpec((nb, 1, RWKV_PROJ), lambda i, j: (i, 0, 0))]
                 + [_const_spec(c.shape) for c in consts] + [st],
        out_specs=[pl.BlockSpec((nb, tt, RWKV_WIDTH), lambda i, j: (i, j, 0)), st],
        out_shape=[jax.ShapeDtypeStruct((b, t, RWKV_WIDTH), F32), jax.ShapeDtypeStruct(s0.shape, F32)],
        scratch_shapes=[pltpu.VMEM((nb,) + s0.shape[1:], F32),
                        pltpu.VMEM((nb, 1, RWKV_PROJ), F32),
                        pltpu.VMEM((nb, tt, RWKV_WIDTH), F32)],
        compiler_params=_params(2),
        name="rwkv_mix",
    )(p, shift0, *consts, s0)


def _rwkv_step_kernel(r_ref, lw_ref, k_ref, kk_ref, a_ref, v_ref, bonus_ref, g_ref, lng_ref, lnb_ref,
                      s_ref, o_ref, so_ref):
    H = s_ref[0]
    kk = kk_ref[0]
    sa = jnp.sum(H * (-kk), axis=0)
    Hn = H * jnp.exp(lw_ref[0]) + (kk * a_ref[0]) * sa[None] + k_ref[0] * v_ref[...][None]
    so_ref[0] = Hn
    y = jnp.sum(Hn * r_ref[0], axis=0)
    d = y - jnp.mean(y, axis=0, keepdims=True)
    var = jnp.mean(d * d, axis=0, keepdims=True)
    o_ref[...] = (d * lax.rsqrt(var + RWKV_GN_EPS) * lng_ref[...] + lnb_ref[...]
                  + bonus_ref[...]) * g_ref[...]


def _rwkv_step(r, lw, k, kk, a, v, bonus, g, lng, lnb, s):
    nb = s.shape[-1]
    kvec = pl.BlockSpec((1, RWKV_HEAD, 1, nb), lambda h: (h, 0, 0, 0))
    vvec = pl.BlockSpec((RWKV_HEAD, nb), lambda h: (h, 0))
    col = pl.BlockSpec((RWKV_HEAD, 1), lambda h: (h, 0))
    st = pl.BlockSpec((1, RWKV_HEAD, RWKV_HEAD, nb), lambda h: (h, 0, 0, 0))
    return pl.pallas_call(
        _rwkv_step_kernel,
        grid=(RWKV_HEADS,),
        in_specs=[kvec] * 5 + [vvec] * 3 + [col, col, st],
        out_specs=[vvec, st],
        out_shape=[jax.ShapeDtypeStruct((RWKV_WIDTH, nb), F32), jax.ShapeDtypeStruct(s.shape, F32)],
        compiler_params=_params(1),
        name="rwkv_step",
    )(r, lw, k, kk, a, v, bonus, g, lng, lnb, s)


def _hgrn_chunk_kernel(p_ref, lb_ref, ng_ref, s0_ref, o_ref, sT_ref, s_scr, *, nch):
    C = CHUNK
    SB = HGRN_SUB
    W = HGRN_WIDTH
    tb = pl.program_id(1)

    @pl.when(tb == 0)
    def _():
        s_scr[...] = s0_ref[0]

    tt = nch * C
    row = lax.broadcasted_iota(jnp.int32, (tt, tt), 0)
    col = lax.broadcasted_iota(jnp.int32, (tt, tt), 1)
    tril_bd = ((col <= row) & (col >= (row & -C))).astype(BF16)
    lb = lb_ref[...]
    ng = ng_ref[...]
    heads = range(HGRN_HEADS)
    chunks = range(nch)
    subs = range(C // SB)
    hsl = [slice(h * HGRN_HEAD, (h + 1) * HGRN_HEAD) for h in heads]

    q = p_ref[0, :, 0:W]
    q = q * _sigmoid(q)
    f = lb + (1.0 - lb) * _sigmoid(p_ref[0, :, W:2 * W])
    kx = 1.0 - f
    iv = p_ref[0, :, 2 * W:3 * W]
    bc = _dot_exact_lhs(tril_bd, jnp.log(f))
    qe = q * jnp.exp(bc)

    sc, kend, glast = {}, {}, {}
    for c in chunks:
        c0 = c * C
        bl = bc[c0 + C - 1:c0 + C, :]
        glast[c] = jnp.exp(bl)
        kend[c] = kx[c0:c0 + C] * jnp.exp(bl - bc[c0:c0 + C])
        for i in subs:
            lo, n = c0 + i * SB, c0 + (i + 1) * SB
            if i == 0:
                eq = jnp.exp(bc[lo:n])
                ek = jnp.exp(jnp.minimum(-bc[c0:n], EXP_CLAMP))
            else:
                beta = bc[lo - 1:lo]
                eq = jnp.exp(bc[lo:n] - beta)
                ek = jnp.exp(jnp.minimum(beta - bc[c0:n], EXP_CLAMP))
            qi = q[lo:n] * eq
            ks = kx[c0:n] * ek
            keep = (lax.broadcasted_iota(jnp.int32, (SB, n - c0), 1)
                    <= lax.broadcasted_iota(jnp.int32, (SB, n - c0), 0) + i * SB)
            for h in heads:
                sc[c, h, i] = jnp.where(keep, _dot_nt(qi[:, hsl[h]], ks[:, hsl[h]]), 0.0)
    intra = {(c, h): jnp.concatenate(
        [_dot(sc[c, h, i], iv[c * C:c * C + (i + 1) * SB, hsl[h]]) for i in subs], axis=0)
        for c in chunks for h in heads}
    upd = {(c, h): _dot_tn(iv[c * C:(c + 1) * C, hsl[h]], kend[c][:, hsl[h]])
           for c in chunks for h in heads}
    St = [s_scr[h] for h in heads]
    for c in chunks:
        sl = slice(c * C, (c + 1) * C)
        for h in heads:
            o = _dot_nt(qe[sl, hsl[h]], St[h]) + intra[c, h]
            o = o * lax.rsqrt(jnp.mean(o * o, axis=-1, keepdims=True) + RMS_EPS)
            o_ref[0, sl, hsl[h]] = o * ng[:, hsl[h]] * _sigmoid(p_ref[0, sl, 3 * W + h * HGRN_HEAD:
                                                                  3 * W + (h + 1) * HGRN_HEAD])
        St = [St[h] * glast[c][:, hsl[h]] + upd[c, h] for h in heads]
    for h in heads:
        s_scr[h] = St[h]

    @pl.when(tb == pl.num_programs(1) - 1)
    def _():
        sT_ref[0] = s_scr[...]


def _hgrn_chunk(p, lb, ng, s0, tt):
    b, t, _ = p.shape
    st = pl.BlockSpec((1, HGRN_HEADS, HGRN_HEAD, HGRN_HEAD), lambda i, j: (i, 0, 0, 0))
    return pl.pallas_call(
        functools.partial(_hgrn_chunk_kernel, nch=tt // CHUNK),
        grid=(b, t // tt),
        in_specs=[pl.BlockSpec((1, tt, HGRN_PROJ), lambda i, j: (i, j, 0)),
                  _const_spec((1, HGRN_WIDTH)), _const_spec((1, HGRN_WIDTH)), st],
        out_specs=[pl.BlockSpec((1, tt, HGRN_WIDTH), lambda i, j: (i, j, 0)), st],
        out_shape=[jax.ShapeDtypeStruct((b, t, HGRN_WIDTH), F32),
                   jax.ShapeDtypeStruct(s0.shape, F32)],
        scratch_shapes=[pltpu.VMEM((HGRN_HEADS, HGRN_HEAD, HGRN_HEAD), F32)],
        compiler_params=_params(2),
        name="hgrn_chunk",
    )(p, lb, ng, s0)


def _hgrn_step_kernel(q_ref, fl_ref, i_ref, og_ref, lb_ref, ng_ref, s_ref, o_ref, so_ref):
    q = q_ref[0]
    q = q * _sigmoid(q)
    lb = lb_ref[0]
    f = lb + (1.0 - lb) * _sigmoid(fl_ref[0])
    Sn = s_ref[0] * f + (1.0 - f) * i_ref[...][None]
    so_ref[0] = Sn
    o = jnp.sum(Sn * q, axis=0)
    o = o * lax.rsqrt(jnp.mean(o * o, axis=0, keepdims=True) + RMS_EPS)
    o_ref[...] = o * ng_ref[...] * _sigmoid(og_ref[...])


def _hgrn_step(q, fl, iv, og, lb, ng, s):
    nb = s.shape[-1]
    kvec = pl.BlockSpec((1, HGRN_HEAD, 1, nb), lambda h: (h, 0, 0, 0))
    vvec = pl.BlockSpec((HGRN_HEAD, nb), lambda h: (h, 0))
    st = pl.BlockSpec((1, HGRN_HEAD, HGRN_HEAD, nb), lambda h: (h, 0, 0, 0))
    return pl.pallas_call(
        _hgrn_step_kernel,
        grid=(HGRN_HEADS,),
        in_specs=[kvec, kvec, vvec, vvec,
                  pl.BlockSpec((1, HGRN_HEAD, 1, 1), lambda h: (h, 0, 0, 0)),
                  pl.BlockSpec((HGRN_HEAD, 1), lambda h: (h, 0)), st],
        out_specs=[vvec, st],
        out_shape=[jax.ShapeDtypeStruct((HGRN_WIDTH, nb), F32), jax.ShapeDtypeStruct(s.shape, F32)],
        compiler_params=_params(1),
        name="hgrn_step",
    )(q, fl, iv, og, lb, ng, s)


def _merge_ffn_kernel(x_ref, oa_ref, ob_ref, pg_ref, wa_ref, wb_ref, wo_ref, nf_ref, wg_ref, wu_ref,
                      wd_ref, nfin_ref, out_ref):
    pg = pg_ref[...]
    merged = (_sigmoid(pg[:, :D_MODEL]) * _dot(oa_ref[...], wa_ref[...])
              + _sigmoid(pg[:, D_MODEL:]) * _dot(ob_ref[...], wb_ref[...]))
    x = x_ref[...] + _dot(merged, wo_ref[...])
    h = _rms_norm(x, nf_ref[...]).astype(BF16)
    gate = jnp.dot(h, wg_ref[...], preferred_element_type=F32)
    up = jnp.dot(h, wu_ref[...], preferred_element_type=F32)
    x = x + _dot(gate * _sigmoid(gate) * up, wd_ref[...])
    out_ref[...] = _rms_norm(x, nfin_ref[...])


def _merge_ffn(x, oa, ob, pg, consts, tm):
    m = x.shape[0]
    row = lambda n: pl.BlockSpec((tm, n), lambda i: (i, 0))
    return pl.pallas_call(
        _merge_ffn_kernel,
        grid=(m // tm,),
        in_specs=[row(D_MODEL), row(RWKV_WIDTH), row(HGRN_WIDTH), row(GATE_PROJ)]
                 + [_const_spec(c.shape) for c in consts],
        out_specs=row(D_MODEL),
        out_shape=jax.ShapeDtypeStruct((m, D_MODEL), F32),
        compiler_params=_params(1),
        name="merge_ffn",
    )(x, oa, ob, pg, *consts)


def kernel(x_prompt, x_sample, state_rwkv_wkv, state_rwkv_shift, state_hgrn, norm_mix_g, w_in, rwkv_mu, rwkv_w0, rwkv_w2, rwkv_a0, rwkv_a2, rwkv_g2, rwkv_k_k, rwkv_k_a, rwkv_r_k, rwkv_ln_g, rwkv_ln_b, w_up_a, hgrn_lb, hgrn_norm_g, w_up_b, w_out, norm_ffn_g, w_ffn_gate, w_ffn_up, w_ffn_down, norm_final_g):
    bp, tp, _ = x_prompt.shape
    bs = x_sample.shape[0]
    mp = bp * tp

    w_in0 = w_in[0]
    wa_in = w_in0[:, :RWKV_PROJ].astype(BF16)
    wb_in = w_in0[:, RWKV_PROJ:RWKV_PROJ + HGRN_PROJ].astype(BF16)
    wc_in = w_in0[:, RWKV_PROJ + HGRN_PROJ:].astype(BF16)
    vec = lambda p: p.reshape(1, -1).astype(F32)
    head_id = jnp.arange(MXU_WIDTH, dtype=jnp.int32) // RWKV_HEAD
    ones_bd = (head_id[:, None] == head_id[None, :]).astype(BF16)
    prep_consts = (vec(rwkv_mu[0]), vec(rwkv_w0[0]), rwkv_w2[0], vec(rwkv_a0[0]), rwkv_a2[0],
                   rwkv_g2[0], vec(rwkv_k_k[0]), vec(rwkv_k_a[0]), vec(rwkv_r_k[0]), ones_bd)
    lb = jnp.cumsum(jax.nn.softmax(hgrn_lb.astype(F32), axis=0), axis=0)[0]
    ffn_consts = (w_up_a[0].astype(BF16), w_up_b[0].astype(BF16), w_out[0].astype(BF16),
                  vec(norm_ffn_g[0]), w_ffn_gate[0].astype(BF16), w_ffn_up[0].astype(BF16),
                  w_ffn_down[0].astype(BF16), vec(norm_final_g))
    g_mix = vec(norm_mix_g[0])

    xp = x_prompt.reshape(mp, D_MODEL)
    p_rwkv, p_hgrn, p_gate = _norm_proj(xp, g_mix, wa_in, wb_in, wc_in, tm=512)
    p3 = p_rwkv.reshape(bp, tp, RWKV_PROJ)
    o_a, wkv_pp = _rwkv_mix(p3, jnp.zeros((bp, 1, RWKV_PROJ), F32),
                            prep_consts + (vec(rwkv_ln_g[0]), vec(rwkv_ln_b[0])),
                            jnp.zeros((bp, RWKV_HEADS // 2, RWKV_HEAD, 2 * RWKV_HEAD), F32),
                            tt=256, nb=2)
    wkv_p = (wkv_pp.reshape(bp, RWKV_HEADS // 2, RWKV_HEAD, 2, RWKV_HEAD)
             .transpose(0, 1, 3, 2, 4).reshape(bp, RWKV_HEADS, RWKV_HEAD, RWKV_HEAD))
    hg0 = jnp.zeros((bp, HGRN_HEADS, HGRN_HEAD, HGRN_HEAD), F32)
    o_b, hgT_p = _hgrn_chunk(p_hgrn.reshape(bp, tp, HGRN_PROJ), vec(lb), vec(hgrn_norm_g[0]), hg0,
                             tt=256)
    y_prompt = _merge_ffn(xp, o_a.reshape(mp, RWKV_WIDTH), o_b.reshape(mp, HGRN_WIDTH), p_gate,
                          ffn_consts, tm=512).reshape(bp, tp, D_MODEL)
    shift_p = p3[:, -1]
    hgrn_p = jnp.swapaxes(hgT_p, -1, -2)

    xs = x_sample.reshape(bs, D_MODEL)
    s_rwkv, s_hgrn, s_gate = _norm_proj(xs, g_mix, wa_in, wb_in, wc_in, tm=bs)
    r, lw, k, v, kk, a, g, bonus = _rwkv_prep(s_rwkv, state_rwkv_shift[0], prep_consts, tm=bs)
    kcol = lambda t: t.T.reshape(RWKV_HEADS, RWKV_HEAD, 1, bs)
    wkv_t = jnp.transpose(state_rwkv_wkv[0], (1, 3, 2, 0))
    oaT, wkv_nt = _rwkv_step(kcol(r), kcol(lw), kcol(k), kcol(kk), kcol(a), v.T, bonus.T, g.T,
                             rwkv_ln_g[0].reshape(RWKV_WIDTH, 1), rwkv_ln_b[0].reshape(RWKV_WIDTH, 1),
                             wkv_t)
    wkv_s = jnp.transpose(wkv_nt, (3, 0, 2, 1))
    hT = s_hgrn.T
    hcol = lambda t: t.reshape(HGRN_HEADS, HGRN_HEAD, 1, bs)
    hg_t = jnp.transpose(state_hgrn[0], (1, 2, 3, 0))
    obT, hg_nt = _hgrn_step(hcol(hT[:HGRN_WIDTH]), hcol(hT[HGRN_WIDTH:2 * HGRN_WIDTH]),
                            hT[2 * HGRN_WIDTH:3 * HGRN_WIDTH], hT[3 * HGRN_WIDTH:],
                            lb.reshape(HGRN_HEADS, HGRN_HEAD, 1, 1),
                            hgrn_norm_g[0].reshape(HGRN_WIDTH, 1), hg_t)
    hgrn_s = jnp.transpose(hg_nt, (3, 0, 1, 2))
    y_sample = _merge_ffn(xs, oaT.T, obT.T, s_gate, ffn_consts, tm=bs).reshape(bs, 1, D_MODEL)

    return (y_prompt, y_sample, wkv_p[None], shift_p[None], hgrn_p[None],
            wkv_s[None], s_rwkv[None], hgrn_s[None])
```

```python
import functools

import jax
import jax.numpy as jnp
from jax import lax
from jax.experimental import pallas as pl
from jax.experimental.pallas import tpu as pltpu

F32 = jnp.float32
BF16 = jnp.bfloat16

D_MODEL = 1024
RWKV_WIDTH = 512
RWKV_HEAD = 64
RWKV_HEADS = 8
RWKV_DECAY_LORA = 64
RWKV_A_LORA = 64
RWKV_GATE_LORA = 128
RWKV_PROJ = 3 * RWKV_WIDTH + RWKV_DECAY_LORA + RWKV_A_LORA + RWKV_GATE_LORA
RWKV_GN_EPS = 64e-5
L2_EPS = 1e-12
HGRN_WIDTH = 512
HGRN_HEADS = 4
HGRN_HEAD = 128
HGRN_PROJ = 4 * HGRN_WIDTH
GATE_PROJ = 2 * D_MODEL
D_FF = 2816
RMS_EPS = 1e-6

MXU_WIDTH = 256
CHUNK = 64
HGRN_SUB = 16
EXP_CLAMP = 80.0
VMEM_LIMIT = 56 * 1024 * 1024


def _dot(a, b):
    return jnp.dot(a.astype(BF16), b.astype(BF16), preferred_element_type=F32)


def _dot_nt(a, b):
    return lax.dot_general(a.astype(BF16), b.astype(BF16), (((1,), (1,)), ((), ())),
                           preferred_element_type=F32)


def _dot_tn(a, b):
    return lax.dot_general(a.astype(BF16), b.astype(BF16), (((0,), (0,)), ((), ())),
                           preferred_element_type=F32)


def _split2(x):
    hi = x.astype(BF16)
    lo = (x - hi.astype(F32)).astype(BF16)
    return hi, lo


def _split3(x):
    hi = x.astype(BF16)
    r1 = x - hi.astype(F32)
    mid = r1.astype(BF16)
    lo = (r1 - mid.astype(F32)).astype(BF16)
    return hi, mid, lo


def _dot_x3(a, b):
    ah, al = _split2(a)
    bh, bl = _split2(b)
    return _dot(ah, bh) + (_dot(ah, bl) + _dot(al, bh))


def _dot_exact_lhs(a_bf16, b):
    bh, bm, bl = _split3(b)
    return _dot(a_bf16, bh) + (_dot(a_bf16, bm) + _dot(a_bf16, bl))


def _sigmoid(x):
    return 1.0 / (1.0 + jnp.exp(-x))


def _rms_norm(x, g):
    return x * lax.rsqrt(jnp.mean(x * x, axis=-1, keepdims=True) + RMS_EPS) * g


def _head_sum(x, ones_bd):
    n = ones_bd.shape[0]
    xb = x.astype(BF16)
    return jnp.concatenate([jnp.dot(xb[:, j:j + n], ones_bd, preferred_element_type=F32)
                            for j in range(0, x.shape[1], n)], axis=1)


def _const_spec(shape):
    nd = len(shape)
    return pl.BlockSpec(shape, lambda *_: (0,) * nd, pipeline_mode=pl.Buffered(1))


def _params(n_grid):
    return pltpu.CompilerParams(dimension_semantics=("arbitrary",) * n_grid,
                                vmem_limit_bytes=VMEM_LIMIT)


def _norm_proj_kernel(x_ref, g_ref, wa_ref, wb_ref, oa_ref, ob_ref):
    h = _rms_norm(x_ref[...], g_ref[...]).astype(BF16)
    oa_ref[...] = jnp.dot(h, wa_ref[...], preferred_element_type=F32)
    ob_ref[...] = jnp.dot(h, wb_ref[...], preferred_element_type=F32)


def _norm_proj(x, g, wa, wb, tm):
    m = x.shape[0]
    row = lambda n: pl.BlockSpec((tm, n), lambda i: (i, 0))
    return pl.pallas_call(
        _norm_proj_kernel,
        grid=(m // tm,),
        in_specs=[row(D_MODEL), _const_spec((1, D_MODEL)), _const_spec(wa.shape),
                  _const_spec(wb.shape)],
        out_specs=[row(RWKV_PROJ), row(HGRN_PROJ)],
        out_shape=[jax.ShapeDtypeStruct((m, RWKV_PROJ), F32),
                   jax.ShapeDtypeStruct((m, HGRN_PROJ), F32)],
        compiler_params=_params(1),
        name="norm_proj",
    )(x, g, wa, wb)


def _rwkv_token_math(p, pprev, consts):
    mu_ref, w0_ref, w2_ref, a0_ref, a2_ref, g2_ref, kk_ref, ka_ref, rk_ref, ones_ref = consts
    W = RWKV_WIDTH
    ps = p + mu_ref[...] * (pprev - p)
    r = ps[:, 0:W]
    k = ps[:, W:2 * W]
    v = ps[:, 2 * W:3 * W]
    o0 = 3 * W
    wd = ps[:, o0:o0 + RWKV_DECAY_LORA]
    ad = ps[:, o0 + RWKV_DECAY_LORA:o0 + RWKV_DECAY_LORA + RWKV_A_LORA]
    gd = ps[:, o0 + RWKV_DECAY_LORA + RWKV_A_LORA:]
    z = -(w0_ref[...] + _dot_x3(jnp.tanh(wd), w2_ref[...]))
    softplus = jnp.maximum(z, 0.0) + jnp.log(1.0 + jnp.exp(-jnp.abs(z)))
    lw = -jnp.exp(-softplus - 0.5)
    a = _sigmoid(a0_ref[...] + _dot_x3(ad, a2_ref[...]))
    g = _dot_x3(_sigmoid(gd), g2_ref[...])
    ones_bd = ones_ref[...]
    kk = k * kk_ref[...]
    norm = jnp.sqrt(_head_sum(kk * kk, ones_bd))
    kk = kk / jnp.maximum(norm, L2_EPS)
    k = k * (1.0 + (a - 1.0) * ka_ref[...])
    bonus = _head_sum(r * k * rk_ref[...], ones_bd) * v
    return r, lw, k, v, kk, a, g, bonus


def _rwkv_prep_kernel(p_ref, pp_ref, *refs):
    consts, outs = refs[:10], refs[10:]
    for o_ref, val in zip(outs, _rwkv_token_math(p_ref[...], pp_ref[...], consts)):
        o_ref[...] = val


def _rwkv_prep(p, pprev, consts, tm):
    m = p.shape[0]
    row = lambda n: pl.BlockSpec((tm, n), lambda i: (i, 0))
    return pl.pallas_call(
        _rwkv_prep_kernel,
        grid=(m // tm,),
        in_specs=[row(RWKV_PROJ), row(RWKV_PROJ)] + [_const_spec(c.shape) for c in consts],
        out_specs=[row(RWKV_WIDTH)] * 8,
        out_shape=[jax.ShapeDtypeStruct((m, RWKV_WIDTH), F32)] * 8,
        compiler_params=_params(1),
        name="rwkv_prep",
    )(p, pprev, *consts)


def _rwkv_mix_kernel(p_ref, sh0_ref, mu_ref, w0_ref, w2_ref, a0_ref, a2_ref, g2_ref, kk_ref, ka_ref,
                     rk_ref, ones_ref, lng_ref, lnb_ref, s0_ref, o_ref, sT_ref,
                     s_scr, prev_scr, y_scr, *, nch):
    C = CHUNK
    tb = pl.program_id(1)

    @pl.when(tb == 0)
    def _():
        s_scr[...] = s0_ref[...]
        prev_scr[...] = sh0_ref[...]

    row = lax.broadcasted_iota(jnp.int32, (C, C), 0)
    col = lax.broadcasted_iota(jnp.int32, (C, C), 1)
    tril_incl = (col <= row).astype(BF16)
    PL = 2 * RWKV_HEAD
    t_row = lax.broadcasted_iota(jnp.int32, (C, 2 * PL), 0)
    s_col = lax.broadcasted_iota(jnp.int32, (C, 2 * PL), 1) & (RWKV_HEAD - 1)
    strict = (s_col < t_row)[:, :PL]
    incl = s_col <= t_row
    eye = jnp.where((s_col == t_row)[:, :PL], 1.0, 0.0)
    low_head = lax.broadcasted_iota(jnp.int32, (C, PL), 1) < RWKV_HEAD

    def bd(y):
        return jnp.concatenate([jnp.where(low_head, y, 0.0), jnp.where(low_head, 0.0, y)], axis=0)

    pairs = range(RWKV_HEADS // 2)
    psl = [slice(j * PL, (j + 1) * PL) for j in pairs]
    seqs = range(p_ref.shape[0])
    ch = [(b, c, j) for b in seqs for c in range(nch) for j in pairs]
    consts = (mu_ref, w0_ref, w2_ref, a0_ref, a2_ref, g2_ref, kk_ref, ka_ref, rk_ref, ones_ref)

    glast, AT, RT, VV, L, R, BK, gate, bonus = {}, {}, {}, {}, {}, {}, {}, {}, {}
    for b in seqs:
        p = p_ref[b]
        rows = lax.broadcasted_iota(jnp.int32, p.shape, 0)
        pprev = jnp.where(rows == 0, prev_scr[b], pltpu.roll(p, 1, 0))
        prev_scr[b] = p[p.shape[0] - 1:, :]
        r_all, lw_all, k_all, v_all, kk_all, a_all, gate[b], bonus[b] = _rwkv_token_math(p, pprev, consts)
        kb_all = kk_all * a_all
        for c in range(nch):
            sl = slice(c * C, (c + 1) * C)
            lw = lw_all[sl]
            k = k_all[sl]
            kk = kk_all[sl]
            kb = kb_all[sl]
            cum = _dot_exact_lhs(tril_incl, lw)
            cl = cum[C - 1:C, :]
            eneg = jnp.exp(-cum)
            elast = jnp.exp(cl - cum)
            glast[b, c] = jnp.exp(cl)
            at = -kk * jnp.exp(cum - lw)
            rt = r_all[sl] * jnp.exp(cum)
            bt = kb * eneg
            kt = k * eneg
            bh = kb * elast
            kh = k * elast
            v = v_all[sl]
            for j, ps in enumerate(psl):
                AT[b, c, j], RT[b, c, j], VV[b, c, j] = at[:, ps], rt[:, ps], v[:, ps]
                L[b, c, j] = jnp.concatenate([at[:, ps], rt[:, ps]], axis=0)
                R[b, c, j] = jnp.concatenate([bd(bt[:, ps]), bd(kt[:, ps])], axis=0)
                BK[b, c, j] = jnp.concatenate([bh[:, ps], kh[:, ps]], axis=0)
    P = {i: _dot_nt(L[i], R[i]) for i in ch}
    A = {i: jnp.where(strict, P[i][:C, :PL], 0.0) for i in ch}
    Aak = {i: jnp.where(strict, P[i][:C, PL:], 0.0) for i in ch}
    Ar = {i: jnp.where(incl, P[i][C:, :], 0.0) for i in ch}
    X = {i: _dot(A[i], bd(A[i])) for i in ch}
    Tm = {i: eye + A[i] for i in ch}
    for _ in range(4):
        Z = {i: _dot(X[i], jnp.concatenate([bd(X[i]), bd(Tm[i])], axis=1)) for i in ch}
        Tm = {i: Tm[i] + Z[i][:, PL:] for i in ch}
        X = {i: Z[i][:, :PL] for i in ch}
    AV = {i: _dot(Aak[i], bd(VV[i])) for i in ch}
    M = {i: _dot(Tm[i], jnp.concatenate([bd(AT[i]), bd(AV[i])], axis=1)) for i in ch}
    WU = {i: M[i] + _dot(X[i], jnp.concatenate([bd(M[i][:, :PL]), bd(M[i][:, PL:])], axis=1))
          for i in ch}
    WR = {i: jnp.concatenate([WU[i][:, :PL], RT[i]], axis=0) for i in ch}

    bj = [(b, j) for b in seqs for j in pairs]
    S = {(b, j): s_scr[b, j] for b, j in bj}
    for c in range(nch):
        sl = slice(c * C, (c + 1) * C)
        WRS = {(b, j): _dot_nt(WR[b, c, j], bd(S[b, j])) for b, j in bj}
        U = {(b, j): WRS[b, j][:C] + WU[b, c, j][:, PL:] for b, j in bj}
        UV = {(b, j): jnp.concatenate([U[b, j], VV[b, c, j]], axis=0) for b, j in bj}
        full = {(b, j): _dot_tn(UV[b, j], BK[b, c, j]) for b, j in bj}
        S = {(b, j): S[b, j] * glast[b, c][:, psl[j]]
             + jnp.where(low_head, full[b, j][:C], full[b, j][C:]) for b, j in bj}
        for b, j in bj:
            y_scr[b, sl, psl[j]] = WRS[b, j][C:] + _dot(
                Ar[b, c, j], jnp.concatenate([bd(U[b, j]), bd(VV[b, c, j])], axis=0))
    for b, j in bj:
        s_scr[b, j] = S[b, j]

    ones_bd = ones_ref[...]
    inv_n = 1.0 / RWKV_HEAD
    for b in seqs:
        y = y_scr[b]
        d = y - _head_sum(y, ones_bd) * inv_n
        var = _head_sum(d * d, ones_bd) * inv_n
        o_ref[b] = (d * lax.rsqrt(var + RWKV_GN_EPS) * lng_ref[...] + lnb_ref[...] + bonus[b]) * gate[b]

    @pl.when(tb == pl.num_programs(1) - 1)
    def _():
        sT_ref[...] = s_scr[...]


def _rwkv_mix(p, shift0, consts, s0, tt, nb):
    b, t, _ = p.shape
    st = pl.BlockSpec((nb,) + s0.shape[1:], lambda i, j: (i, 0, 0, 0))
    return pl.pallas_call(
        functools.partial(_rwkv_mix_kernel, nch=tt // CHUNK),
        grid=(b // nb, t // tt),
        in_specs=[pl.BlockSpec((nb, tt, RWKV_PROJ), lambda i, j: (i, j, 0)),
                  pl.BlockSpec((nb, 1, RWKV_PROJ), lambda i, j: (i, 0, 0))]
                 + [_const_spec(c.shape) for c in consts] + [st],
        out_specs=[pl.BlockSpec((nb, tt, RWKV_WIDTH), lambda i, j: (i, j, 0)), st],
        out_shape=[jax.ShapeDtypeStruct((b, t, RWKV_WIDTH), F32), jax.ShapeDtypeStruct(s0.shape, F32)],
        scratch_shapes=[pltpu.VMEM((nb,) + s0.shape[1:], F32),
                        pltpu.VMEM((nb, 1, RWKV_PROJ), F32),
                        pltpu.VMEM((nb, tt, RWKV_WIDTH), F32)],
        compiler_params=_params(2),
        name="rwkv_mix",
    )(p, shift0, *consts, s0)


def _rwkv_step_kernel(r_ref, lw_ref, k_ref, kk_ref, a_ref, v_ref, bonus_ref, g_ref, lng_ref, lnb_ref,
                      s_ref, o_ref, so_ref):
    H = s_ref[0]
    kk = kk_ref[0]
    sa = jnp.sum(H * (-kk), axis=0)
    Hn = H * jnp.exp(lw_ref[0]) + (kk * a_ref[0]) * sa[None] + k_ref[0] * v_ref[...][None]
    so_ref[0] = Hn
    y = jnp.sum(Hn * r_ref[0], axis=0)
    d = y - jnp.mean(y, axis=0, keepdims=True)
    var = jnp.mean(d * d, axis=0, keepdims=True)
    o_ref[...] = (d * lax.rsqrt(var + RWKV_GN_EPS) * lng_ref[...] + lnb_ref[...]
                  + bonus_ref[...]) * g_ref[...]


def _rwkv_step(r, lw, k, kk, a, v, bonus, g, lng, lnb, s):
    nb = s.shape[-1]
    kvec = pl.BlockSpec((1, RWKV_HEAD, 1, nb), lambda h: (h, 0, 0, 0))
    vvec = pl.BlockSpec((RWKV_HEAD, nb), lambda h: (h, 0))
    col = pl.BlockSpec((RWKV_HEAD, 1), lambda h: (h, 0))
    st = pl.BlockSpec((1, RWKV_HEAD, RWKV_HEAD, nb), lambda h: (h, 0, 0, 0))
    return pl.pallas_call(
        _rwkv_step_kernel,
        grid=(RWKV_HEADS,),
        in_specs=[kvec] * 5 + [vvec] * 3 + [col, col, st],
        out_specs=[vvec, st],
        out_shape=[jax.ShapeDtypeStruct((RWKV_WIDTH, nb), F32), jax.ShapeDtypeStruct(s.shape, F32)],
        compiler_params=_params(1),
        name="rwkv_step",
    )(r, lw, k, kk, a, v, bonus, g, lng, lnb, s)


def _hgrn_chunk_kernel(p_ref, lb_ref, ng_ref, s0_ref, o_ref, sT_ref, s_scr, *, nch):
    C = CHUNK
    SB = HGRN_SUB
    W = HGRN_WIDTH
    tb = pl.program_id(1)

    @pl.when(tb == 0)
    def _():
        s_scr[...] = s0_ref[0]

    tt = nch * C
    row = lax.broadcasted_iota(jnp.int32, (tt, tt), 0)
    col = lax.broadcasted_iota(jnp.int32, (tt, tt), 1)
    tril_bd = ((col <= row) & (col >= (row & -C))).astype(BF16)
    lb = lb_ref[...]
    ng = ng_ref[...]
    heads = range(HGRN_HEADS)
    chunks = range(nch)
    subs = range(C // SB)
    hsl = [slice(h * HGRN_HEAD, (h + 1) * HGRN_HEAD) for h in heads]

    q = p_ref[0, :, 0:W]
    q = q * _sigmoid(q)
    f = lb + (1.0 - lb) * _sigmoid(p_ref[0, :, W:2 * W])
    kx = 1.0 - f
    iv = p_ref[0, :, 2 * W:3 * W]
    bc = _dot_exact_lhs(tril_bd, jnp.log(f))
    qe = q * jnp.exp(bc)

    sc, kend, glast = {}, {}, {}
    for c in chunks:
        c0 = c * C
        bl = bc[c0 + C - 1:c0 + C, :]
        glast[c] = jnp.exp(bl)
        kend[c] = kx[c0:c0 + C] * jnp.exp(bl - bc[c0:c0 + C])
        for i in subs:
            lo, n = c0 + i * SB, c0 + (i + 1) * SB
            if i == 0:
                eq = jnp.exp(bc[lo:n])
                ek = jnp.exp(jnp.minimum(-bc[c0:n], EXP_CLAMP))
            else:
                beta = bc[lo - 1:lo]
                eq = jnp.exp(bc[lo:n] - beta)
                ek = jnp.exp(jnp.minimum(beta - bc[c0:n], EXP_CLAMP))
            qi = q[lo:n] * eq
            ks = kx[c0:n] * ek
            keep = (lax.broadcasted_iota(jnp.int32, (SB, n - c0), 1)
                    <= lax.broadcasted_iota(jnp.int32, (SB, n - c0), 0) + i * SB)
            for h in heads:
                sc[c, h, i] = jnp.where(keep, _dot_nt(qi[:, hsl[h]], ks[:, hsl[h]]), 0.0)
    intra = {(c, h): jnp.concatenate(
        [_dot(sc[c, h, i], iv[c * C:c * C + (i + 1) * SB, hsl[h]]) for i in subs], axis=0)
        for c in chunks for h in heads}
    upd = {(c, h): _dot_tn(iv[c * C:(c + 1) * C, hsl[h]], kend[c][:, hsl[h]])
           for c in chunks for h in heads}
    St = [s_scr[h] for h in heads]
    for c in chunks:
        sl = slice(c * C, (c + 1) * C)
        for h in heads:
            o = _dot_nt(qe[sl, hsl[h]], St[h]) + intra[c, h]
            o = o * lax.rsqrt(jnp.mean(o * o, axis=-1, keepdims=True) + RMS_EPS)
            o_ref[0, sl, hsl[h]] = o * ng[:, hsl[h]] * _sigmoid(p_ref[0, sl, 3 * W + h * HGRN_HEAD:
                                                                  3 * W + (h + 1) * HGRN_HEAD])
        St = [St[h] * glast[c][:, hsl[h]] + upd[c, h] for h in heads]
    for h in heads:
        s_scr[h] = St[h]

    @pl.when(tb == pl.num_programs(1) - 1)
    def _():
        sT_ref[0] = s_scr[...]


def _hgrn_chunk(p, lb, ng, s0, tt):
    b, t, _ = p.shape
    st = pl.BlockSpec((1, HGRN_HEADS, HGRN_HEAD, HGRN_HEAD), lambda i, j: (i, 0, 0, 0))
    return pl.pallas_call(
        functools.partial(_hgrn_chunk_kernel, nch=tt // CHUNK),
        grid=(b, t // tt),
        in_specs=[pl.BlockSpec((1, tt, HGRN_PROJ), lambda i, j: (i, j, 0)),
                  _const_spec((1, HGRN_WIDTH)), _const_spec((1, HGRN_WIDTH)), st],
        out_specs=[pl.BlockSpec((1, tt, HGRN_WIDTH), lambda i, j: (i, j, 0)), st],
        out_shape=[jax.ShapeDtypeStruct((b, t, HGRN_WIDTH), F32),
                   jax.ShapeDtypeStruct(s0.shape, F32)],
        scratch_shapes=[pltpu.VMEM((HGRN_HEADS, HGRN_HEAD, HGRN_HEAD), F32)],
        compiler_params=_params(2),
        name="hgrn_chunk",
    )(p, lb, ng, s0)


def _hgrn_step_kernel(q_ref, fl_ref, i_ref, og_ref, lb_ref, ng_ref, s_ref, o_ref, so_ref):
    q = q_ref[0]
    q = q * _sigmoid(q)
    lb = lb_ref[0]
    f = lb + (1.0 - lb) * _sigmoid(fl_ref[0])
    Sn = s_ref[0] * f + (1.0 - f) * i_ref[...][None]
    so_ref[0] = Sn
    o = jnp.sum(Sn * q, axis=0)
    o = o * lax.rsqrt(jnp.mean(o * o, axis=0, keepdims=True) + RMS_EPS)
    o_ref[...] = o * ng_ref[...] * _sigmoid(og_ref[...])


def _hgrn_step(q, fl, iv, og, lb, ng, s):
    nb = s.shape[-1]
    kvec = pl.BlockSpec((1, HGRN_HEAD, 1, nb), lambda h: (h, 0, 0, 0))
    vvec = pl.BlockSpec((HGRN_HEAD, nb), lambda h: (h, 0))
    st = pl.BlockSpec((1, HGRN_HEAD, HGRN_HEAD, nb), lambda h: (h, 0, 0, 0))
    return pl.pallas_call(
        _hgrn_step_kernel,
        grid=(HGRN_HEADS,),
        in_specs=[kvec, kvec, vvec, vvec,
                  pl.BlockSpec((1, HGRN_HEAD, 1, 1), lambda h: (h, 0, 0, 0)),
                  pl.BlockSpec((HGRN_HEAD, 1), lambda h: (h, 0)), st],
        out_specs=[vvec, st],
        out_shape=[jax.ShapeDtypeStruct((HGRN_WIDTH, nb), F32), jax.ShapeDtypeStruct(s.shape, F32)],
        compiler_params=_params(1),
        name="hgrn_step",
    )(q, fl, iv, og, lb, ng, s)


def _merge_ffn_kernel(x_ref, oa_ref, ob_ref, gm_ref, wc_ref, wa_ref, wb_ref, wo_ref, nf_ref, wg_ref,
                      wu_ref, wd_ref, nfin_ref, out_ref):
    x = x_ref[...]
    pg = jnp.dot(_rms_norm(x, gm_ref[...]).astype(BF16), wc_ref[...], preferred_element_type=F32)
    merged = (_sigmoid(pg[:, :D_MODEL]) * _dot(oa_ref[...], wa_ref[...])
              + _sigmoid(pg[:, D_MODEL:]) * _dot(ob_ref[...], wb_ref[...]))
    x = x + _dot(merged, wo_ref[...])
    h = _rms_norm(x, nf_ref[...]).astype(BF16)
    gate = jnp.dot(h, wg_ref[...], preferred_element_type=F32)
    up = jnp.dot(h, wu_ref[...], preferred_element_type=F32)
    x = x + _dot(gate * _sigmoid(gate) * up, wd_ref[...])
    out_ref[...] = _rms_norm(x, nfin_ref[...])


def _merge_ffn(x, oa, ob, consts, tm):
    m = x.shape[0]
    row = lambda n: pl.BlockSpec((tm, n), lambda i: (i, 0))
    return pl.pallas_call(
        _merge_ffn_kernel,
        grid=(m // tm,),
        in_specs=[row(D_MODEL), row(RWKV_WIDTH), row(HGRN_WIDTH)]
                 + [_const_spec(c.shape) for c in consts],
        out_specs=row(D_MODEL),
        out_shape=jax.ShapeDtypeStruct((m, D_MODEL), F32),
        compiler_params=_params(1),
        name="merge_ffn",
    )(x, oa, ob, *consts)


def kernel(x_prompt, x_sample, state_rwkv_wkv, state_rwkv_shift, state_hgrn, norm_mix_g, w_in, rwkv_mu, rwkv_w0, rwkv_w2, rwkv_a0, rwkv_a2, rwkv_g2, rwkv_k_k, rwkv_k_a, rwkv_r_k, rwkv_ln_g, rwkv_ln_b, w_up_a, hgrn_lb, hgrn_norm_g, w_up_b, w_out, norm_ffn_g, w_ffn_gate, w_ffn_up, w_ffn_down, norm_final_g):
    bp, tp, _ = x_prompt.shape
    bs = x_sample.shape[0]
    mp = bp * tp

    w_in0 = w_in[0]
    wa_in = w_in0[:, :RWKV_PROJ].astype(BF16)
    wb_in = w_in0[:, RWKV_PROJ:RWKV_PROJ + HGRN_PROJ].astype(BF16)
    wc_in = w_in0[:, RWKV_PROJ + HGRN_PROJ:].astype(BF16)
    vec = lambda p: p.reshape(1, -1).astype(F32)
    head_id = jnp.arange(MXU_WIDTH, dtype=jnp.int32) // RWKV_HEAD
    ones_bd = (head_id[:, None] == head_id[None, :]).astype(BF16)
    prep_consts = (vec(rwkv_mu[0]), vec(rwkv_w0[0]), rwkv_w2[0], vec(rwkv_a0[0]), rwkv_a2[0],
                   rwkv_g2[0], vec(rwkv_k_k[0]), vec(rwkv_k_a[0]), vec(rwkv_r_k[0]), ones_bd)
    lb = jnp.cumsum(jax.nn.softmax(hgrn_lb.astype(F32), axis=0), axis=0)[0]
    g_mix = vec(norm_mix_g[0])
    ffn_consts = (g_mix, wc_in, w_up_a[0].astype(BF16), w_up_b[0].astype(BF16), w_out[0].astype(BF16),
                  vec(norm_ffn_g[0]), w_ffn_gate[0].astype(BF16), w_ffn_up[0].astype(BF16),
                  w_ffn_down[0].astype(BF16), vec(norm_final_g))

    xp = x_prompt.reshape(mp, D_MODEL)
    p_rwkv, p_hgrn = _norm_proj(xp, g_mix, wa_in, wb_in, tm=1024)
    p3 = p_rwkv.reshape(bp, tp, RWKV_PROJ)
    o_a, wkv_pp = _rwkv_mix(p3, jnp.zeros((bp, 1, RWKV_PROJ), F32),
                            prep_consts + (vec(rwkv_ln_g[0]), vec(rwkv_ln_b[0])),
                            jnp.zeros((bp, RWKV_HEADS // 2, RWKV_HEAD, 2 * RWKV_HEAD), F32),
                            tt=256, nb=2)
    wkv_p = (wkv_pp.reshape(bp, RWKV_HEADS // 2, RWKV_HEAD, 2, RWKV_HEAD)
             .transpose(0, 1, 3, 2, 4).reshape(bp, RWKV_HEADS, RWKV_HEAD, RWKV_HEAD))
    hg0 = jnp.zeros((bp, HGRN_HEADS, HGRN_HEAD, HGRN_HEAD), F32)
    o_b, hgT_p = _hgrn_chunk(p_hgrn.reshape(bp, tp, HGRN_PROJ), vec(lb), vec(hgrn_norm_g[0]), hg0,
                             tt=256)
    y_prompt = _merge_ffn(xp, o_a.reshape(mp, RWKV_WIDTH), o_b.reshape(mp, HGRN_WIDTH),
                          ffn_consts, tm=512).reshape(bp, tp, D_MODEL)
    shift_p = p3[:, -1]
    hgrn_p = jnp.swapaxes(hgT_p, -1, -2)

    xs = x_sample.reshape(bs, D_MODEL)
    s_rwkv, s_hgrn = _norm_proj(xs, g_mix, wa_in, wb_in, tm=bs)
    r, lw, k, v, kk, a, g, bonus = _rwkv_prep(s_rwkv, state_rwkv_shift[0], prep_consts, tm=bs)
    kcol = lambda t: t.T.reshape(RWKV_HEADS, RWKV_HEAD, 1, bs)
    wkv_t = jnp.transpose(state_rwkv_wkv[0], (1, 3, 2, 0))
    oaT, wkv_nt = _rwkv_step(kcol(r), kcol(lw), kcol(k), kcol(kk), kcol(a), v.T, bonus.T, g.T,
                             rwkv_ln_g[0].reshape(RWKV_WIDTH, 1), rwkv_ln_b[0].reshape(RWKV_WIDTH, 1),
                             wkv_t)
    wkv_s = jnp.transpose(wkv_nt, (3, 0, 2, 1))
    hT = s_hgrn.T
    hcol = lambda t: t.reshape(HGRN_HEADS, HGRN_HEAD, 1, bs)
    hg_t = jnp.transpose(state_hgrn[0], (1, 2, 3, 0))
    obT, hg_nt = _hgrn_step(hcol(hT[:HGRN_WIDTH]), hcol(hT[HGRN_WIDTH:2 * HGRN_WIDTH]),
                            hT[2 * HGRN_WIDTH:3 * HGRN_WIDTH], hT[3 * HGRN_WIDTH:],
                            lb.reshape(HGRN_HEADS, HGRN_HEAD, 1, 1),
                            hgrn_norm_g[0].reshape(HGRN_WIDTH, 1), hg_t)
    hgrn_s = jnp.transpose(hg_nt, (3, 0, 1, 2))
    y_sample = _merge_ffn(xs, oaT.T, obT.T, ffn_consts, tm=bs).reshape(bs, 1, D_MODEL)

    return (y_prompt, y_sample, wkv_p[None], shift_p[None], hgrn_p[None],
            wkv_s[None], s_rwkv[None], hgrn_s[None])
```

```python
import functools

import jax
import jax.numpy as jnp
from jax import lax
from jax.experimental import pallas as pl
from jax.experimental.pallas import tpu as pltpu

F32 = jnp.float32
BF16 = jnp.bfloat16

D_MODEL = 1024
RWKV_WIDTH = 512
RWKV_HEAD = 64
RWKV_HEADS = 8
RWKV_DECAY_LORA = 64
RWKV_A_LORA = 64
RWKV_GATE_LORA = 128
RWKV_PROJ = 3 * RWKV_WIDTH + RWKV_DECAY_LORA + RWKV_A_LORA + RWKV_GATE_LORA
RWKV_GN_EPS = 64e-5
L2_EPS = 1e-12
HGRN_WIDTH = 512
HGRN_HEADS = 4
HGRN_HEAD = 128
HGRN_PROJ = 4 * HGRN_WIDTH
GATE_PROJ = 2 * D_MODEL
D_FF = 2816
RMS_EPS = 1e-6

MXU_WIDTH = 256
CHUNK = 64
HGRN_SUB = 16
STEP_ROWS = 8
EXP_CLAMP = 80.0
VMEM_LIMIT = 56 * 1024 * 1024


def _dot(a, b):
    return jnp.dot(a.astype(BF16), b.astype(BF16), preferred_element_type=F32)


def _dot_nt(a, b):
    return lax.dot_general(a.astype(BF16), b.astype(BF16), (((1,), (1,)), ((), ())),
                           preferred_element_type=F32)


def _dot_tn(a, b):
    return lax.dot_general(a.astype(BF16), b.astype(BF16), (((0,), (0,)), ((), ())),
                           preferred_element_type=F32)


def _split2(x):
    hi = x.astype(BF16)
    lo = (x - hi.astype(F32)).astype(BF16)
    return hi, lo


def _split3(x):
    hi = x.astype(BF16)
    r1 = x - hi.astype(F32)
    mid = r1.astype(BF16)
    lo = (r1 - mid.astype(F32)).astype(BF16)
    return hi, mid, lo


def _dot_x3(a, b):
    ah, al = _split2(a)
    bh, bl = _split2(b)
    return _dot(ah, bh) + (_dot(ah, bl) + _dot(al, bh))


def _dot_exact_lhs(a_bf16, b):
    bh, bm, bl = _split3(b)
    return _dot(a_bf16, bh) + (_dot(a_bf16, bm) + _dot(a_bf16, bl))


def _sigmoid(x):
    return 1.0 / (1.0 + jnp.exp(-x))


def _rms_norm(x, g):
    return x * lax.rsqrt(jnp.mean(x * x, axis=-1, keepdims=True) + RMS_EPS) * g


def _head_sum(x, ones_bd):
    n = ones_bd.shape[0]
    xb = x.astype(BF16)
    return jnp.concatenate([jnp.dot(xb[:, j:j + n], ones_bd, preferred_element_type=F32)
                            for j in range(0, x.shape[1], n)], axis=1)


def _const_spec(shape):
    nd = len(shape)
    return pl.BlockSpec(shape, lambda *_: (0,) * nd, pipeline_mode=pl.Buffered(1))


def _params(n_grid):
    return pltpu.CompilerParams(dimension_semantics=("arbitrary",) * n_grid,
                                vmem_limit_bytes=VMEM_LIMIT)


def _norm_proj_kernel(x_ref, g_ref, wa_ref, wb_ref, oa_ref, ob_ref):
    h = _rms_norm(x_ref[...], g_ref[...]).astype(BF16)
    oa_ref[...] = jnp.dot(h, wa_ref[...], preferred_element_type=F32)
    ob_ref[...] = jnp.dot(h, wb_ref[...], preferred_element_type=F32)


def _norm_proj(x, g, wa, wb, tm):
    m = x.shape[0]
    row = lambda n: pl.BlockSpec((tm, n), lambda i: (i, 0))
    return pl.pallas_call(
        _norm_proj_kernel,
        grid=(m // tm,),
        in_specs=[row(D_MODEL), _const_spec((1, D_MODEL)), _const_spec(wa.shape),
                  _const_spec(wb.shape)],
        out_specs=[row(RWKV_PROJ), row(HGRN_PROJ)],
        out_shape=[jax.ShapeDtypeStruct((m, RWKV_PROJ), F32),
                   jax.ShapeDtypeStruct((m, HGRN_PROJ), F32)],
        compiler_params=_params(1),
        name="norm_proj",
    )(x, g, wa, wb)


def _rwkv_token_math(p, pprev, consts):
    mu_ref, w0_ref, w2_ref, a0_ref, a2_ref, g2_ref, kk_ref, ka_ref, rk_ref, ones_ref = consts
    W = RWKV_WIDTH
    ps = p + mu_ref[...] * (pprev - p)
    r = ps[:, 0:W]
    k = ps[:, W:2 * W]
    v = ps[:, 2 * W:3 * W]
    o0 = 3 * W
    wd = ps[:, o0:o0 + RWKV_DECAY_LORA]
    ad = ps[:, o0 + RWKV_DECAY_LORA:o0 + RWKV_DECAY_LORA + RWKV_A_LORA]
    gd = ps[:, o0 + RWKV_DECAY_LORA + RWKV_A_LORA:]
    z = -(w0_ref[...] + _dot_x3(jnp.tanh(wd), w2_ref[...]))
    softplus = jnp.maximum(z, 0.0) + jnp.log(1.0 + jnp.exp(-jnp.abs(z)))
    lw = -jnp.exp(-softplus - 0.5)
    a = _sigmoid(a0_ref[...] + _dot_x3(ad, a2_ref[...]))
    g = _dot_x3(_sigmoid(gd), g2_ref[...])
    ones_bd = ones_ref[...]
    kk = k * kk_ref[...]
    norm = jnp.sqrt(_head_sum(kk * kk, ones_bd))
    kk = kk / jnp.maximum(norm, L2_EPS)
    k = k * (1.0 + (a - 1.0) * ka_ref[...])
    bonus = _head_sum(r * k * rk_ref[...], ones_bd) * v
    return r, lw, k, v, kk, a, g, bonus


def _rwkv_prep_kernel(p_ref, pp_ref, *refs):
    consts, outs = refs[:10], refs[10:]
    for o_ref, val in zip(outs, _rwkv_token_math(p_ref[...], pp_ref[...], consts)):
        o_ref[...] = val


def _rwkv_prep(p, pprev, consts, tm):
    m = p.shape[0]
    row = lambda n: pl.BlockSpec((tm, n), lambda i: (i, 0))
    return pl.pallas_call(
        _rwkv_prep_kernel,
        grid=(m // tm,),
        in_specs=[row(RWKV_PROJ), row(RWKV_PROJ)] + [_const_spec(c.shape) for c in consts],
        out_specs=[row(RWKV_WIDTH)] * 8,
        out_shape=[jax.ShapeDtypeStruct((m, RWKV_WIDTH), F32)] * 8,
        compiler_params=_params(1),
        name="rwkv_prep",
    )(p, pprev, *consts)


def _rwkv_mix_kernel(p_ref, sh0_ref, mu_ref, w0_ref, w2_ref, a0_ref, a2_ref, g2_ref, kk_ref, ka_ref,
                     rk_ref, ones_ref, lng_ref, lnb_ref, s0_ref, o_ref, sT_ref,
                     s_scr, prev_scr, y_scr, *, nch):
    C = CHUNK
    tb = pl.program_id(1)

    @pl.when(tb == 0)
    def _():
        s_scr[...] = s0_ref[...]
        prev_scr[...] = sh0_ref[...]

    row = lax.broadcasted_iota(jnp.int32, (C, C), 0)
    col = lax.broadcasted_iota(jnp.int32, (C, C), 1)
    tril_incl = (col <= row).astype(BF16)
    PL = 2 * RWKV_HEAD
    t_row = lax.broadcasted_iota(jnp.int32, (C, 2 * PL), 0)
    s_col = lax.broadcasted_iota(jnp.int32, (C, 2 * PL), 1) & (RWKV_HEAD - 1)
    strict = (s_col < t_row)[:, :PL]
    incl = s_col <= t_row
    eye = jnp.where((s_col == t_row)[:, :PL], 1.0, 0.0)
    low_head = lax.broadcasted_iota(jnp.int32, (C, PL), 1) < RWKV_HEAD

    def bd(y):
        return jnp.concatenate([jnp.where(low_head, y, 0.0), jnp.where(low_head, 0.0, y)], axis=0)

    pairs = range(RWKV_HEADS // 2)
    psl = [slice(j * PL, (j + 1) * PL) for j in pairs]
    seqs = range(p_ref.shape[0])
    ch = [(b, c, j) for b in seqs for c in range(nch) for j in pairs]
    consts = (mu_ref, w0_ref, w2_ref, a0_ref, a2_ref, g2_ref, kk_ref, ka_ref, rk_ref, ones_ref)

    glast, AT, RT, VV, L, R, BK, gate, bonus = {}, {}, {}, {}, {}, {}, {}, {}, {}
    for b in seqs:
        p = p_ref[b]
        rows = lax.broadcasted_iota(jnp.int32, p.shape, 0)
        pprev = jnp.where(rows == 0, prev_scr[b], pltpu.roll(p, 1, 0))
        prev_scr[b] = p[p.shape[0] - 1:, :]
        r_all, lw_all, k_all, v_all, kk_all, a_all, gate[b], bonus[b] = _rwkv_token_math(p, pprev, consts)
        kb_all = kk_all * a_all
        for c in range(nch):
            sl = slice(c * C, (c + 1) * C)
            lw = lw_all[sl]
            k = k_all[sl]
            kk = kk_all[sl]
            kb = kb_all[sl]
            cum = _dot_exact_lhs(tril_incl, lw)
            cl = cum[C - 1:C, :]
            eneg = jnp.exp(-cum)
            elast = jnp.exp(cl - cum)
            glast[b, c] = jnp.exp(cl)
            at = -kk * jnp.exp(cum - lw)
            rt = r_all[sl] * jnp.exp(cum)
            bt = kb * eneg
            kt = k * eneg
            bh = kb * elast
            kh = k * elast
            v = v_all[sl]
            for j, ps in enumerate(psl):
                AT[b, c, j], RT[b, c, j], VV[b, c, j] = at[:, ps], rt[:, ps], v[:, ps]
                L[b, c, j] = jnp.concatenate([at[:, ps], rt[:, ps]], axis=0)
                R[b, c, j] = jnp.concatenate([bd(bt[:, ps]), bd(kt[:, ps])], axis=0)
                BK[b, c, j] = jnp.concatenate([bh[:, ps], kh[:, ps]], axis=0)
    P = {i: _dot_nt(L[i], R[i]) for i in ch}
    A = {i: jnp.where(strict, P[i][:C, :PL], 0.0) for i in ch}
    Aak = {i: jnp.where(strict, P[i][:C, PL:], 0.0) for i in ch}
    Ar = {i: jnp.where(incl, P[i][C:, :], 0.0) for i in ch}
    X = {i: _dot(A[i], bd(A[i])) for i in ch}
    Tm = {i: eye + A[i] for i in ch}
    for _ in range(4):
        Z = {i: _dot(X[i], jnp.concatenate([bd(X[i]), bd(Tm[i])], axis=1)) for i in ch}
        Tm = {i: Tm[i] + Z[i][:, PL:] for i in ch}
        X = {i: Z[i][:, :PL] for i in ch}
    AV = {i: _dot(Aak[i], bd(VV[i])) for i in ch}
    M = {i: _dot(Tm[i], jnp.concatenate([bd(AT[i]), bd(AV[i])], axis=1)) for i in ch}
    WU = {i: M[i] + _dot(X[i], jnp.concatenate([bd(M[i][:, :PL]), bd(M[i][:, PL:])], axis=1))
          for i in ch}
    WR = {i: jnp.concatenate([WU[i][:, :PL], RT[i]], axis=0) for i in ch}

    bj = [(b, j) for b in seqs for j in pairs]
    S = {(b, j): s_scr[b, j] for b, j in bj}
    for c in range(nch):
        sl = slice(c * C, (c + 1) * C)
        WRS = {(b, j): _dot_nt(WR[b, c, j], bd(S[b, j])) for b, j in bj}
        U = {(b, j): WRS[b, j][:C] + WU[b, c, j][:, PL:] for b, j in bj}
        UV = {(b, j): jnp.concatenate([U[b, j], VV[b, c, j]], axis=0) for b, j in bj}
        full = {(b, j): _dot_tn(UV[b, j], BK[b, c, j]) for b, j in bj}
        S = {(b, j): S[b, j] * glast[b, c][:, psl[j]]
             + jnp.where(low_head, full[b, j][:C], full[b, j][C:]) for b, j in bj}
        for b, j in bj:
            y_scr[b, sl, psl[j]] = WRS[b, j][C:] + _dot(
                Ar[b, c, j], jnp.concatenate([bd(U[b, j]), bd(VV[b, c, j])], axis=0))
    for b, j in bj:
        s_scr[b, j] = S[b, j]

    ones_bd = ones_ref[...]
    inv_n = 1.0 / RWKV_HEAD
    for b in seqs:
        y = y_scr[b]
        d = y - _head_sum(y, ones_bd) * inv_n
        var = _head_sum(d * d, ones_bd) * inv_n
        o_ref[b] = (d * lax.rsqrt(var + RWKV_GN_EPS) * lng_ref[...] + lnb_ref[...] + bonus[b]) * gate[b]

    @pl.when(tb == pl.num_programs(1) - 1)
    def _():
        sT_ref[...] = s_scr[...]


def _rwkv_mix(p, shift0, consts, s0, tt, nb):
    b, t, _ = p.shape
    st = pl.BlockSpec((nb,) + s0.shape[1:], lambda i, j: (i, 0, 0, 0))
    return pl.pallas_call(
        functools.partial(_rwkv_mix_kernel, nch=tt // CHUNK),
        grid=(b // nb, t // tt),
        in_specs=[pl.BlockSpec((nb, tt, RWKV_PROJ), lambda i, j: (i, j, 0)),
                  pl.BlockSpec((nb, 1, RWKV_PROJ), lambda i, j: (i, 0, 0))]
                 + [_const_spec(c.shape) for c in consts] + [st],
        out_specs=[pl.BlockSpec((nb, tt, RWKV_WIDTH), lambda i, j: (i, j, 0)), st],
        out_shape=[jax.ShapeDtypeStruct((b, t, RWKV_WIDTH), F32), jax.ShapeDtypeStruct(s0.shape, F32)],
        scratch_shapes=[pltpu.VMEM((nb,) + s0.shape[1:], F32),
                        pltpu.VMEM((nb, 1, RWKV_PROJ), F32),
                        pltpu.VMEM((nb, tt, RWKV_WIDTH), F32)],
        compiler_params=_params(2),
        name="rwkv_mix",
    )(p, shift0, *consts, s0)


def _rwkv_step_kernel(r_ref, lw_ref, k_ref, kk_ref, a_ref, v_ref, bonus_ref, g_ref, lng_ref, lnb_ref,
                      s_ref, o_ref, so_ref):
    H = s_ref[0]
    kk = kk_ref[0]
    sa = jnp.sum(H * (-kk), axis=0)
    Hn = H * jnp.exp(lw_ref[0]) + (kk * a_ref[0]) * sa[None] + k_ref[0] * v_ref[...][None]
    so_ref[0] = Hn
    y = jnp.sum(Hn * r_ref[0], axis=0)
    d = y - jnp.mean(y, axis=0, keepdims=True)
    var = jnp.mean(d * d, axis=0, keepdims=True)
    o_ref[...] = (d * lax.rsqrt(var + RWKV_GN_EPS) * lng_ref[...] + lnb_ref[...]
                  + bonus_ref[...]) * g_ref[...]


def _rwkv_step(r, lw, k, kk, a, v, bonus, g, lng, lnb, s):
    nb = s.shape[-1]
    kvec = pl.BlockSpec((1, RWKV_HEAD, 1, nb), lambda h: (h, 0, 0, 0))
    vvec = pl.BlockSpec((RWKV_HEAD, nb), lambda h: (h, 0))
    col = pl.BlockSpec((RWKV_HEAD, 1), lambda h: (h, 0))
    st = pl.BlockSpec((1, RWKV_HEAD, RWKV_HEAD, nb), lambda h: (h, 0, 0, 0))
    return pl.pallas_call(
        _rwkv_step_kernel,
        grid=(RWKV_HEADS,),
        in_specs=[kvec] * 5 + [vvec] * 3 + [col, col, st],
        out_specs=[vvec, st],
        out_shape=[jax.ShapeDtypeStruct((RWKV_WIDTH, nb), F32), jax.ShapeDtypeStruct(s.shape, F32)],
        compiler_params=_params(1),
        name="rwkv_step",
    )(r, lw, k, kk, a, v, bonus, g, lng, lnb, s)


def _hgrn_chunk_kernel(p_ref, lb_ref, ng_ref, s0_ref, o_ref, sT_ref, s_scr, *, nch):
    C = CHUNK
    SB = HGRN_SUB
    W = HGRN_WIDTH
    tb = pl.program_id(1)

    @pl.when(tb == 0)
    def _():
        s_scr[...] = s0_ref[0]

    tt = nch * C
    row = lax.broadcasted_iota(jnp.int32, (tt, tt), 0)
    col = lax.broadcasted_iota(jnp.int32, (tt, tt), 1)
    tril_bd = ((col <= row) & (col >= (row & -C))).astype(BF16)
    lb = lb_ref[...]
    ng = ng_ref[...]
    heads = range(HGRN_HEADS)
    chunks = range(nch)
    subs = range(C // SB)
    hsl = [slice(h * HGRN_HEAD, (h + 1) * HGRN_HEAD) for h in heads]

    q = p_ref[0, :, 0:W]
    q = q * _sigmoid(q)
    f = lb + (1.0 - lb) * _sigmoid(p_ref[0, :, W:2 * W])
    kx = 1.0 - f
    iv = p_ref[0, :, 2 * W:3 * W]
    bc = _dot_exact_lhs(tril_bd, jnp.log(f))
    qe = q * jnp.exp(bc)

    sc, kend, glast = {}, {}, {}
    for c in chunks:
        c0 = c * C
        bl = bc[c0 + C - 1:c0 + C, :]
        glast[c] = jnp.exp(bl)
        kend[c] = kx[c0:c0 + C] * jnp.exp(bl - bc[c0:c0 + C])
        for i in subs:
            lo, n = c0 + i * SB, c0 + (i + 1) * SB
            if i == 0:
                eq = jnp.exp(bc[lo:n])
                ek = jnp.exp(jnp.minimum(-bc[c0:n], EXP_CLAMP))
            else:
                beta = bc[lo - 1:lo]
                eq = jnp.exp(bc[lo:n] - beta)
                ek = jnp.exp(jnp.minimum(beta - bc[c0:n], EXP_CLAMP))
            qi = q[lo:n] * eq
            ks = kx[c0:n] * ek
            keep = (lax.broadcasted_iota(jnp.int32, (SB, n - c0), 1)
                    <= lax.broadcasted_iota(jnp.int32, (SB, n - c0), 0) + i * SB)
            for h in heads:
                sc[c, h, i] = jnp.where(keep, _dot_nt(qi[:, hsl[h]], ks[:, hsl[h]]), 0.0)
    intra = {(c, h): jnp.concatenate(
        [_dot(sc[c, h, i], iv[c * C:c * C + (i + 1) * SB, hsl[h]]) for i in subs], axis=0)
        for c in chunks for h in heads}
    upd = {(c, h): _dot_tn(iv[c * C:(c + 1) * C, hsl[h]], kend[c][:, hsl[h]])
           for c in chunks for h in heads}
    St = [s_scr[h] for h in heads]
    for c in chunks:
        sl = slice(c * C, (c + 1) * C)
        for h in heads:
            o = _dot_nt(qe[sl, hsl[h]], St[h]) + intra[c, h]
            o = o * lax.rsqrt(jnp.mean(o * o, axis=-1, keepdims=True) + RMS_EPS)
            o_ref[0, sl, hsl[h]] = o * ng[:, hsl[h]] * _sigmoid(p_ref[0, sl, 3 * W + h * HGRN_HEAD:
                                                                  3 * W + (h + 1) * HGRN_HEAD])
        St = [St[h] * glast[c][:, hsl[h]] + upd[c, h] for h in heads]
    for h in heads:
        s_scr[h] = St[h]

    @pl.when(tb == pl.num_programs(1) - 1)
    def _():
        sT_ref[0] = s_scr[...]


def _hgrn_chunk(p, lb, ng, s0, tt):
    b, t, _ = p.shape
    st = pl.BlockSpec((1, HGRN_HEADS, HGRN_HEAD, HGRN_HEAD), lambda i, j: (i, 0, 0, 0))
    return pl.pallas_call(
        functools.partial(_hgrn_chunk_kernel, nch=tt // CHUNK),
        grid=(b, t // tt),
        in_specs=[pl.BlockSpec((1, tt, HGRN_PROJ), lambda i, j: (i, j, 0)),
                  _const_spec((1, HGRN_WIDTH)), _const_spec((1, HGRN_WIDTH)), st],
        out_specs=[pl.BlockSpec((1, tt, HGRN_WIDTH), lambda i, j: (i, j, 0)), st],
        out_shape=[jax.ShapeDtypeStruct((b, t, HGRN_WIDTH), F32),
                   jax.ShapeDtypeStruct(s0.shape, F32)],
        scratch_shapes=[pltpu.VMEM((HGRN_HEADS, HGRN_HEAD, HGRN_HEAD), F32)],
        compiler_params=_params(2),
        name="hgrn_chunk",
    )(p, lb, ng, s0)


def _hgrn_step_kernel(qc_ref, fc_ref, p_ref, lb_ref, ng_ref, s_ref, o_ref, so_ref):
    W = HGRN_WIDTH
    q = qc_ref[0]
    q = q * _sigmoid(q)
    lb = lb_ref[...]
    f = lb + (1.0 - lb) * _sigmoid(fc_ref[0])
    iv = p_ref[:, 2 * W:3 * W]
    gate = ng_ref[...] * _sigmoid(p_ref[:, 3 * W:4 * W])
    bh = [(b, h) for b in range(s_ref.shape[0]) for h in range(HGRN_HEADS)]
    hsl = [slice(h * HGRN_HEAD, (h + 1) * HGRN_HEAD) for h in range(HGRN_HEADS)]
    shape = (HGRN_HEAD, HGRN_HEAD)
    fcol = {(b, h): jnp.broadcast_to(f[h, :, b:b + 1], shape) for b, h in bh}
    qcol = {(b, h): jnp.broadcast_to(q[h, :, b:b + 1], shape) for b, h in bh}
    Sn = {(b, h): s_ref[b, h] * fcol[b, h] + (1.0 - fcol[b, h]) * iv[b:b + 1, hsl[h]] for b, h in bh}
    for b, h in bh:
        so_ref[b, h] = Sn[b, h]
    o = {(b, h): jnp.sum(Sn[b, h] * qcol[b, h], axis=0, keepdims=True) for b, h in bh}
    ms = {i: jnp.mean(o[i] * o[i], axis=-1, keepdims=True) for i in bh}
    for b, h in bh:
        o_ref[b:b + 1, hsl[h]] = o[b, h] * lax.rsqrt(ms[b, h] + RMS_EPS) * gate[b:b + 1, hsl[h]]


def _hgrn_step(qc, fc, p, lb, ng, s):
    nb = s.shape[0]
    rows = qc.shape[-1]
    col = pl.BlockSpec((1,) + qc.shape[1:], lambda i: (i, 0, 0, 0))
    st = pl.BlockSpec((rows,) + s.shape[1:], lambda i: (i, 0, 0, 0))
    return pl.pallas_call(
        _hgrn_step_kernel,
        grid=(nb // rows,),
        in_specs=[col, col, pl.BlockSpec((rows, HGRN_PROJ), lambda i: (i, 0)),
                  _const_spec(lb.shape), _const_spec(ng.shape), st],
        out_specs=[pl.BlockSpec((rows, HGRN_WIDTH), lambda i: (i, 0)), st],
        out_shape=[jax.ShapeDtypeStruct((nb, HGRN_WIDTH), F32), jax.ShapeDtypeStruct(s.shape, F32)],
        compiler_params=_params(1),
        name="hgrn_step",
    )(qc, fc, p, lb, ng, s)


def _merge_ffn_kernel(x_ref, oa_ref, ob_ref, gm_ref, wc_ref, wa_ref, wb_ref, wo_ref, nf_ref, wg_ref,
                      wu_ref, wd_ref, nfin_ref, out_ref):
    x = x_ref[...]
    pg = jnp.dot(_rms_norm(x, gm_ref[...]).astype(BF16), wc_ref[...], preferred_element_type=F32)
    merged = (_sigmoid(pg[:, :D_MODEL]) * _dot(oa_ref[...], wa_ref[...])
              + _sigmoid(pg[:, D_MODEL:]) * _dot(ob_ref[...], wb_ref[...]))
    x = x + _dot(merged, wo_ref[...])
    h = _rms_norm(x, nf_ref[...]).astype(BF16)
    gate = jnp.dot(h, wg_ref[...], preferred_element_type=F32)
    up = jnp.dot(h, wu_ref[...], preferred_element_type=F32)
    x = x + _dot(gate * _sigmoid(gate) * up, wd_ref[...])
    out_ref[...] = _rms_norm(x, nfin_ref[...])


def _merge_ffn(x, oa, ob, consts, tm):
    m = x.shape[0]
    row = lambda n: pl.BlockSpec((tm, n), lambda i: (i, 0))
    return pl.pallas_call(
        _merge_ffn_kernel,
        grid=(m // tm,),
        in_specs=[row(D_MODEL), row(RWKV_WIDTH), row(HGRN_WIDTH)]
                 + [_const_spec(c.shape) for c in consts],
        out_specs=row(D_MODEL),
        out_shape=jax.ShapeDtypeStruct((m, D_MODEL), F32),
        compiler_params=_params(1),
        name="merge_ffn",
    )(x, oa, ob, *consts)


def kernel(x_prompt, x_sample, state_rwkv_wkv, state_rwkv_shift, state_hgrn, norm_mix_g, w_in, rwkv_mu, rwkv_w0, rwkv_w2, rwkv_a0, rwkv_a2, rwkv_g2, rwkv_k_k, rwkv_k_a, rwkv_r_k, rwkv_ln_g, rwkv_ln_b, w_up_a, hgrn_lb, hgrn_norm_g, w_up_b, w_out, norm_ffn_g, w_ffn_gate, w_ffn_up, w_ffn_down, norm_final_g):
    bp, tp, _ = x_prompt.shape
    bs = x_sample.shape[0]
    mp = bp * tp

    w_in0 = w_in[0]
    wa_in = w_in0[:, :RWKV_PROJ].astype(BF16)
    wb_in = w_in0[:, RWKV_PROJ:RWKV_PROJ + HGRN_PROJ].astype(BF16)
    wc_in = w_in0[:, RWKV_PROJ + HGRN_PROJ:].astype(BF16)
    vec = lambda p: p.reshape(1, -1).astype(F32)
    head_id = jnp.arange(MXU_WIDTH, dtype=jnp.int32) // RWKV_HEAD
    ones_bd = (head_id[:, None] == head_id[None, :]).astype(BF16)
    prep_consts = (vec(rwkv_mu[0]), vec(rwkv_w0[0]), rwkv_w2[0], vec(rwkv_a0[0]), rwkv_a2[0],
                   rwkv_g2[0], vec(rwkv_k_k[0]), vec(rwkv_k_a[0]), vec(rwkv_r_k[0]), ones_bd)
    lb = jnp.cumsum(jax.nn.softmax(hgrn_lb.astype(F32), axis=0), axis=0)[0]
    g_mix = vec(norm_mix_g[0])
    ffn_consts = (g_mix, wc_in, w_up_a[0].astype(BF16), w_up_b[0].astype(BF16), w_out[0].astype(BF16),
                  vec(norm_ffn_g[0]), w_ffn_gate[0].astype(BF16), w_ffn_up[0].astype(BF16),
                  w_ffn_down[0].astype(BF16), vec(norm_final_g))

    xp = x_prompt.reshape(mp, D_MODEL)
    p_rwkv, p_hgrn = _norm_proj(xp, g_mix, wa_in, wb_in, tm=1024)
    p3 = p_rwkv.reshape(bp, tp, RWKV_PROJ)
    o_a, wkv_pp = _rwkv_mix(p3, jnp.zeros((bp, 1, RWKV_PROJ), F32),
                            prep_consts + (vec(rwkv_ln_g[0]), vec(rwkv_ln_b[0])),
                            jnp.zeros((bp, RWKV_HEADS // 2, RWKV_HEAD, 2 * RWKV_HEAD), F32),
                            tt=256, nb=2)
    wkv_p = (wkv_pp.reshape(bp, RWKV_HEADS // 2, RWKV_HEAD, 2, RWKV_HEAD)
             .transpose(0, 1, 3, 2, 4).reshape(bp, RWKV_HEADS, RWKV_HEAD, RWKV_HEAD))
    hg0 = jnp.zeros((bp, HGRN_HEADS, HGRN_HEAD, HGRN_HEAD), F32)
    o_b, hgT_p = _hgrn_chunk(p_hgrn.reshape(bp, tp, HGRN_PROJ), vec(lb), vec(hgrn_norm_g[0]), hg0,
                             tt=256)
    y_prompt = _merge_ffn(xp, o_a.reshape(mp, RWKV_WIDTH), o_b.reshape(mp, HGRN_WIDTH),
                          ffn_consts, tm=512).reshape(bp, tp, D_MODEL)
    shift_p = p3[:, -1]
    hgrn_p = jnp.swapaxes(hgT_p, -1, -2)

    xs = x_sample.reshape(bs, D_MODEL)
    s_rwkv, s_hgrn = _norm_proj(xs, g_mix, wa_in, wb_in, tm=bs)
    r, lw, k, v, kk, a, g, bonus = _rwkv_prep(s_rwkv, state_rwkv_shift[0], prep_consts, tm=bs)
    kcol = lambda t: t.T.reshape(RWKV_HEADS, RWKV_HEAD, 1, bs)
    wkv_t = jnp.transpose(state_rwkv_wkv[0], (1, 3, 2, 0))
    oaT, wkv_nt = _rwkv_step(kcol(r), kcol(lw), kcol(k), kcol(kk), kcol(a), v.T, bonus.T, g.T,
                             rwkv_ln_g[0].reshape(RWKV_WIDTH, 1), rwkv_ln_b[0].reshape(RWKV_WIDTH, 1),
                             wkv_t)
    wkv_s = jnp.transpose(wkv_nt, (3, 0, 2, 1))
    kmajor = lambda t: (t.reshape(bs // STEP_ROWS, STEP_ROWS, HGRN_HEADS, HGRN_HEAD)
                        .transpose(0, 2, 3, 1))
    o_bs, hgrn_s = _hgrn_step(kmajor(s_hgrn[:, :HGRN_WIDTH]), kmajor(s_hgrn[:, HGRN_WIDTH:2 * HGRN_WIDTH]),
                              s_hgrn, lb.reshape(HGRN_HEADS, HGRN_HEAD, 1), vec(hgrn_norm_g[0]),
                              state_hgrn[0])
    y_sample = _merge_ffn(xs, oaT.T, o_bs, ffn_consts, tm=bs).reshape(bs, 1, D_MODEL)

    return (y_prompt, y_sample, wkv_p[None], shift_p[None], hgrn_p[None],
            wkv_s[None], s_rwkv[None], hgrn_s[None])
```

```python
import functools

import jax
import jax.numpy as jnp
from jax import lax
from jax.experimental import pallas as pl
from jax.experimental.pallas import tpu as pltpu

F32 = jnp.float32
BF16 = jnp.bfloat16

D_MODEL = 1024
RWKV_WIDTH = 512
RWKV_HEAD = 64
RWKV_HEADS = 8
RWKV_DECAY_LORA = 64
RWKV_A_LORA = 64
RWKV_GATE_LORA = 128
RWKV_PROJ = 3 * RWKV_WIDTH + RWKV_DECAY_LORA + RWKV_A_LORA + RWKV_GATE_LORA
RWKV_GN_EPS = 64e-5
L2_EPS = 1e-12
HGRN_WIDTH = 512
HGRN_HEADS = 4
HGRN_HEAD = 128
HGRN_PROJ = 4 * HGRN_WIDTH
GATE_PROJ = 2 * D_MODEL
D_FF = 2816
RMS_EPS = 1e-6

MXU_WIDTH = 256
CHUNK = 64
HGRN_SUB = 16
STEP_ROWS = 8
EXP_CLAMP = 80.0
VMEM_LIMIT = 56 * 1024 * 1024


def _dot(a, b):
    return jnp.dot(a.astype(BF16), b.astype(BF16), preferred_element_type=F32)


def _dot_nt(a, b):
    return lax.dot_general(a.astype(BF16), b.astype(BF16), (((1,), (1,)), ((), ())),
                           preferred_element_type=F32)


def _dot_tn(a, b):
    return lax.dot_general(a.astype(BF16), b.astype(BF16), (((0,), (0,)), ((), ())),
                           preferred_element_type=F32)


def _split2(x):
    hi = x.astype(BF16)
    lo = (x - hi.astype(F32)).astype(BF16)
    return hi, lo


def _split3(x):
    hi = x.astype(BF16)
    r1 = x - hi.astype(F32)
    mid = r1.astype(BF16)
    lo = (r1 - mid.astype(F32)).astype(BF16)
    return hi, mid, lo


def _dot_x3(a, b):
    ah, al = _split2(a)
    bh, bl = _split2(b)
    return _dot(ah, bh) + (_dot(ah, bl) + _dot(al, bh))


def _dot_exact_lhs(a_bf16, b):
    bh, bm, bl = _split3(b)
    return _dot(a_bf16, bh) + (_dot(a_bf16, bm) + _dot(a_bf16, bl))


def _sigmoid(x):
    return 1.0 / (1.0 + jnp.exp(-x))


def _rms_norm(x, g):
    return x * lax.rsqrt(jnp.mean(x * x, axis=-1, keepdims=True) + RMS_EPS) * g


def _head_sum(x, ones_bd):
    n = ones_bd.shape[0]
    xb = x.astype(BF16)
    return jnp.concatenate([jnp.dot(xb[:, j:j + n], ones_bd, preferred_element_type=F32)
                            for j in range(0, x.shape[1], n)], axis=1)


def _const_spec(shape):
    nd = len(shape)
    return pl.BlockSpec(shape, lambda *_: (0,) * nd, pipeline_mode=pl.Buffered(1))


def _params(n_grid):
    return pltpu.CompilerParams(dimension_semantics=("arbitrary",) * n_grid,
                                vmem_limit_bytes=VMEM_LIMIT)


def _norm_proj_kernel(x_ref, g_ref, wa_ref, wb_ref, oa_ref, ob_ref):
    h = _rms_norm(x_ref[...], g_ref[...]).astype(BF16)
    oa_ref[...] = jnp.dot(h, wa_ref[...], preferred_element_type=F32)
    ob_ref[...] = jnp.dot(h, wb_ref[...], preferred_element_type=F32)


def _norm_proj(x, g, wa, wb, tm):
    m = x.shape[0]
    row = lambda n: pl.BlockSpec((tm, n), lambda i: (i, 0))
    return pl.pallas_call(
        _norm_proj_kernel,
        grid=(m // tm,),
        in_specs=[row(D_MODEL), _const_spec((1, D_MODEL)), _const_spec(wa.shape),
                  _const_spec(wb.shape)],
        out_specs=[row(RWKV_PROJ), row(HGRN_PROJ)],
        out_shape=[jax.ShapeDtypeStruct((m, RWKV_PROJ), F32),
                   jax.ShapeDtypeStruct((m, HGRN_PROJ), F32)],
        compiler_params=_params(1),
        name="norm_proj",
    )(x, g, wa, wb)


def _rwkv_token_math(p, pprev, consts):
    mu_ref, w0_ref, w2_ref, a0_ref, a2_ref, g2_ref, kk_ref, ka_ref, rk_ref, ones_ref = consts
    W = RWKV_WIDTH
    ps = p + mu_ref[...] * (pprev - p)
    r = ps[:, 0:W]
    k = ps[:, W:2 * W]
    v = ps[:, 2 * W:3 * W]
    o0 = 3 * W
    wd = ps[:, o0:o0 + RWKV_DECAY_LORA]
    ad = ps[:, o0 + RWKV_DECAY_LORA:o0 + RWKV_DECAY_LORA + RWKV_A_LORA]
    gd = ps[:, o0 + RWKV_DECAY_LORA + RWKV_A_LORA:]
    z = -(w0_ref[...] + _dot_x3(jnp.tanh(wd), w2_ref[...]))
    softplus = jnp.maximum(z, 0.0) + jnp.log(1.0 + jnp.exp(-jnp.abs(z)))
    lw = -jnp.exp(-softplus - 0.5)
    a = _sigmoid(a0_ref[...] + _dot_x3(ad, a2_ref[...]))
    g = _dot_x3(_sigmoid(gd), g2_ref[...])
    ones_bd = ones_ref[...]
    kk = k * kk_ref[...]
    norm = jnp.sqrt(_head_sum(kk * kk, ones_bd))
    kk = kk / jnp.maximum(norm, L2_EPS)
    k = k * (1.0 + (a - 1.0) * ka_ref[...])
    bonus = _head_sum(r * k * rk_ref[...], ones_bd) * v
    return r, lw, k, v, kk, a, g, bonus


def _rwkv_prep_kernel(p_ref, pp_ref, *refs):
    consts, outs = refs[:10], refs[10:]
    for o_ref, val in zip(outs, _rwkv_token_math(p_ref[...], pp_ref[...], consts)):
        o_ref[...] = val


def _rwkv_prep(p, pprev, consts, tm):
    m = p.shape[0]
    row = lambda n: pl.BlockSpec((tm, n), lambda i: (i, 0))
    return pl.pallas_call(
        _rwkv_prep_kernel,
        grid=(m // tm,),
        in_specs=[row(RWKV_PROJ), row(RWKV_PROJ)] + [_const_spec(c.shape) for c in consts],
        out_specs=[row(RWKV_WIDTH)] * 8,
        out_shape=[jax.ShapeDtypeStruct((m, RWKV_WIDTH), F32)] * 8,
        compiler_params=_params(1),
        name="rwkv_prep",
    )(p, pprev, *consts)


def _rwkv_mix_kernel(p_ref, sh0_ref, mu_ref, w0_ref, w2_ref, a0_ref, a2_ref, g2_ref, kk_ref, ka_ref,
                     rk_ref, ones_ref, lng_ref, lnb_ref, s0_ref, o_ref, sT_ref,
                     s_scr, prev_scr, y_scr, *, nch):
    C = CHUNK
    tb = pl.program_id(1)

    @pl.when(tb == 0)
    def _():
        s_scr[...] = s0_ref[...]
        prev_scr[...] = sh0_ref[...]

    row = lax.broadcasted_iota(jnp.int32, (C, C), 0)
    col = lax.broadcasted_iota(jnp.int32, (C, C), 1)
    tril_incl = (col <= row).astype(BF16)
    PL = 2 * RWKV_HEAD
    t_row = lax.broadcasted_iota(jnp.int32, (C, 2 * PL), 0)
    s_col = lax.broadcasted_iota(jnp.int32, (C, 2 * PL), 1) & (RWKV_HEAD - 1)
    strict = (s_col < t_row)[:, :PL]
    incl = s_col <= t_row
    eye = jnp.where((s_col == t_row)[:, :PL], 1.0, 0.0)
    low_head = lax.broadcasted_iota(jnp.int32, (C, PL), 1) < RWKV_HEAD

    def bd(y):
        return jnp.concatenate([jnp.where(low_head, y, 0.0), jnp.where(low_head, 0.0, y)], axis=0)

    pairs = range(RWKV_HEADS // 2)
    psl = [slice(j * PL, (j + 1) * PL) for j in pairs]
    seqs = range(p_ref.shape[0])
    ch = [(b, c, j) for b in seqs for c in range(nch) for j in pairs]
    consts = (mu_ref, w0_ref, w2_ref, a0_ref, a2_ref, g2_ref, kk_ref, ka_ref, rk_ref, ones_ref)

    glast, AT, RT, VV, L, R, BK, gate, bonus = {}, {}, {}, {}, {}, {}, {}, {}, {}
    for b in seqs:
        p = p_ref[b]
        rows = lax.broadcasted_iota(jnp.int32, p.shape, 0)
        pprev = jnp.where(rows == 0, prev_scr[b], pltpu.roll(p, 1, 0))
        prev_scr[b] = p[p.shape[0] - 1:, :]
        r_all, lw_all, k_all, v_all, kk_all, a_all, gate[b], bonus[b] = _rwkv_token_math(p, pprev, consts)
        kb_all = kk_all * a_all
        for c in range(nch):
            sl = slice(c * C, (c + 1) * C)
            lw = lw_all[sl]
            k = k_all[sl]
            kk = kk_all[sl]
            kb = kb_all[sl]
            cum = _dot_exact_lhs(tril_incl, lw)
            cl = cum[C - 1:C, :]
            eneg = jnp.exp(-cum)
            elast = jnp.exp(cl - cum)
            glast[b, c] = jnp.exp(cl)
            at = -kk * jnp.exp(cum - lw)
            rt = r_all[sl] * jnp.exp(cum)
            bt = kb * eneg
            kt = k * eneg
            bh = kb * elast
            kh = k * elast
            v = v_all[sl]
            for j, ps in enumerate(psl):
                AT[b, c, j], RT[b, c, j], VV[b, c, j] = at[:, ps], rt[:, ps], v[:, ps]
                L[b, c, j] = jnp.concatenate([at[:, ps], rt[:, ps]], axis=0)
                R[b, c, j] = jnp.concatenate([bd(bt[:, ps]), bd(kt[:, ps])], axis=0)
                BK[b, c, j] = jnp.concatenate([bh[:, ps], kh[:, ps]], axis=0)
    P = {i: _dot_nt(L[i], R[i]) for i in ch}
    A = {i: jnp.where(strict, P[i][:C, :PL], 0.0) for i in ch}
    Aak = {i: jnp.where(strict, P[i][:C, PL:], 0.0) for i in ch}
    Ar = {i: jnp.where(incl, P[i][C:, :], 0.0) for i in ch}
    X = {i: _dot(A[i], bd(A[i])) for i in ch}
    Tm = {i: eye + A[i] for i in ch}
    for _ in range(4):
        Z = {i: _dot(X[i], jnp.concatenate([bd(X[i]), bd(Tm[i])], axis=1)) for i in ch}
        Tm = {i: Tm[i] + Z[i][:, PL:] for i in ch}
        X = {i: Z[i][:, :PL] for i in ch}
    AV = {i: _dot(Aak[i], bd(VV[i])) for i in ch}
    M = {i: _dot(Tm[i], jnp.concatenate([bd(AT[i]), bd(AV[i])], axis=1)) for i in ch}
    WU = {i: M[i] + _dot(X[i], jnp.concatenate([bd(M[i][:, :PL]), bd(M[i][:, PL:])], axis=1))
          for i in ch}
    WR = {i: jnp.concatenate([WU[i][:, :PL], RT[i]], axis=0) for i in ch}

    bj = [(b, j) for b in seqs for j in pairs]
    S = {(b, j): s_scr[b, j] for b, j in bj}
    for c in range(nch):
        sl = slice(c * C, (c + 1) * C)
        WRS = {(b, j): _dot_nt(WR[b, c, j], bd(S[b, j])) for b, j in bj}
        U = {(b, j): WRS[b, j][:C] + WU[b, c, j][:, PL:] for b, j in bj}
        UV = {(b, j): jnp.concatenate([U[b, j], VV[b, c, j]], axis=0) for b, j in bj}
        full = {(b, j): _dot_tn(UV[b, j], BK[b, c, j]) for b, j in bj}
        S = {(b, j): S[b, j] * glast[b, c][:, psl[j]]
             + jnp.where(low_head, full[b, j][:C], full[b, j][C:]) for b, j in bj}
        for b, j in bj:
            y_scr[b, sl, psl[j]] = WRS[b, j][C:] + _dot(
                Ar[b, c, j], jnp.concatenate([bd(U[b, j]), bd(VV[b, c, j])], axis=0))
    for b, j in bj:
        s_scr[b, j] = S[b, j]

    ones_bd = ones_ref[...]
    inv_n = 1.0 / RWKV_HEAD
    for b in seqs:
        y = y_scr[b]
        d = y - _head_sum(y, ones_bd) * inv_n
        var = _head_sum(d * d, ones_bd) * inv_n
        o_ref[b] = (d * lax.rsqrt(var + RWKV_GN_EPS) * lng_ref[...] + lnb_ref[...] + bonus[b]) * gate[b]

    @pl.when(tb == pl.num_programs(1) - 1)
    def _():
        sT_ref[...] = s_scr[...]


def _rwkv_mix(p, shift0, consts, s0, tt, nb):
    b, t, _ = p.shape
    st = pl.BlockSpec((nb,) + s0.shape[1:], lambda i, j: (i, 0, 0, 0))
    return pl.pallas_call(
        functools.partial(_rwkv_mix_kernel, nch=tt // CHUNK),
        grid=(b // nb, t // tt),
        in_specs=[pl.BlockSpec((nb, tt, RWKV_PROJ), lambda i, j: (i, j, 0)),
                  pl.BlockSpec((nb, 1, RWKV_PROJ), lambda i, j: (i, 0, 0))]
                 + [_const_spec(c.shape) for c in consts] + [st],
        out_specs=[pl.BlockSpec((nb, tt, RWKV_WIDTH), lambda i, j: (i, j, 0)), st],
        out_shape=[jax.ShapeDtypeStruct((b, t, RWKV_WIDTH), F32), jax.ShapeDtypeStruct(s0.shape, F32)],
        scratch_shapes=[pltpu.VMEM((nb,) + s0.shape[1:], F32),
                        pltpu.VMEM((nb, 1, RWKV_PROJ), F32),
                        pltpu.VMEM((nb, tt, RWKV_WIDTH), F32)],
        compiler_params=_params(2),
        name="rwkv_mix",
    )(p, shift0, *consts, s0)


def _rwkv_step_kernel(r_ref, lw_ref, k_ref, kk_ref, a_ref, v_ref, bonus_ref, g_ref, lng_ref, lnb_ref,
                      s_ref, o_ref, so_ref):
    H = s_ref[0]
    kk = kk_ref[0]
    sa = jnp.sum(H * (-kk), axis=0)
    Hn = H * jnp.exp(lw_ref[0]) + (kk * a_ref[0]) * sa[None] + k_ref[0] * v_ref[...][None]
    so_ref[0] = Hn
    y = jnp.sum(Hn * r_ref[0], axis=0)
    d = y - jnp.mean(y, axis=0, keepdims=True)
    var = jnp.mean(d * d, axis=0, keepdims=True)
    o_ref[...] = (d * lax.rsqrt(var + RWKV_GN_EPS) * lng_ref[...] + lnb_ref[...]
                  + bonus_ref[...]) * g_ref[...]


def _rwkv_step(r, lw, k, kk, a, v, bonus, g, lng, lnb, s):
    nb = s.shape[-1]
    kvec = pl.BlockSpec((1, RWKV_HEAD, 1, nb), lambda h: (h, 0, 0, 0))
    vvec = pl.BlockSpec((RWKV_HEAD, nb), lambda h: (h, 0))
    col = pl.BlockSpec((RWKV_HEAD, 1), lambda h: (h, 0))
    st = pl.BlockSpec((1, RWKV_HEAD, RWKV_HEAD, nb), lambda h: (h, 0, 0, 0))
    return pl.pallas_call(
        _rwkv_step_kernel,
        grid=(RWKV_HEADS,),
        in_specs=[kvec] * 5 + [vvec] * 3 + [col, col, st],
        out_specs=[vvec, st],
        out_shape=[jax.ShapeDtypeStruct((RWKV_WIDTH, nb), F32), jax.ShapeDtypeStruct(s.shape, F32)],
        compiler_params=_params(1),
        name="rwkv_step",
    )(r, lw, k, kk, a, v, bonus, g, lng, lnb, s)


def _hgrn_chunk_kernel(p_ref, lb_ref, ng_ref, s0_ref, o_ref, sT_ref, s_scr, *, nch):
    C = CHUNK
    SB = HGRN_SUB
    W = HGRN_WIDTH
    tb = pl.program_id(1)

    @pl.when(tb == 0)
    def _():
        s_scr[...] = s0_ref[0]

    tt = nch * C
    row = lax.broadcasted_iota(jnp.int32, (tt, tt), 0)
    col = lax.broadcasted_iota(jnp.int32, (tt, tt), 1)
    tril_bd = ((col <= row) & (col >= (row & -C))).astype(BF16)
    lb = lb_ref[...]
    ng = ng_ref[...]
    heads = range(HGRN_HEADS)
    chunks = range(nch)
    subs = range(C // SB)
    hsl = [slice(h * HGRN_HEAD, (h + 1) * HGRN_HEAD) for h in heads]

    q = p_ref[0, :, 0:W]
    q = q * _sigmoid(q)
    f = lb + (1.0 - lb) * _sigmoid(p_ref[0, :, W:2 * W])
    kx = 1.0 - f
    iv = p_ref[0, :, 2 * W:3 * W]
    bc = _dot_exact_lhs(tril_bd, jnp.log(f))
    qe = q * jnp.exp(bc)

    sc, kend, glast = {}, {}, {}
    for c in chunks:
        c0 = c * C
        bl = bc[c0 + C - 1:c0 + C, :]
        glast[c] = jnp.exp(bl)
        kend[c] = kx[c0:c0 + C] * jnp.exp(bl - bc[c0:c0 + C])
        for i in subs:
            lo, n = c0 + i * SB, c0 + (i + 1) * SB
            if i == 0:
                eq = jnp.exp(bc[lo:n])
                ek = jnp.exp(jnp.minimum(-bc[c0:n], EXP_CLAMP))
            else:
                beta = bc[lo - 1:lo]
                eq = jnp.exp(bc[lo:n] - beta)
                ek = jnp.exp(jnp.minimum(beta - bc[c0:n], EXP_CLAMP))
            qi = q[lo:n] * eq
            ks = kx[c0:n] * ek
            keep = (lax.broadcasted_iota(jnp.int32, (SB, n - c0), 1)
                    <= lax.broadcasted_iota(jnp.int32, (SB, n - c0), 0) + i * SB)
            for h in heads:
                sc[c, h, i] = jnp.where(keep, _dot_nt(qi[:, hsl[h]], ks[:, hsl[h]]), 0.0)
    intra = {(c, h): jnp.concatenate(
        [_dot(sc[c, h, i], iv[c * C:c * C + (i + 1) * SB, hsl[h]]) for i in subs], axis=0)
        for c in chunks for h in heads}
    upd = {(c, h): _dot_tn(iv[c * C:(c + 1) * C, hsl[h]], kend[c][:, hsl[h]])
           for c in chunks for h in heads}
    St = [s_scr[h] for h in heads]
    for c in chunks:
        sl = slice(c * C, (c + 1) * C)
        for h in heads:
            o = _dot_nt(qe[sl, hsl[h]], St[h]) + intra[c, h]
            o = o * lax.rsqrt(jnp.mean(o * o, axis=-1, keepdims=True) + RMS_EPS)
            o_ref[0, sl, hsl[h]] = o * ng[:, hsl[h]] * _sigmoid(p_ref[0, sl, 3 * W + h * HGRN_HEAD:
                                                                  3 * W + (h + 1) * HGRN_HEAD])
        St = [St[h] * glast[c][:, hsl[h]] + upd[c, h] for h in heads]
    for h in heads:
        s_scr[h] = St[h]

    @pl.when(tb == pl.num_programs(1) - 1)
    def _():
        sT_ref[0] = s_scr[...]


def _hgrn_chunk(p, lb, ng, s0, tt):
    b, t, _ = p.shape
    st = pl.BlockSpec((1, HGRN_HEADS, HGRN_HEAD, HGRN_HEAD), lambda i, j: (i, 0, 0, 0))
    return pl.pallas_call(
        functools.partial(_hgrn_chunk_kernel, nch=tt // CHUNK),
        grid=(b, t // tt),
        in_specs=[pl.BlockSpec((1, tt, HGRN_PROJ), lambda i, j: (i, j, 0)),
                  _const_spec((1, HGRN_WIDTH)), _const_spec((1, HGRN_WIDTH)), st],
        out_specs=[pl.BlockSpec((1, tt, HGRN_WIDTH), lambda i, j: (i, j, 0)), st],
        out_shape=[jax.ShapeDtypeStruct((b, t, HGRN_WIDTH), F32),
                   jax.ShapeDtypeStruct(s0.shape, F32)],
        scratch_shapes=[pltpu.VMEM((HGRN_HEADS, HGRN_HEAD, HGRN_HEAD), F32)],
        compiler_params=_params(2),
        name="hgrn_chunk",
    )(p, lb, ng, s0)


def _hgrn_step_kernel(qc_ref, fc_ref, p_ref, lb_ref, ng_ref, s_ref, o_ref, so_ref):
    W = HGRN_WIDTH
    q = qc_ref[0]
    q = q * _sigmoid(q)
    lb = lb_ref[...]
    f = lb + (1.0 - lb) * _sigmoid(fc_ref[0])
    iv = p_ref[:, 2 * W:3 * W]
    gate = ng_ref[...] * _sigmoid(p_ref[:, 3 * W:4 * W])
    bh = [(b, h) for b in range(s_ref.shape[0]) for h in range(HGRN_HEADS)]
    hsl = [slice(h * HGRN_HEAD, (h + 1) * HGRN_HEAD) for h in range(HGRN_HEADS)]
    shape = (HGRN_HEAD, HGRN_HEAD)
    fcol = {(b, h): jnp.broadcast_to(f[h, :, b:b + 1], shape) for b, h in bh}
    qcol = {(b, h): jnp.broadcast_to(q[h, :, b:b + 1], shape) for b, h in bh}
    Sn = {(b, h): s_ref[b, h] * fcol[b, h] + (1.0 - fcol[b, h]) * iv[b:b + 1, hsl[h]] for b, h in bh}
    for b, h in bh:
        so_ref[b, h] = Sn[b, h]
    o = {(b, h): jnp.sum(Sn[b, h] * qcol[b, h], axis=0, keepdims=True) for b, h in bh}
    ms = {i: jnp.mean(o[i] * o[i], axis=-1, keepdims=True) for i in bh}
    for b, h in bh:
        o_ref[b:b + 1, hsl[h]] = o[b, h] * lax.rsqrt(ms[b, h] + RMS_EPS) * gate[b:b + 1, hsl[h]]


def _hgrn_step(qc, fc, p, lb, ng, s):
    nb = s.shape[0]
    rows = qc.shape[-1]
    col = pl.BlockSpec((1,) + qc.shape[1:], lambda i: (i, 0, 0, 0))
    st = pl.BlockSpec((rows,) + s.shape[1:], lambda i: (i, 0, 0, 0))
    return pl.pallas_call(
        _hgrn_step_kernel,
        grid=(nb // rows,),
        in_specs=[col, col, pl.BlockSpec((rows, HGRN_PROJ), lambda i: (i, 0)),
                  _const_spec(lb.shape), _const_spec(ng.shape), st],
        out_specs=[pl.BlockSpec((rows, HGRN_WIDTH), lambda i: (i, 0)), st],
        out_shape=[jax.ShapeDtypeStruct((nb, HGRN_WIDTH), F32), jax.ShapeDtypeStruct(s.shape, F32)],
        compiler_params=_params(1),
        name="hgrn_step",
    )(qc, fc, p, lb, ng, s)


def _merge_ffn_kernel(x_ref, oa_ref, ob_ref, gm_ref, wc_ref, wa_ref, wb_ref, wo_ref, nf_ref, wg_ref,
                      wu_ref, wd_ref, nfin_ref, out_ref):
    x = x_ref[...]
    pg = jnp.dot(_rms_norm(x, gm_ref[...]).astype(BF16), wc_ref[...], preferred_element_type=F32)
    merged = (_sigmoid(pg[:, :D_MODEL]) * _dot(oa_ref[...], wa_ref[...])
              + _sigmoid(pg[:, D_MODEL:]) * _dot(ob_ref[...], wb_ref[...]))
    x = x + _dot(merged, wo_ref[...])
    h = _rms_norm(x, nf_ref[...]).astype(BF16)
    gate = jnp.dot(h, wg_ref[...], preferred_element_type=F32)
    up = jnp.dot(h, wu_ref[...], preferred_element_type=F32)
    x = x + _dot(gate * _sigmoid(gate) * up, wd_ref[...])
    out_ref[...] = _rms_norm(x, nfin_ref[...])


def _merge_ffn(x, oa, ob, consts, tm):
    m = x.shape[0]
    row = lambda n: pl.BlockSpec((tm, n), lambda i: (i, 0))
    return pl.pallas_call(
        _merge_ffn_kernel,
        grid=(m // tm,),
        in_specs=[row(D_MODEL), row(RWKV_WIDTH), row(HGRN_WIDTH)]
                 + [_const_spec(c.shape) for c in consts],
        out_specs=row(D_MODEL),
        out_shape=jax.ShapeDtypeStruct((m, D_MODEL), F32),
        compiler_params=_params(1),
        name="merge_ffn",
    )(x, oa, ob, *consts)


def kernel(x_prompt, x_sample, state_rwkv_wkv, state_rwkv_shift, state_hgrn, norm_mix_g, w_in, rwkv_mu, rwkv_w0, rwkv_w2, rwkv_a0, rwkv_a2, rwkv_g2, rwkv_k_k, rwkv_k_a, rwkv_r_k, rwkv_ln_g, rwkv_ln_b, w_up_a, hgrn_lb, hgrn_norm_g, w_up_b, w_out, norm_ffn_g, w_ffn_gate, w_ffn_up, w_ffn_down, norm_final_g):
    bp, tp, _ = x_prompt.shape
    bs = x_sample.shape[0]
    mp = bp * tp

    w_in0 = w_in[0]
    wa_in = w_in0[:, :RWKV_PROJ].astype(BF16)
    wb_in = w_in0[:, RWKV_PROJ:RWKV_PROJ + HGRN_PROJ].astype(BF16)
    wc_in = w_in0[:, RWKV_PROJ + HGRN_PROJ:].astype(BF16)
    vec = lambda p: p.reshape(1, -1).astype(F32)
    head_id = jnp.arange(MXU_WIDTH, dtype=jnp.int32) // RWKV_HEAD
    ones_bd = (head_id[:, None] == head_id[None, :]).astype(BF16)
    prep_consts = (vec(rwkv_mu[0]), vec(rwkv_w0[0]), rwkv_w2[0], vec(rwkv_a0[0]), rwkv_a2[0],
                   rwkv_g2[0], vec(rwkv_k_k[0]), vec(rwkv_k_a[0]), vec(rwkv_r_k[0]), ones_bd)
    lb = jnp.cumsum(jax.nn.softmax(hgrn_lb.astype(F32), axis=0), axis=0)[0]
    g_mix = vec(norm_mix_g[0])
    ffn_consts = (g_mix, wc_in, w_up_a[0].astype(BF16), w_up_b[0].astype(BF16), w_out[0].astype(BF16),
                  vec(norm_ffn_g[0]), w_ffn_gate[0].astype(BF16), w_ffn_up[0].astype(BF16),
                  w_ffn_down[0].astype(BF16), vec(norm_final_g))

    xp = x_prompt.reshape(mp, D_MODEL)
    p_rwkv, p_hgrn = _norm_proj(xp, g_mix, wa_in, wb_in, tm=1024)
    p3 = p_rwkv.reshape(bp, tp, RWKV_PROJ)
    o_a, wkv_pp = _rwkv_mix(p3, jnp.zeros((bp, 1, RWKV_PROJ), F32),
                            prep_consts + (vec(rwkv_ln_g[0]), vec(rwkv_ln_b[0])),
                            jnp.zeros((bp, RWKV_HEADS // 2, RWKV_HEAD, 2 * RWKV_HEAD), F32),
                            tt=256, nb=4)
    wkv_p = (wkv_pp.reshape(bp, RWKV_HEADS // 2, RWKV_HEAD, 2, RWKV_HEAD)
             .transpose(0, 1, 3, 2, 4).reshape(bp, RWKV_HEADS, RWKV_HEAD, RWKV_HEAD))
    hg0 = jnp.zeros((bp, HGRN_HEADS, HGRN_HEAD, HGRN_HEAD), F32)
    o_b, hgT_p = _hgrn_chunk(p_hgrn.reshape(bp, tp, HGRN_PROJ), vec(lb), vec(hgrn_norm_g[0]), hg0,
                             tt=256)
    y_prompt = _merge_ffn(xp, o_a.reshape(mp, RWKV_WIDTH), o_b.reshape(mp, HGRN_WIDTH),
                          ffn_consts, tm=512).reshape(bp, tp, D_MODEL)
    shift_p = p3[:, -1]
    hgrn_p = jnp.swapaxes(hgT_p, -1, -2)

    xs = x_sample.reshape(bs, D_MODEL)
    s_rwkv, s_hgrn = _norm_proj(xs, g_mix, wa_in, wb_in, tm=bs)
    r, lw, k, v, kk, a, g, bonus = _rwkv_prep(s_rwkv, state_rwkv_shift[0], prep_consts, tm=bs)
    kcol = lambda t: t.T.reshape(RWKV_HEADS, RWKV_HEAD, 1, bs)
    wkv_t = jnp.transpose(state_rwkv_wkv[0], (1, 3, 2, 0))
    oaT, wkv_nt = _rwkv_step(kcol(r), kcol(lw), kcol(k), kcol(kk), kcol(a), v.T, bonus.T, g.T,
                             rwkv_ln_g[0].reshape(RWKV_WIDTH, 1), rwkv_ln_b[0].reshape(RWKV_WIDTH, 1),
                             wkv_t)
    wkv_s = jnp.transpose(wkv_nt, (3, 0, 2, 1))
    kmajor = lambda t: (t.reshape(bs // STEP_ROWS, STEP_ROWS, HGRN_HEADS, HGRN_HEAD)
                        .transpose(0, 2, 3, 1))
    o_bs, hgrn_s = _hgrn_step(kmajor(s_hgrn[:, :HGRN_WIDTH]), kmajor(s_hgrn[:, HGRN_WIDTH:2 * HGRN_WIDTH]),
                              s_hgrn, lb.reshape(HGRN_HEADS, HGRN_HEAD, 1), vec(hgrn_norm_g[0]),
                              state_hgrn[0])
    y_sample = _merge_ffn(xs, oaT.T, o_bs, ffn_consts, tm=bs).reshape(bs, 1, D_MODEL)

    return (y_prompt, y_sample, wkv_p[None], shift_p[None], hgrn_p[None],
            wkv_s[None], s_rwkv[None], hgrn_s[None])
```

```python
import functools

import jax
import jax.numpy as jnp
from jax import lax
from jax.experimental import pallas as pl
from jax.experimental.pallas import tpu as pltpu

F32 = jnp.float32
BF16 = jnp.bfloat16

D_MODEL = 1024
RWKV_WIDTH = 512
RWKV_HEAD = 64
RWKV_HEADS = 8
RWKV_DECAY_LORA = 64
RWKV_A_LORA = 64
RWKV_GATE_LORA = 128
RWKV_PROJ = 3 * RWKV_WIDTH + RWKV_DECAY_LORA + RWKV_A_LORA + RWKV_GATE_LORA
RWKV_GN_EPS = 64e-5
L2_EPS = 1e-12
HGRN_WIDTH = 512
HGRN_HEADS = 4
HGRN_HEAD = 128
HGRN_PROJ = 4 * HGRN_WIDTH
GATE_PROJ = 2 * D_MODEL
D_FF = 2816
RMS_EPS = 1e-6

MXU_WIDTH = 256
CHUNK = 64
HGRN_SUB = 16
STEP_ROWS = 16
EXP_CLAMP = 80.0
VMEM_LIMIT = 56 * 1024 * 1024


def _dot(a, b):
    return jnp.dot(a.astype(BF16), b.astype(BF16), preferred_element_type=F32)


def _dot_nt(a, b):
    return lax.dot_general(a.astype(BF16), b.astype(BF16), (((1,), (1,)), ((), ())),
                           preferred_element_type=F32)


def _dot_tn(a, b):
    return lax.dot_general(a.astype(BF16), b.astype(BF16), (((0,), (0,)), ((), ())),
                           preferred_element_type=F32)


def _split2(x):
    hi = x.astype(BF16)
    lo = (x - hi.astype(F32)).astype(BF16)
    return hi, lo


def _split3(x):
    hi = x.astype(BF16)
    r1 = x - hi.astype(F32)
    mid = r1.astype(BF16)
    lo = (r1 - mid.astype(F32)).astype(BF16)
    return hi, mid, lo


def _dot_x3(a, b):
    ah, al = _split2(a)
    bh, bl = _split2(b)
    return _dot(ah, bh) + (_dot(ah, bl) + _dot(al, bh))


def _dot_exact_lhs(a_bf16, b):
    bh, bm, bl = _split3(b)
    return _dot(a_bf16, bh) + (_dot(a_bf16, bm) + _dot(a_bf16, bl))


def _sigmoid(x):
    return 1.0 / (1.0 + jnp.exp(-x))


def _rms_norm(x, g):
    return x * lax.rsqrt(jnp.mean(x * x, axis=-1, keepdims=True) + RMS_EPS) * g


def _head_sum(x, ones_bd):
    n = ones_bd.shape[0]
    xb = x.astype(BF16)
    return jnp.concatenate([jnp.dot(xb[:, j:j + n], ones_bd, preferred_element_type=F32)
                            for j in range(0, x.shape[1], n)], axis=1)


def _const_spec(shape):
    nd = len(shape)
    return pl.BlockSpec(shape, lambda *_: (0,) * nd, pipeline_mode=pl.Buffered(1))


def _params(n_grid):
    return pltpu.CompilerParams(dimension_semantics=("arbitrary",) * n_grid,
                                vmem_limit_bytes=VMEM_LIMIT)


def _norm_proj_kernel(x_ref, g_ref, wa_ref, wb_ref, oa_ref, ob_ref):
    h = _rms_norm(x_ref[...], g_ref[...]).astype(BF16)
    oa_ref[...] = jnp.dot(h, wa_ref[...], preferred_element_type=F32)
    ob_ref[...] = jnp.dot(h, wb_ref[...], preferred_element_type=F32)


def _norm_proj(x, g, wa, wb, tm):
    m = x.shape[0]
    row = lambda n: pl.BlockSpec((tm, n), lambda i: (i, 0))
    return pl.pallas_call(
        _norm_proj_kernel,
        grid=(m // tm,),
        in_specs=[row(D_MODEL), _const_spec((1, D_MODEL)), _const_spec(wa.shape),
                  _const_spec(wb.shape)],
        out_specs=[row(RWKV_PROJ), row(HGRN_PROJ)],
        out_shape=[jax.ShapeDtypeStruct((m, RWKV_PROJ), F32),
                   jax.ShapeDtypeStruct((m, HGRN_PROJ), F32)],
        compiler_params=_params(1),
        name="norm_proj",
    )(x, g, wa, wb)


def _rwkv_token_math(p, pprev, consts):
    mu_ref, w0_ref, w2_ref, a0_ref, a2_ref, g2_ref, kk_ref, ka_ref, rk_ref, ones_ref = consts
    W = RWKV_WIDTH
    ps = p + mu_ref[...] * (pprev - p)
    r = ps[:, 0:W]
    k = ps[:, W:2 * W]
    v = ps[:, 2 * W:3 * W]
    o0 = 3 * W
    wd = ps[:, o0:o0 + RWKV_DECAY_LORA]
    ad = ps[:, o0 + RWKV_DECAY_LORA:o0 + RWKV_DECAY_LORA + RWKV_A_LORA]
    gd = ps[:, o0 + RWKV_DECAY_LORA + RWKV_A_LORA:]
    z = -(w0_ref[...] + _dot_x3(jnp.tanh(wd), w2_ref[...]))
    softplus = jnp.maximum(z, 0.0) + jnp.log(1.0 + jnp.exp(-jnp.abs(z)))
    lw = -jnp.exp(-softplus - 0.5)
    a = _sigmoid(a0_ref[...] + _dot_x3(ad, a2_ref[...]))
    g = _dot_x3(_sigmoid(gd), g2_ref[...])
    ones_bd = ones_ref[...]
    kk = k * kk_ref[...]
    norm = jnp.sqrt(_head_sum(kk * kk, ones_bd))
    kk = kk / jnp.maximum(norm, L2_EPS)
    k = k * (1.0 + (a - 1.0) * ka_ref[...])
    bonus = _head_sum(r * k * rk_ref[...], ones_bd) * v
    return r, lw, k, v, kk, a, g, bonus


def _rwkv_prep_kernel(p_ref, pp_ref, *refs):
    consts, outs = refs[:10], refs[10:]
    for o_ref, val in zip(outs, _rwkv_token_math(p_ref[...], pp_ref[...], consts)):
        o_ref[...] = val


def _rwkv_prep(p, pprev, consts, tm):
    m = p.shape[0]
    row = lambda n: pl.BlockSpec((tm, n), lambda i: (i, 0))
    return pl.pallas_call(
        _rwkv_prep_kernel,
        grid=(m // tm,),
        in_specs=[row(RWKV_PROJ), row(RWKV_PROJ)] + [_const_spec(c.shape) for c in consts],
        out_specs=[row(RWKV_WIDTH)] * 8,
        out_shape=[jax.ShapeDtypeStruct((m, RWKV_WIDTH), F32)] * 8,
        compiler_params=_params(1),
        name="rwkv_prep",
    )(p, pprev, *consts)


def _rwkv_mix_kernel(p_ref, sh0_ref, mu_ref, w0_ref, w2_ref, a0_ref, a2_ref, g2_ref, kk_ref, ka_ref,
                     rk_ref, ones_ref, lng_ref, lnb_ref, s0_ref, o_ref, sT_ref,
                     s_scr, prev_scr, y_scr, *, nch):
    C = CHUNK
    tb = pl.program_id(1)

    @pl.when(tb == 0)
    def _():
        s_scr[...] = s0_ref[...]
        prev_scr[...] = sh0_ref[...]

    row = lax.broadcasted_iota(jnp.int32, (C, C), 0)
    col = lax.broadcasted_iota(jnp.int32, (C, C), 1)
    tril_incl = (col <= row).astype(BF16)
    PL = 2 * RWKV_HEAD
    t_row = lax.broadcasted_iota(jnp.int32, (C, 2 * PL), 0)
    s_col = lax.broadcasted_iota(jnp.int32, (C, 2 * PL), 1) & (RWKV_HEAD - 1)
    strict = (s_col < t_row)[:, :PL]
    incl = s_col <= t_row
    eye = jnp.where((s_col == t_row)[:, :PL], 1.0, 0.0)
    low_head = lax.broadcasted_iota(jnp.int32, (C, PL), 1) < RWKV_HEAD

    def bd(y):
        return jnp.concatenate([jnp.where(low_head, y, 0.0), jnp.where(low_head, 0.0, y)], axis=0)

    pairs = range(RWKV_HEADS // 2)
    psl = [slice(j * PL, (j + 1) * PL) for j in pairs]
    seqs = range(p_ref.shape[0])
    ch = [(b, c, j) for b in seqs for c in range(nch) for j in pairs]
    consts = (mu_ref, w0_ref, w2_ref, a0_ref, a2_ref, g2_ref, kk_ref, ka_ref, rk_ref, ones_ref)

    glast, AT, RT, VV, L, R, BK, gate, bonus = {}, {}, {}, {}, {}, {}, {}, {}, {}
    for b in seqs:
        p = p_ref[b]
        rows = lax.broadcasted_iota(jnp.int32, p.shape, 0)
        pprev = jnp.where(rows == 0, prev_scr[b], pltpu.roll(p, 1, 0))
        prev_scr[b] = p[p.shape[0] - 1:, :]
        r_all, lw_all, k_all, v_all, kk_all, a_all, gate[b], bonus[b] = _rwkv_token_math(p, pprev, consts)
        kb_all = kk_all * a_all
        for c in range(nch):
            sl = slice(c * C, (c + 1) * C)
            lw = lw_all[sl]
            k = k_all[sl]
            kk = kk_all[sl]
            kb = kb_all[sl]
            cum = _dot_exact_lhs(tril_incl, lw)
            cl = cum[C - 1:C, :]
            eneg = jnp.exp(-cum)
            elast = jnp.exp(cl - cum)
            glast[b, c] = jnp.exp(cl)
            at = -kk * jnp.exp(cum - lw)
            rt = r_all[sl] * jnp.exp(cum)
            bt = kb * eneg
            kt = k * eneg
            bh = kb * elast
            kh = k * elast
            v = v_all[sl]
            for j, ps in enumerate(psl):
                AT[b, c, j], RT[b, c, j], VV[b, c, j] = at[:, ps], rt[:, ps], v[:, ps]
                L[b, c, j] = jnp.concatenate([at[:, ps], rt[:, ps]], axis=0)
                R[b, c, j] = jnp.concatenate([bd(bt[:, ps]), bd(kt[:, ps])], axis=0)
                BK[b, c, j] = jnp.concatenate([bh[:, ps], kh[:, ps]], axis=0)
    P = {i: _dot_nt(L[i], R[i]) for i in ch}
    A = {i: jnp.where(strict, P[i][:C, :PL], 0.0) for i in ch}
    Aak = {i: jnp.where(strict, P[i][:C, PL:], 0.0) for i in ch}
    Ar = {i: jnp.where(incl, P[i][C:, :], 0.0) for i in ch}
    X = {i: _dot(A[i], bd(A[i])) for i in ch}
    Tm = {i: eye + A[i] for i in ch}
    for _ in range(4):
        Z = {i: _dot(X[i], jnp.concatenate([bd(X[i]), bd(Tm[i])], axis=1)) for i in ch}
        Tm = {i: Tm[i] + Z[i][:, PL:] for i in ch}
        X = {i: Z[i][:, :PL] for i in ch}
    AV = {i: _dot(Aak[i], bd(VV[i])) for i in ch}
    M = {i: _dot(Tm[i], jnp.concatenate([bd(AT[i]), bd(AV[i])], axis=1)) for i in ch}
    WU = {i: M[i] + _dot(X[i], jnp.concatenate([bd(M[i][:, :PL]), bd(M[i][:, PL:])], axis=1))
          for i in ch}
    WR = {i: jnp.concatenate([WU[i][:, :PL], RT[i]], axis=0) for i in ch}

    bj = [(b, j) for b in seqs for j in pairs]
    S = {(b, j): s_scr[b, j] for b, j in bj}
    for c in range(nch):
        sl = slice(c * C, (c + 1) * C)
        WRS = {(b, j): _dot_nt(WR[b, c, j], bd(S[b, j])) for b, j in bj}
        U = {(b, j): WRS[b, j][:C] + WU[b, c, j][:, PL:] for b, j in bj}
        UV = {(b, j): jnp.concatenate([U[b, j], VV[b, c, j]], axis=0) for b, j in bj}
        full = {(b, j): _dot_tn(UV[b, j], BK[b, c, j]) for b, j in bj}
        S = {(b, j): S[b, j] * glast[b, c][:, psl[j]]
             + jnp.where(low_head, full[b, j][:C], full[b, j][C:]) for b, j in bj}
        for b, j in bj:
            y_scr[b, sl, psl[j]] = WRS[b, j][C:] + _dot(
                Ar[b, c, j], jnp.concatenate([bd(U[b, j]), bd(VV[b, c, j])], axis=0))
    for b, j in bj:
        s_scr[b, j] = S[b, j]

    ones_bd = ones_ref[...]
    inv_n = 1.0 / RWKV_HEAD
    for b in seqs:
        y = y_scr[b]
        d = y - _head_sum(y, ones_bd) * inv_n
        var = _head_sum(d * d, ones_bd) * inv_n
        o_ref[b] = (d * lax.rsqrt(var + RWKV_GN_EPS) * lng_ref[...] + lnb_ref[...] + bonus[b]) * gate[b]

    @pl.when(tb == pl.num_programs(1) - 1)
    def _():
        sT_ref[...] = s_scr[...]


def _rwkv_mix(p, shift0, consts, s0, tt, nb):
    b, t, _ = p.shape
    st = pl.BlockSpec((nb,) + s0.shape[1:], lambda i, j: (i, 0, 0, 0))
    return pl.pallas_call(
        functools.partial(_rwkv_mix_kernel, nch=tt // CHUNK),
        grid=(b // nb, t // tt),
        in_specs=[pl.BlockSpec((nb, tt, RWKV_PROJ), lambda i, j: (i, j, 0)),
                  pl.BlockSpec((nb, 1, RWKV_PROJ), lambda i, j: (i, 0, 0))]
                 + [_const_spec(c.shape) for c in consts] + [st],
        out_specs=[pl.BlockSpec((nb, tt, RWKV_WIDTH), lambda i, j: (i, j, 0)), st],
        out_shape=[jax.ShapeDtypeStruct((b, t, RWKV_WIDTH), F32), jax.ShapeDtypeStruct(s0.shape, F32)],
        scratch_shapes=[pltpu.VMEM((nb,) + s0.shape[1:], F32),
                        pltpu.VMEM((nb, 1, RWKV_PROJ), F32),
                        pltpu.VMEM((nb, tt, RWKV_WIDTH), F32)],
        compiler_params=_params(2),
        name="rwkv_mix",
    )(p, shift0, *consts, s0)


def _rwkv_step_kernel(r_ref, lw_ref, k_ref, kk_ref, a_ref, v_ref, bonus_ref, g_ref, lng_ref, lnb_ref,
                      s_ref, o_ref, so_ref):
    H = s_ref[0]
    over_v = lambda t: t[:, None, :]
    kk = over_v(kk_ref[...])
    sa = jnp.sum(H * (-kk), axis=0)
    Hn = (H * over_v(jnp.exp(lw_ref[...])) + (kk * over_v(a_ref[...])) * sa[None]
          + over_v(k_ref[...]) * v_ref[...][None])
    so_ref[0] = Hn
    y = jnp.sum(Hn * over_v(r_ref[...]), axis=0)
    d = y - jnp.mean(y, axis=0, keepdims=True)
    var = jnp.mean(d * d, axis=0, keepdims=True)
    o_ref[...] = (d * lax.rsqrt(var + RWKV_GN_EPS) * lng_ref[...] + lnb_ref[...]
                  + bonus_ref[...]) * g_ref[...]


def _rwkv_step(r, lw, k, kk, a, v, bonus, g, lng, lnb, s):
    nb = s.shape[-1]
    vvec = pl.BlockSpec((RWKV_HEAD, nb), lambda h: (h, 0))
    col = pl.BlockSpec((RWKV_HEAD, 1), lambda h: (h, 0))
    st = pl.BlockSpec((1, RWKV_HEAD, RWKV_HEAD, nb), lambda h: (h, 0, 0, 0))
    return pl.pallas_call(
        _rwkv_step_kernel,
        grid=(RWKV_HEADS,),
        in_specs=[vvec] * 8 + [col, col, st],
        out_specs=[vvec, st],
        out_shape=[jax.ShapeDtypeStruct((RWKV_WIDTH, nb), F32), jax.ShapeDtypeStruct(s.shape, F32)],
        compiler_params=_params(1),
        name="rwkv_step",
    )(r, lw, k, kk, a, v, bonus, g, lng, lnb, s)


def _hgrn_chunk_kernel(p_ref, lb_ref, ng_ref, s0_ref, o_ref, sT_ref, s_scr, *, nch):
    C = CHUNK
    SB = HGRN_SUB
    W = HGRN_WIDTH
    tb = pl.program_id(1)

    @pl.when(tb == 0)
    def _():
        s_scr[...] = s0_ref[...]

    tt = nch * C
    row = lax.broadcasted_iota(jnp.int32, (tt, tt), 0)
    col = lax.broadcasted_iota(jnp.int32, (tt, tt), 1)
    tril_bd = ((col <= row) & (col >= (row & -C))).astype(BF16)
    lb = lb_ref[...]
    ng = ng_ref[...]
    heads = range(HGRN_HEADS)
    chunks = range(nch)
    subs = range(C // SB)
    hsl = [slice(h * HGRN_HEAD, (h + 1) * HGRN_HEAD) for h in heads]

    seqs = range(p_ref.shape[0])
    q, iv, qe, sc, kend, glast = {}, {}, {}, {}, {}, {}
    for b in seqs:
        qb = p_ref[b, :, 0:W]
        q[b] = qb * _sigmoid(qb)
        f = lb + (1.0 - lb) * _sigmoid(p_ref[b, :, W:2 * W])
        kx = 1.0 - f
        iv[b] = p_ref[b, :, 2 * W:3 * W]
        bc = _dot_exact_lhs(tril_bd, jnp.log(f))
        qe[b] = q[b] * jnp.exp(bc)
        for c in chunks:
            c0 = c * C
            bl = bc[c0 + C - 1:c0 + C, :]
            glast[b, c] = jnp.exp(bl)
            kend[b, c] = kx[c0:c0 + C] * jnp.exp(bl - bc[c0:c0 + C])
            for i in subs:
                lo, n = c0 + i * SB, c0 + (i + 1) * SB
                if i == 0:
                    eq = jnp.exp(bc[lo:n])
                    ek = jnp.exp(jnp.minimum(-bc[c0:n], EXP_CLAMP))
                else:
                    beta = bc[lo - 1:lo]
                    eq = jnp.exp(bc[lo:n] - beta)
                    ek = jnp.exp(jnp.minimum(beta - bc[c0:n], EXP_CLAMP))
                qi = q[b][lo:n] * eq
                ks = kx[c0:n] * ek
                keep = (lax.broadcasted_iota(jnp.int32, (SB, n - c0), 1)
                        <= lax.broadcasted_iota(jnp.int32, (SB, n - c0), 0) + i * SB)
                for h in heads:
                    sc[b, c, h, i] = jnp.where(keep, _dot_nt(qi[:, hsl[h]], ks[:, hsl[h]]), 0.0)
    bch = [(b, c, h) for b in seqs for c in chunks for h in heads]
    intra = {(b, c, h): jnp.concatenate(
        [_dot(sc[b, c, h, i], iv[b][c * C:c * C + (i + 1) * SB, hsl[h]]) for i in subs], axis=0)
        for b, c, h in bch}
    upd = {(b, c, h): _dot_tn(iv[b][c * C:(c + 1) * C, hsl[h]], kend[b, c][:, hsl[h]]) for b, c, h in bch}
    St = {(b, h): s_scr[b, h] for b in seqs for h in heads}
    for c in chunks:
        sl = slice(c * C, (c + 1) * C)
        for b in seqs:
            for h in heads:
                o = _dot_nt(qe[b][sl, hsl[h]], St[b, h]) + intra[b, c, h]
                o = o * lax.rsqrt(jnp.mean(o * o, axis=-1, keepdims=True) + RMS_EPS)
                o_ref[b, sl, hsl[h]] = o * ng[:, hsl[h]] * _sigmoid(
                    p_ref[b, sl, 3 * W + h * HGRN_HEAD:3 * W + (h + 1) * HGRN_HEAD])
        St = {(b, h): St[b, h] * glast[b, c][:, hsl[h]] + upd[b, c, h] for b in seqs for h in heads}
    for b in seqs:
        for h in heads:
            s_scr[b, h] = St[b, h]

    @pl.when(tb == pl.num_programs(1) - 1)
    def _():
        sT_ref[...] = s_scr[...]


def _hgrn_chunk(p, lb, ng, s0, tt, nb):
    b, t, _ = p.shape
    st = pl.BlockSpec((nb, HGRN_HEADS, HGRN_HEAD, HGRN_HEAD), lambda i, j: (i, 0, 0, 0))
    return pl.pallas_call(
        functools.partial(_hgrn_chunk_kernel, nch=tt // CHUNK),
        grid=(b // nb, t // tt),
        in_specs=[pl.BlockSpec((nb, tt, HGRN_PROJ), lambda i, j: (i, j, 0)),
                  _const_spec((1, HGRN_WIDTH)), _const_spec((1, HGRN_WIDTH)), st],
        out_specs=[pl.BlockSpec((nb, tt, HGRN_WIDTH), lambda i, j: (i, j, 0)), st],
        out_shape=[jax.ShapeDtypeStruct((b, t, HGRN_WIDTH), F32),
                   jax.ShapeDtypeStruct(s0.shape, F32)],
        scratch_shapes=[pltpu.VMEM((nb, HGRN_HEADS, HGRN_HEAD, HGRN_HEAD), F32)],
        compiler_params=_params(2),
        name="hgrn_chunk",
    )(p, lb, ng, s0)


def _hgrn_step_kernel(qc_ref, fc_ref, p_ref, lb_ref, ng_ref, s_ref, o_ref, so_ref):
    W = HGRN_WIDTH
    q = qc_ref[0]
    q = q * _sigmoid(q)
    lb = lb_ref[...]
    f = lb + (1.0 - lb) * _sigmoid(fc_ref[0])
    iv = p_ref[:, 2 * W:3 * W]
    gate = ng_ref[...] * _sigmoid(p_ref[:, 3 * W:4 * W])
    bh = [(b, h) for b in range(s_ref.shape[0]) for h in range(HGRN_HEADS)]
    hsl = [slice(h * HGRN_HEAD, (h + 1) * HGRN_HEAD) for h in range(HGRN_HEADS)]
    shape = (HGRN_HEAD, HGRN_HEAD)
    fcol = {(b, h): jnp.broadcast_to(f[h, :, b:b + 1], shape) for b, h in bh}
    qcol = {(b, h): jnp.broadcast_to(q[h, :, b:b + 1], shape) for b, h in bh}
    Sn = {(b, h): s_ref[b, h] * fcol[b, h] + (1.0 - fcol[b, h]) * iv[b:b + 1, hsl[h]] for b, h in bh}
    for b, h in bh:
        so_ref[b, h] = Sn[b, h]
    o = {(b, h): jnp.sum(Sn[b, h] * qcol[b, h], axis=0, keepdims=True) for b, h in bh}
    ms = {i: jnp.mean(o[i] * o[i], axis=-1, keepdims=True) for i in bh}
    for b, h in bh:
        o_ref[b:b + 1, hsl[h]] = o[b, h] * lax.rsqrt(ms[b, h] + RMS_EPS) * gate[b:b + 1, hsl[h]]


def _hgrn_step(qc, fc, p, lb, ng, s):
    nb = s.shape[0]
    rows = qc.shape[-1]
    col = pl.BlockSpec((1,) + qc.shape[1:], lambda i: (i, 0, 0, 0))
    st = pl.BlockSpec((rows,) + s.shape[1:], lambda i: (i, 0, 0, 0))
    return pl.pallas_call(
        _hgrn_step_kernel,
        grid=(nb // rows,),
        in_specs=[col, col, pl.BlockSpec((rows, HGRN_PROJ), lambda i: (i, 0)),
                  _const_spec(lb.shape), _const_spec(ng.shape), st],
        out_specs=[pl.BlockSpec((rows, HGRN_WIDTH), lambda i: (i, 0)), st],
        out_shape=[jax.ShapeDtypeStruct((nb, HGRN_WIDTH), F32), jax.ShapeDtypeStruct(s.shape, F32)],
        compiler_params=_params(1),
        name="hgrn_step",
    )(qc, fc, p, lb, ng, s)


def _merge_ffn_kernel(x_ref, oa_ref, ob_ref, gm_ref, wc_ref, wa_ref, wb_ref, wo_ref, nf_ref, wg_ref,
                      wu_ref, wd_ref, nfin_ref, out_ref):
    x = x_ref[...]
    pg = jnp.dot(_rms_norm(x, gm_ref[...]).astype(BF16), wc_ref[...], preferred_element_type=F32)
    merged = (_sigmoid(pg[:, :D_MODEL]) * _dot(oa_ref[...], wa_ref[...])
              + _sigmoid(pg[:, D_MODEL:]) * _dot(ob_ref[...], wb_ref[...]))
    x = x + _dot(merged, wo_ref[...])
    h = _rms_norm(x, nf_ref[...]).astype(BF16)
    gate = jnp.dot(h, wg_ref[...], preferred_element_type=F32)
    up = jnp.dot(h, wu_ref[...], preferred_element_type=F32)
    x = x + _dot(gate * _sigmoid(gate) * up, wd_ref[...])
    out_ref[...] = _rms_norm(x, nfin_ref[...])


def _merge_ffn(x, oa, ob, consts, tm):
    m = x.shape[0]
    row = lambda n: pl.BlockSpec((tm, n), lambda i: (i, 0))
    return pl.pallas_call(
        _merge_ffn_kernel,
        grid=(m // tm,),
        in_specs=[row(D_MODEL), row(RWKV_WIDTH), row(HGRN_WIDTH)]
                 + [_const_spec(c.shape) for c in consts],
        out_specs=row(D_MODEL),
        out_shape=jax.ShapeDtypeStruct((m, D_MODEL), F32),
        compiler_params=_params(1),
        name="merge_ffn",
    )(x, oa, ob, *consts)


def kernel(x_prompt, x_sample, state_rwkv_wkv, state_rwkv_shift, state_hgrn, norm_mix_g, w_in, rwkv_mu, rwkv_w0, rwkv_w2, rwkv_a0, rwkv_a2, rwkv_g2, rwkv_k_k, rwkv_k_a, rwkv_r_k, rwkv_ln_g, rwkv_ln_b, w_up_a, hgrn_lb, hgrn_norm_g, w_up_b, w_out, norm_ffn_g, w_ffn_gate, w_ffn_up, w_ffn_down, norm_final_g):
    bp, tp, _ = x_prompt.shape
    bs = x_sample.shape[0]
    mp = bp * tp

    w_in0 = w_in[0]
    wa_in = w_in0[:, :RWKV_PROJ].astype(BF16)
    wb_in = w_in0[:, RWKV_PROJ:RWKV_PROJ + HGRN_PROJ].astype(BF16)
    wc_in = w_in0[:, RWKV_PROJ + HGRN_PROJ:].astype(BF16)
    vec = lambda p: p.reshape(1, -1).astype(F32)
    head_id = jnp.arange(MXU_WIDTH, dtype=jnp.int32) // RWKV_HEAD
    ones_bd = (head_id[:, None] == head_id[None, :]).astype(BF16)
    prep_consts = (vec(rwkv_mu[0]), vec(rwkv_w0[0]), rwkv_w2[0], vec(rwkv_a0[0]), rwkv_a2[0],
                   rwkv_g2[0], vec(rwkv_k_k[0]), vec(rwkv_k_a[0]), vec(rwkv_r_k[0]), ones_bd)
    lb = jnp.cumsum(jax.nn.softmax(hgrn_lb.astype(F32), axis=0), axis=0)[0]
    g_mix = vec(norm_mix_g[0])
    ffn_consts = (g_mix, wc_in, w_up_a[0].astype(BF16), w_up_b[0].astype(BF16), w_out[0].astype(BF16),
                  vec(norm_ffn_g[0]), w_ffn_gate[0].astype(BF16), w_ffn_up[0].astype(BF16),
                  w_ffn_down[0].astype(BF16), vec(norm_final_g))

    xp = x_prompt.reshape(mp, D_MODEL)
    p_rwkv, p_hgrn = _norm_proj(xp, g_mix, wa_in, wb_in, tm=1024)
    p3 = p_rwkv.reshape(bp, tp, RWKV_PROJ)
    o_a, wkv_pp = _rwkv_mix(p3, jnp.zeros((bp, 1, RWKV_PROJ), F32),
                            prep_consts + (vec(rwkv_ln_g[0]), vec(rwkv_ln_b[0])),
                            jnp.zeros((bp, RWKV_HEADS // 2, RWKV_HEAD, 2 * RWKV_HEAD), F32),
                            tt=256, nb=4)
    wkv_p = (wkv_pp.reshape(bp, RWKV_HEADS // 2, RWKV_HEAD, 2, RWKV_HEAD)
             .transpose(0, 1, 3, 2, 4).reshape(bp, RWKV_HEADS, RWKV_HEAD, RWKV_HEAD))
    hg0 = jnp.zeros((bp, HGRN_HEADS, HGRN_HEAD, HGRN_HEAD), F32)
    o_b, hgT_p = _hgrn_chunk(p_hgrn.reshape(bp, tp, HGRN_PROJ), vec(lb), vec(hgrn_norm_g[0]), hg0,
                             tt=256, nb=2)
    y_prompt = _merge_ffn(xp, o_a.reshape(mp, RWKV_WIDTH), o_b.reshape(mp, HGRN_WIDTH),
                          ffn_consts, tm=512).reshape(bp, tp, D_MODEL)
    shift_p = p3[:, -1]
    hgrn_p = jnp.swapaxes(hgT_p, -1, -2)

    xs = x_sample.reshape(bs, D_MODEL)
    s_rwkv, s_hgrn = _norm_proj(xs, g_mix, wa_in, wb_in, tm=bs)
    r, lw, k, v, kk, a, g, bonus = _rwkv_prep(s_rwkv, state_rwkv_shift[0], prep_consts, tm=bs)
    wkv_t = jnp.transpose(state_rwkv_wkv[0], (1, 3, 2, 0))
    oaT, wkv_nt = _rwkv_step(r.T, lw.T, k.T, kk.T, a.T, v.T, bonus.T, g.T,
                             rwkv_ln_g[0].reshape(RWKV_WIDTH, 1), rwkv_ln_b[0].reshape(RWKV_WIDTH, 1),
                             wkv_t)
    wkv_s = jnp.transpose(wkv_nt, (3, 0, 2, 1))
    kmajor = lambda t: (t.reshape(bs // STEP_ROWS, STEP_ROWS, HGRN_HEADS, HGRN_HEAD)
                        .transpose(0, 2, 3, 1))
    o_bs, hgrn_s = _hgrn_step(kmajor(s_hgrn[:, :HGRN_WIDTH]), kmajor(s_hgrn[:, HGRN_WIDTH:2 * HGRN_WIDTH]),
                              s_hgrn, lb.reshape(HGRN_HEADS, HGRN_HEAD, 1), vec(hgrn_norm_g[0]),
                              state_hgrn[0])
    y_sample = _merge_ffn(xs, oaT.T, o_bs, ffn_consts, tm=bs).reshape(bs, 1, D_MODEL)

    return (y_prompt, y_sample, wkv_p[None], shift_p[None], hgrn_p[None],
            wkv_s[None], s_rwkv[None], hgrn_s[None])
```

```python
import functools

import jax
import jax.numpy as jnp
from jax import lax
from jax.experimental import pallas as pl
from jax.experimental.pallas import tpu as pltpu

F32 = jnp.float32
BF16 = jnp.bfloat16

D_MODEL = 1024
RWKV_WIDTH = 512
RWKV_HEAD = 64
RWKV_HEADS = 8
RWKV_DECAY_LORA = 64
RWKV_A_LORA = 64
RWKV_GATE_LORA = 128
RWKV_PROJ = 3 * RWKV_WIDTH + RWKV_DECAY_LORA + RWKV_A_LORA + RWKV_GATE_LORA
RWKV_GN_EPS = 64e-5
L2_EPS = 1e-12
HGRN_WIDTH = 512
HGRN_HEADS = 4
HGRN_HEAD = 128
HGRN_PROJ = 4 * HGRN_WIDTH
GATE_PROJ = 2 * D_MODEL
D_FF = 2816
RMS_EPS = 1e-6

MXU_WIDTH = 256
CHUNK = 64
HGRN_SUB = 16
STEP_ROWS = 16
EXP_CLAMP = 80.0
VMEM_LIMIT = 56 * 1024 * 1024


def _dot(a, b):
    return jnp.dot(a.astype(BF16), b.astype(BF16), preferred_element_type=F32)


def _dot_nt(a, b):
    return lax.dot_general(a.astype(BF16), b.astype(BF16), (((1,), (1,)), ((), ())),
                           preferred_element_type=F32)


def _dot_tn(a, b):
    return lax.dot_general(a.astype(BF16), b.astype(BF16), (((0,), (0,)), ((), ())),
                           preferred_element_type=F32)


def _split2(x):
    hi = x.astype(BF16)
    lo = (x - hi.astype(F32)).astype(BF16)
    return hi, lo


def _split3(x):
    hi = x.astype(BF16)
    r1 = x - hi.astype(F32)
    mid = r1.astype(BF16)
    lo = (r1 - mid.astype(F32)).astype(BF16)
    return hi, mid, lo


def _dot_x3(a, b):
    ah, al = _split2(a)
    bh, bl = _split2(b)
    return _dot(ah, bh) + (_dot(ah, bl) + _dot(al, bh))


def _dot_exact_lhs(a_bf16, b):
    bh, bm, bl = _split3(b)
    return _dot(a_bf16, bh) + (_dot(a_bf16, bm) + _dot(a_bf16, bl))


def _sigmoid(x):
    return 1.0 / (1.0 + jnp.exp(-x))


def _rms_norm(x, g):
    return x * lax.rsqrt(jnp.mean(x * x, axis=-1, keepdims=True) + RMS_EPS) * g


def _head_sum(x, ones_bd):
    n = ones_bd.shape[0]
    xb = x.astype(BF16)
    return jnp.concatenate([jnp.dot(xb[:, j:j + n], ones_bd, preferred_element_type=F32)
                            for j in range(0, x.shape[1], n)], axis=1)


def _const_spec(shape):
    nd = len(shape)
    return pl.BlockSpec(shape, lambda *_: (0,) * nd, pipeline_mode=pl.Buffered(1))


def _params(n_grid):
    return pltpu.CompilerParams(dimension_semantics=("arbitrary",) * n_grid,
                                vmem_limit_bytes=VMEM_LIMIT)


def _norm_proj_kernel(x_ref, g_ref, wa_ref, wb_ref, oa_ref, ob_ref):
    h = _rms_norm(x_ref[...], g_ref[...]).astype(BF16)
    oa_ref[...] = jnp.dot(h, wa_ref[...], preferred_element_type=F32)
    ob_ref[...] = jnp.dot(h, wb_ref[...], preferred_element_type=F32)


def _norm_proj(x, g, wa, wb, tm):
    m = x.shape[0]
    row = lambda n: pl.BlockSpec((tm, n), lambda i: (i, 0))
    return pl.pallas_call(
        _norm_proj_kernel,
        grid=(m // tm,),
        in_specs=[row(D_MODEL), _const_spec((1, D_MODEL)), _const_spec(wa.shape),
                  _const_spec(wb.shape)],
        out_specs=[row(RWKV_PROJ), row(HGRN_PROJ)],
        out_shape=[jax.ShapeDtypeStruct((m, RWKV_PROJ), F32),
                   jax.ShapeDtypeStruct((m, HGRN_PROJ), F32)],
        compiler_params=_params(1),
        name="norm_proj",
    )(x, g, wa, wb)


def _rwkv_token_math(p, pprev, consts):
    mu_ref, w0_ref, w2_ref, a0_ref, a2_ref, g2_ref, kk_ref, ka_ref, rk_ref, ones_ref = consts
    W = RWKV_WIDTH
    ps = p + mu_ref[...] * (pprev - p)
    r = ps[:, 0:W]
    k = ps[:, W:2 * W]
    v = ps[:, 2 * W:3 * W]
    o0 = 3 * W
    wd = ps[:, o0:o0 + RWKV_DECAY_LORA]
    ad = ps[:, o0 + RWKV_DECAY_LORA:o0 + RWKV_DECAY_LORA + RWKV_A_LORA]
    gd = ps[:, o0 + RWKV_DECAY_LORA + RWKV_A_LORA:]
    z = -(w0_ref[...] + _dot_x3(jnp.tanh(wd), w2_ref[...]))
    softplus = jnp.maximum(z, 0.0) + jnp.log(1.0 + jnp.exp(-jnp.abs(z)))
    lw = -jnp.exp(-softplus - 0.5)
    a = _sigmoid(a0_ref[...] + _dot_x3(ad, a2_ref[...]))
    g = _dot_x3(_sigmoid(gd), g2_ref[...])
    ones_bd = ones_ref[...]
    kk = k * kk_ref[...]
    norm = jnp.sqrt(_head_sum(kk * kk, ones_bd))
    kk = kk / jnp.maximum(norm, L2_EPS)
    k = k * (1.0 + (a - 1.0) * ka_ref[...])
    bonus = _head_sum(r * k * rk_ref[...], ones_bd) * v
    return r, lw, k, v, kk, a, g, bonus


def _rwkv_prep_kernel(p_ref, pp_ref, *refs):
    consts, outs = refs[:10], refs[10:]
    for o_ref, val in zip(outs, _rwkv_token_math(p_ref[...], pp_ref[...], consts)):
        o_ref[...] = val.T


def _rwkv_prep(p, pprev, consts):
    m = p.shape[0]
    full = lambda shape: pl.BlockSpec(shape, lambda i: (0, 0))
    return pl.pallas_call(
        _rwkv_prep_kernel,
        grid=(1,),
        in_specs=[full((m, RWKV_PROJ))] * 2 + [_const_spec(c.shape) for c in consts],
        out_specs=[full((RWKV_WIDTH, m))] * 8,
        out_shape=[jax.ShapeDtypeStruct((RWKV_WIDTH, m), F32)] * 8,
        compiler_params=_params(1),
        name="rwkv_prep",
    )(p, pprev, *consts)


def _rwkv_mix_kernel(p_ref, sh0_ref, mu_ref, w0_ref, w2_ref, a0_ref, a2_ref, g2_ref, kk_ref, ka_ref,
                     rk_ref, ones_ref, lng_ref, lnb_ref, s0_ref, o_ref, sT_ref,
                     s_scr, prev_scr, y_scr, *, nch):
    C = CHUNK
    tb = pl.program_id(1)

    @pl.when(tb == 0)
    def _():
        s_scr[...] = s0_ref[...]
        prev_scr[...] = sh0_ref[...]

    row = lax.broadcasted_iota(jnp.int32, (C, C), 0)
    col = lax.broadcasted_iota(jnp.int32, (C, C), 1)
    tril_incl = (col <= row).astype(BF16)
    PL = 2 * RWKV_HEAD
    t_row = lax.broadcasted_iota(jnp.int32, (C, 2 * PL), 0)
    s_col = lax.broadcasted_iota(jnp.int32, (C, 2 * PL), 1) & (RWKV_HEAD - 1)
    strict = (s_col < t_row)[:, :PL]
    incl = s_col <= t_row
    eye = jnp.where((s_col == t_row)[:, :PL], 1.0, 0.0)
    low_head = lax.broadcasted_iota(jnp.int32, (C, PL), 1) < RWKV_HEAD

    def bd(y):
        return jnp.concatenate([jnp.where(low_head, y, 0.0), jnp.where(low_head, 0.0, y)], axis=0)

    pairs = range(RWKV_HEADS // 2)
    psl = [slice(j * PL, (j + 1) * PL) for j in pairs]
    seqs = range(p_ref.shape[0])
    ch = [(b, c, j) for b in seqs for c in range(nch) for j in pairs]
    consts = (mu_ref, w0_ref, w2_ref, a0_ref, a2_ref, g2_ref, kk_ref, ka_ref, rk_ref, ones_ref)

    glast, AT, RT, VV, L, R, BK, gate, bonus = {}, {}, {}, {}, {}, {}, {}, {}, {}
    for b in seqs:
        p = p_ref[b]
        rows = lax.broadcasted_iota(jnp.int32, p.shape, 0)
        pprev = jnp.where(rows == 0, prev_scr[b], pltpu.roll(p, 1, 0))
        prev_scr[b] = p[p.shape[0] - 1:, :]
        r_all, lw_all, k_all, v_all, kk_all, a_all, gate[b], bonus[b] = _rwkv_token_math(p, pprev, consts)
        kb_all = kk_all * a_all
        for c in range(nch):
            sl = slice(c * C, (c + 1) * C)
            lw = lw_all[sl]
            k = k_all[sl]
            kk = kk_all[sl]
            kb = kb_all[sl]
            cum = _dot_exact_lhs(tril_incl, lw)
            cl = cum[C - 1:C, :]
            eneg = jnp.exp(-cum)
            elast = jnp.exp(cl - cum)
            glast[b, c] = jnp.exp(cl)
            at = -kk * jnp.exp(cum - lw)
            rt = r_all[sl] * jnp.exp(cum)
            bt = kb * eneg
            kt = k * eneg
            bh = kb * elast
            kh = k * elast
            v = v_all[sl]
            for j, ps in enumerate(psl):
                AT[b, c, j], RT[b, c, j], VV[b, c, j] = at[:, ps], rt[:, ps], v[:, ps]
                L[b, c, j] = jnp.concatenate([at[:, ps], rt[:, ps]], axis=0)
                R[b, c, j] = jnp.concatenate([bd(bt[:, ps]), bd(kt[:, ps])], axis=0)
                BK[b, c, j] = jnp.concatenate([bh[:, ps], kh[:, ps]], axis=0)
    P = {i: _dot_nt(L[i], R[i]) for i in ch}
    A = {i: jnp.where(strict, P[i][:C, :PL], 0.0) for i in ch}
    Aak = {i: jnp.where(strict, P[i][:C, PL:], 0.0) for i in ch}
    Ar = {i: jnp.where(incl, P[i][C:, :], 0.0) for i in ch}
    X = {i: _dot(A[i], bd(A[i])) for i in ch}
    Tm = {i: eye + A[i] for i in ch}
    for _ in range(4):
        Z = {i: _dot(X[i], jnp.concatenate([bd(X[i]), bd(Tm[i])], axis=1)) for i in ch}
        Tm = {i: Tm[i] + Z[i][:, PL:] for i in ch}
        X = {i: Z[i][:, :PL] for i in ch}
    AV = {i: _dot(Aak[i], bd(VV[i])) for i in ch}
    M = {i: _dot(Tm[i], jnp.concatenate([bd(AT[i]), bd(AV[i])], axis=1)) for i in ch}
    WU = {i: M[i] + _dot(X[i], jnp.concatenate([bd(M[i][:, :PL]), bd(M[i][:, PL:])], axis=1))
          for i in ch}
    WR = {i: jnp.concatenate([WU[i][:, :PL], RT[i]], axis=0) for i in ch}

    bj = [(b, j) for b in seqs for j in pairs]
    S = {(b, j): s_scr[b, j] for b, j in bj}
    for c in range(nch):
        sl = slice(c * C, (c + 1) * C)
        WRS = {(b, j): _dot_nt(WR[b, c, j], bd(S[b, j])) for b, j in bj}
        U = {(b, j): WRS[b, j][:C] + WU[b, c, j][:, PL:] for b, j in bj}
        UV = {(b, j): jnp.concatenate([U[b, j], VV[b, c, j]], axis=0) for b, j in bj}
        full = {(b, j): _dot_tn(UV[b, j], BK[b, c, j]) for b, j in bj}
        S = {(b, j): S[b, j] * glast[b, c][:, psl[j]]
             + jnp.where(low_head, full[b, j][:C], full[b, j][C:]) for b, j in bj}
        for b, j in bj:
            y_scr[b, sl, psl[j]] = WRS[b, j][C:] + _dot(
                Ar[b, c, j], jnp.concatenate([bd(U[b, j]), bd(VV[b, c, j])], axis=0))
    for b, j in bj:
        s_scr[b, j] = S[b, j]

    ones_bd = ones_ref[...]
    inv_n = 1.0 / RWKV_HEAD
    for b in seqs:
        y = y_scr[b]
        d = y - _head_sum(y, ones_bd) * inv_n
        var = _head_sum(d * d, ones_bd) * inv_n
        o_ref[b] = (d * lax.rsqrt(var + RWKV_GN_EPS) * lng_ref[...] + lnb_ref[...] + bonus[b]) * gate[b]

    @pl.when(tb == pl.num_programs(1) - 1)
    def _():
        sT_ref[...] = s_scr[...]


def _rwkv_mix(p, shift0, consts, s0, tt, nb):
    b, t, _ = p.shape
    st = pl.BlockSpec((nb,) + s0.shape[1:], lambda i, j: (i, 0, 0, 0))
    return pl.pallas_call(
        functools.partial(_rwkv_mix_kernel, nch=tt // CHUNK),
        grid=(b // nb, t // tt),
        in_specs=[pl.BlockSpec((nb, tt, RWKV_PROJ), lambda i, j: (i, j, 0)),
                  pl.BlockSpec((nb, 1, RWKV_PROJ), lambda i, j: (i, 0, 0))]
                 + [_const_spec(c.shape) for c in consts] + [st],
        out_specs=[pl.BlockSpec((nb, tt, RWKV_WIDTH), lambda i, j: (i, j, 0)), st],
        out_shape=[jax.ShapeDtypeStruct((b, t, RWKV_WIDTH), F32), jax.ShapeDtypeStruct(s0.shape, F32)],
        scratch_shapes=[pltpu.VMEM((nb,) + s0.shape[1:], F32),
                        pltpu.VMEM((nb, 1, RWKV_PROJ), F32),
                        pltpu.VMEM((nb, tt, RWKV_WIDTH), F32)],
        compiler_params=_params(2),
        name="rwkv_mix",
    )(p, shift0, *consts, s0)


def _rwkv_step_kernel(r_ref, lw_ref, k_ref, kk_ref, a_ref, v_ref, bonus_ref, g_ref, lng_ref, lnb_ref,
                      s_ref, o_ref, so_ref):
    H = s_ref[0]
    over_v = lambda t: t[:, None, :]
    kk = over_v(kk_ref[...])
    sa = jnp.sum(H * (-kk), axis=0)
    Hn = (H * over_v(jnp.exp(lw_ref[...])) + (kk * over_v(a_ref[...])) * sa[None]
          + over_v(k_ref[...]) * v_ref[...][None])
    so_ref[0] = Hn
    y = jnp.sum(Hn * over_v(r_ref[...]), axis=0)
    d = y - jnp.mean(y, axis=0, keepdims=True)
    var = jnp.mean(d * d, axis=0, keepdims=True)
    o_ref[...] = (d * lax.rsqrt(var + RWKV_GN_EPS) * lng_ref[...] + lnb_ref[...]
                  + bonus_ref[...]) * g_ref[...]


def _rwkv_step(r, lw, k, kk, a, v, bonus, g, lng, lnb, s):
    nb = s.shape[-1]
    vvec = pl.BlockSpec((RWKV_HEAD, nb), lambda h: (h, 0))
    col = pl.BlockSpec((RWKV_HEAD, 1), lambda h: (h, 0))
    st = pl.BlockSpec((1, RWKV_HEAD, RWKV_HEAD, nb), lambda h: (h, 0, 0, 0))
    return pl.pallas_call(
        _rwkv_step_kernel,
        grid=(RWKV_HEADS,),
        in_specs=[vvec] * 8 + [col, col, st],
        out_specs=[vvec, st],
        out_shape=[jax.ShapeDtypeStruct((RWKV_WIDTH, nb), F32), jax.ShapeDtypeStruct(s.shape, F32)],
        compiler_params=_params(1),
        name="rwkv_step",
    )(r, lw, k, kk, a, v, bonus, g, lng, lnb, s)


def _hgrn_chunk_kernel(p_ref, lb_ref, ng_ref, s0_ref, o_ref, sT_ref, s_scr, *, nch):
    C = CHUNK
    SB = HGRN_SUB
    W = HGRN_WIDTH
    tb = pl.program_id(1)

    @pl.when(tb == 0)
    def _():
        s_scr[...] = s0_ref[...]

    tt = nch * C
    row = lax.broadcasted_iota(jnp.int32, (tt, tt), 0)
    col = lax.broadcasted_iota(jnp.int32, (tt, tt), 1)
    tril_bd = ((col <= row) & (col >= (row & -C))).astype(BF16)
    lb = lb_ref[...]
    ng = ng_ref[...]
    heads = range(HGRN_HEADS)
    chunks = range(nch)
    subs = range(C // SB)
    hsl = [slice(h * HGRN_HEAD, (h + 1) * HGRN_HEAD) for h in heads]

    seqs = range(p_ref.shape[0])
    q, iv, qe, sc, kend, glast = {}, {}, {}, {}, {}, {}
    for b in seqs:
        qb = p_ref[b, :, 0:W]
        q[b] = qb * _sigmoid(qb)
        f = lb + (1.0 - lb) * _sigmoid(p_ref[b, :, W:2 * W])
        kx = 1.0 - f
        iv[b] = p_ref[b, :, 2 * W:3 * W]
        bc = _dot_exact_lhs(tril_bd, jnp.log(f))
        qe[b] = q[b] * jnp.exp(bc)
        for c in chunks:
            c0 = c * C
            bl = bc[c0 + C - 1:c0 + C, :]
            glast[b, c] = jnp.exp(bl)
            kend[b, c] = kx[c0:c0 + C] * jnp.exp(bl - bc[c0:c0 + C])
            for i in subs:
                lo, n = c0 + i * SB, c0 + (i + 1) * SB
                if i == 0:
                    eq = jnp.exp(bc[lo:n])
                    ek = jnp.exp(jnp.minimum(-bc[c0:n], EXP_CLAMP))
                else:
                    beta = bc[lo - 1:lo]
                    eq = jnp.exp(bc[lo:n] - beta)
                    ek = jnp.exp(jnp.minimum(beta - bc[c0:n], EXP_CLAMP))
                qi = q[b][lo:n] * eq
                ks = kx[c0:n] * ek
                keep = (lax.broadcasted_iota(jnp.int32, (SB, n - c0), 1)
                        <= lax.broadcasted_iota(jnp.int32, (SB, n - c0), 0) + i * SB)
                for h in heads:
                    sc[b, c, h, i] = jnp.where(keep, _dot_nt(qi[:, hsl[h]], ks[:, hsl[h]]), 0.0)
    bch = [(b, c, h) for b in seqs for c in chunks for h in heads]
    intra = {(b, c, h): jnp.concatenate(
        [_dot(sc[b, c, h, i], iv[b][c * C:c * C + (i + 1) * SB, hsl[h]]) for i in subs], axis=0)
        for b, c, h in bch}
    upd = {(b, c, h): _dot_tn(iv[b][c * C:(c + 1) * C, hsl[h]], kend[b, c][:, hsl[h]]) for b, c, h in bch}
    St = {(b, h): s_scr[b, h] for b in seqs for h in heads}
    for c in chunks:
        sl = slice(c * C, (c + 1) * C)
        for b in seqs:
            for h in heads:
                o = _dot_nt(qe[b][sl, hsl[h]], St[b, h]) + intra[b, c, h]
                o = o * lax.rsqrt(jnp.mean(o * o, axis=-1, keepdims=True) + RMS_EPS)
                o_ref[b, sl, hsl[h]] = o * ng[:, hsl[h]] * _sigmoid(
                    p_ref[b, sl, 3 * W + h * HGRN_HEAD:3 * W + (h + 1) * HGRN_HEAD])
        St = {(b, h): St[b, h] * glast[b, c][:, hsl[h]] + upd[b, c, h] for b in seqs for h in heads}
    for b in seqs:
        for h in heads:
            s_scr[b, h] = St[b, h]

    @pl.when(tb == pl.num_programs(1) - 1)
    def _():
        sT_ref[...] = s_scr[...]


def _hgrn_chunk(p, lb, ng, s0, tt, nb):
    b, t, _ = p.shape
    st = pl.BlockSpec((nb, HGRN_HEADS, HGRN_HEAD, HGRN_HEAD), lambda i, j: (i, 0, 0, 0))
    return pl.pallas_call(
        functools.partial(_hgrn_chunk_kernel, nch=tt // CHUNK),
        grid=(b // nb, t // tt),
        in_specs=[pl.BlockSpec((nb, tt, HGRN_PROJ), lambda i, j: (i, j, 0)),
                  _const_spec((1, HGRN_WIDTH)), _const_spec((1, HGRN_WIDTH)), st],
        out_specs=[pl.BlockSpec((nb, tt, HGRN_WIDTH), lambda i, j: (i, j, 0)), st],
        out_shape=[jax.ShapeDtypeStruct((b, t, HGRN_WIDTH), F32),
                   jax.ShapeDtypeStruct(s0.shape, F32)],
        scratch_shapes=[pltpu.VMEM((nb, HGRN_HEADS, HGRN_HEAD, HGRN_HEAD), F32)],
        compiler_params=_params(2),
        name="hgrn_chunk",
    )(p, lb, ng, s0)


def _hgrn_step_kernel(qc_ref, fc_ref, p_ref, lb_ref, ng_ref, s_ref, o_ref, so_ref):
    W = HGRN_WIDTH
    q = qc_ref[0]
    q = q * _sigmoid(q)
    lb = lb_ref[...]
    f = lb + (1.0 - lb) * _sigmoid(fc_ref[0])
    iv = p_ref[:, 2 * W:3 * W]
    gate = ng_ref[...] * _sigmoid(p_ref[:, 3 * W:4 * W])
    bh = [(b, h) for b in range(s_ref.shape[0]) for h in range(HGRN_HEADS)]
    hsl = [slice(h * HGRN_HEAD, (h + 1) * HGRN_HEAD) for h in range(HGRN_HEADS)]
    shape = (HGRN_HEAD, HGRN_HEAD)
    fcol = {(b, h): jnp.broadcast_to(f[h, :, b:b + 1], shape) for b, h in bh}
    qcol = {(b, h): jnp.broadcast_to(q[h, :, b:b + 1], shape) for b, h in bh}
    Sn = {(b, h): s_ref[b, h] * fcol[b, h] + (1.0 - fcol[b, h]) * iv[b:b + 1, hsl[h]] for b, h in bh}
    for b, h in bh:
        so_ref[b, h] = Sn[b, h]
    o = {(b, h): jnp.sum(Sn[b, h] * qcol[b, h], axis=0, keepdims=True) for b, h in bh}
    ms = {i: jnp.mean(o[i] * o[i], axis=-1, keepdims=True) for i in bh}
    for b, h in bh:
        o_ref[b:b + 1, hsl[h]] = o[b, h] * lax.rsqrt(ms[b, h] + RMS_EPS) * gate[b:b + 1, hsl[h]]


def _hgrn_step(qc, fc, p, lb, ng, s):
    nb = s.shape[0]
    rows = qc.shape[-1]
    col = pl.BlockSpec((1,) + qc.shape[1:], lambda i: (i, 0, 0, 0))
    st = pl.BlockSpec((rows,) + s.shape[1:], lambda i: (i, 0, 0, 0))
    return pl.pallas_call(
        _hgrn_step_kernel,
        grid=(nb // rows,),
        in_specs=[col, col, pl.BlockSpec((rows, HGRN_PROJ), lambda i: (i, 0)),
                  _const_spec(lb.shape), _const_spec(ng.shape), st],
        out_specs=[pl.BlockSpec((rows, HGRN_WIDTH), lambda i: (i, 0)), st],
        out_shape=[jax.ShapeDtypeStruct((nb, HGRN_WIDTH), F32), jax.ShapeDtypeStruct(s.shape, F32)],
        compiler_params=_params(1),
        name="hgrn_step",
    )(qc, fc, p, lb, ng, s)


def _merge_ffn_kernel(x_ref, oa_ref, ob_ref, gm_ref, wc_ref, wa_ref, wb_ref, wo_ref, nf_ref, wg_ref,
                      wu_ref, wd_ref, nfin_ref, out_ref):
    x = x_ref[...]
    pg = jnp.dot(_rms_norm(x, gm_ref[...]).astype(BF16), wc_ref[...], preferred_element_type=F32)
    merged = (_sigmoid(pg[:, :D_MODEL]) * _dot(oa_ref[...], wa_ref[...])
              + _sigmoid(pg[:, D_MODEL:]) * _dot(ob_ref[...], wb_ref[...]))
    x = x + _dot(merged, wo_ref[...])
    h = _rms_norm(x, nf_ref[...]).astype(BF16)
    gate = jnp.dot(h, wg_ref[...], preferred_element_type=F32)
    up = jnp.dot(h, wu_ref[...], preferred_element_type=F32)
    x = x + _dot(gate * _sigmoid(gate) * up, wd_ref[...])
    out_ref[...] = _rms_norm(x, nfin_ref[...])


def _merge_ffn(x, oa, ob, consts, tm):
    m = x.shape[0]
    row = lambda n: pl.BlockSpec((tm, n), lambda i: (i, 0))
    return pl.pallas_call(
        _merge_ffn_kernel,
        grid=(m // tm,),
        in_specs=[row(D_MODEL), row(RWKV_WIDTH), row(HGRN_WIDTH)]
                 + [_const_spec(c.shape) for c in consts],
        out_specs=row(D_MODEL),
        out_shape=jax.ShapeDtypeStruct((m, D_MODEL), F32),
        compiler_params=_params(1),
        name="merge_ffn",
    )(x, oa, ob, *consts)


def kernel(x_prompt, x_sample, state_rwkv_wkv, state_rwkv_shift, state_hgrn, norm_mix_g, w_in, rwkv_mu, rwkv_w0, rwkv_w2, rwkv_a0, rwkv_a2, rwkv_g2, rwkv_k_k, rwkv_k_a, rwkv_r_k, rwkv_ln_g, rwkv_ln_b, w_up_a, hgrn_lb, hgrn_norm_g, w_up_b, w_out, norm_ffn_g, w_ffn_gate, w_ffn_up, w_ffn_down, norm_final_g):
    bp, tp, _ = x_prompt.shape
    bs = x_sample.shape[0]
    mp = bp * tp

    w_in0 = w_in[0]
    wa_in = w_in0[:, :RWKV_PROJ].astype(BF16)
    wb_in = w_in0[:, RWKV_PROJ:RWKV_PROJ + HGRN_PROJ].astype(BF16)
    wc_in = w_in0[:, RWKV_PROJ + HGRN_PROJ:].astype(BF16)
    vec = lambda p: p.reshape(1, -1).astype(F32)
    head_id = jnp.arange(MXU_WIDTH, dtype=jnp.int32) // RWKV_HEAD
    ones_bd = (head_id[:, None] == head_id[None, :]).astype(BF16)
    prep_consts = (vec(rwkv_mu[0]), vec(rwkv_w0[0]), rwkv_w2[0], vec(rwkv_a0[0]), rwkv_a2[0],
                   rwkv_g2[0], vec(rwkv_k_k[0]), vec(rwkv_k_a[0]), vec(rwkv_r_k[0]), ones_bd)
    lb = jnp.cumsum(jax.nn.softmax(hgrn_lb.astype(F32), axis=0), axis=0)[0]
    g_mix = vec(norm_mix_g[0])
    ffn_consts = (g_mix, wc_in, w_up_a[0].astype(BF16), w_up_b[0].astype(BF16), w_out[0].astype(BF16),
                  vec(norm_ffn_g[0]), w_ffn_gate[0].astype(BF16), w_ffn_up[0].astype(BF16),
                  w_ffn_down[0].astype(BF16), vec(norm_final_g))

    xp = x_prompt.reshape(mp, D_MODEL)
    p_rwkv, p_hgrn = _norm_proj(xp, g_mix, wa_in, wb_in, tm=1024)
    p3 = p_rwkv.reshape(bp, tp, RWKV_PROJ)
    o_a, wkv_pp = _rwkv_mix(p3, jnp.zeros((bp, 1, RWKV_PROJ), F32),
                            prep_consts + (vec(rwkv_ln_g[0]), vec(rwkv_ln_b[0])),
                            jnp.zeros((bp, RWKV_HEADS // 2, RWKV_HEAD, 2 * RWKV_HEAD), F32),
                            tt=256, nb=4)
    wkv_p = (wkv_pp.reshape(bp, RWKV_HEADS // 2, RWKV_HEAD, 2, RWKV_HEAD)
             .transpose(0, 1, 3, 2, 4).reshape(bp, RWKV_HEADS, RWKV_HEAD, RWKV_HEAD))
    hg0 = jnp.zeros((bp, HGRN_HEADS, HGRN_HEAD, HGRN_HEAD), F32)
    o_b, hgT_p = _hgrn_chunk(p_hgrn.reshape(bp, tp, HGRN_PROJ), vec(lb), vec(hgrn_norm_g[0]), hg0,
                             tt=256, nb=2)
    y_prompt = _merge_ffn(xp, o_a.reshape(mp, RWKV_WIDTH), o_b.reshape(mp, HGRN_WIDTH),
                          ffn_consts, tm=512).reshape(bp, tp, D_MODEL)
    shift_p = p3[:, -1]
    hgrn_p = jnp.swapaxes(hgT_p, -1, -2)

    xs = x_sample.reshape(bs, D_MODEL)
    s_rwkv, s_hgrn = _norm_proj(xs, g_mix, wa_in, wb_in, tm=bs)
    r, lw, k, v, kk, a, g, bonus = _rwkv_prep(s_rwkv, state_rwkv_shift[0], prep_consts)
    wkv_t = jnp.transpose(state_rwkv_wkv[0], (1, 3, 2, 0))
    oaT, wkv_nt = _rwkv_step(r, lw, k, kk, a, v, bonus, g,
                             rwkv_ln_g[0].reshape(RWKV_WIDTH, 1), rwkv_ln_b[0].reshape(RWKV_WIDTH, 1),
                             wkv_t)
    wkv_s = jnp.transpose(wkv_nt, (3, 0, 2, 1))
    kmajor = lambda t: (t.reshape(bs // STEP_ROWS, STEP_ROWS, HGRN_HEADS, HGRN_HEAD)
                        .transpose(0, 2, 3, 1))
    o_bs, hgrn_s = _hgrn_step(kmajor(s_hgrn[:, :HGRN_WIDTH]), kmajor(s_hgrn[:, HGRN_WIDTH:2 * HGRN_WIDTH]),
                              s_hgrn, lb.reshape(HGRN_HEADS, HGRN_HEAD, 1), vec(hgrn_norm_g[0]),
                              state_hgrn[0])
    y_sample = _merge_ffn(xs, oaT.T, o_bs, ffn_consts, tm=bs).reshape(bs, 1, D_MODEL)

    return (y_prompt, y_sample, wkv_p[None], shift_p[None], hgrn_p[None],
            wkv_s[None], s_rwkv[None], hgrn_s[None])
```

```python
import functools

import jax
import jax.numpy as jnp
from jax import lax
from jax.experimental import pallas as pl
from jax.experimental.pallas import tpu as pltpu

F32 = jnp.float32
BF16 = jnp.bfloat16

D_MODEL = 1024
RWKV_WIDTH = 512
RWKV_HEAD = 64
RWKV_HEADS = 8
RWKV_DECAY_LORA = 64
RWKV_A_LORA = 64
RWKV_GATE_LORA = 128
RWKV_PROJ = 3 * RWKV_WIDTH + RWKV_DECAY_LORA + RWKV_A_LORA + RWKV_GATE_LORA
RWKV_GN_EPS = 64e-5
L2_EPS = 1e-12
HGRN_WIDTH = 512
HGRN_HEADS = 4
HGRN_HEAD = 128
HGRN_PROJ = 4 * HGRN_WIDTH
GATE_PROJ = 2 * D_MODEL
D_FF = 2816
RMS_EPS = 1e-6

MXU_WIDTH = 256
CHUNK = 64
HGRN_SUB = 16
CAST_STEPS = 8
STEP_ROWS = 16
EXP_CLAMP = 80.0
VMEM_LIMIT = 56 * 1024 * 1024


def _dot(a, b):
    return jnp.dot(a.astype(BF16), b.astype(BF16), preferred_element_type=F32)


def _dot_nt(a, b):
    return lax.dot_general(a.astype(BF16), b.astype(BF16), (((1,), (1,)), ((), ())),
                           preferred_element_type=F32)


def _dot_tn(a, b):
    return lax.dot_general(a.astype(BF16), b.astype(BF16), (((0,), (0,)), ((), ())),
                           preferred_element_type=F32)


def _split2(x):
    hi = x.astype(BF16)
    lo = (x - hi.astype(F32)).astype(BF16)
    return hi, lo


def _split3(x):
    hi = x.astype(BF16)
    r1 = x - hi.astype(F32)
    mid = r1.astype(BF16)
    lo = (r1 - mid.astype(F32)).astype(BF16)
    return hi, mid, lo


def _dot_x3(a, b):
    ah, al = _split2(a)
    bh, bl = _split2(b)
    return _dot(ah, bh) + (_dot(ah, bl) + _dot(al, bh))


def _dot_exact_lhs(a_bf16, b):
    bh, bm, bl = _split3(b)
    return _dot(a_bf16, bh) + (_dot(a_bf16, bm) + _dot(a_bf16, bl))


def _sigmoid(x):
    return 1.0 / (1.0 + jnp.exp(-x))


def _rms_norm(x, g):
    return x * lax.rsqrt(jnp.mean(x * x, axis=-1, keepdims=True) + RMS_EPS) * g


def _head_sum(x, ones_bd):
    n = ones_bd.shape[0]
    xb = x.astype(BF16)
    return jnp.concatenate([jnp.dot(xb[:, j:j + n], ones_bd, preferred_element_type=F32)
                            for j in range(0, x.shape[1], n)], axis=1)


def _const_spec(shape):
    nd = len(shape)
    return pl.BlockSpec(shape, lambda *_: (0,) * nd, pipeline_mode=pl.Buffered(1))


def _params(n_grid):
    return pltpu.CompilerParams(dimension_semantics=("arbitrary",) * n_grid,
                                vmem_limit_bytes=VMEM_LIMIT)


def _cast_weights_kernel(*refs):
    n = (len(refs) - 4) // 2
    win_ref, ins = refs[0], refs[1:n + 1]
    oa_ref, ob_ref, oc_ref, outs = refs[n + 1], refs[n + 2], refs[n + 3], refs[n + 4:]
    oa_ref[...] = win_ref[:, :RWKV_PROJ].astype(BF16)
    ob_ref[...] = win_ref[:, RWKV_PROJ:RWKV_PROJ + HGRN_PROJ].astype(BF16)
    oc_ref[...] = win_ref[:, RWKV_PROJ + HGRN_PROJ:].astype(BF16)
    for i_ref, o_ref in zip(ins, outs):
        o_ref[...] = i_ref[...].astype(BF16)


def _cast_weights(w_in, others):
    blk = lambda rows, cols: pl.BlockSpec((rows // CAST_STEPS, cols), lambda i: (i, 0))
    rows = w_in.shape[0]
    widths = (RWKV_PROJ, HGRN_PROJ, GATE_PROJ)
    return pl.pallas_call(
        _cast_weights_kernel,
        grid=(CAST_STEPS,),
        in_specs=[blk(*w_in.shape)] + [blk(*w.shape) for w in others],
        out_specs=[blk(rows, n) for n in widths] + [blk(*w.shape) for w in others],
        out_shape=[jax.ShapeDtypeStruct((rows, n), BF16) for n in widths]
                  + [jax.ShapeDtypeStruct(w.shape, BF16) for w in others],
        compiler_params=_params(1),
        name="cast_weights",
    )(w_in, *others)


def _tail_specs(tm, n_tiles, rows_tail):
    main = lambda n: pl.BlockSpec((tm, n), lambda i: (jnp.minimum(i, n_tiles - 1), 0))
    tail = lambda n: pl.BlockSpec((rows_tail, n), lambda i: (0, 0))
    return main, tail


def _norm_proj_kernel(x_ref, xs_ref, g_ref, wa_ref, wb_ref, oa_ref, ob_ref, osa_ref, osb_ref):
    def project(x_in, oa, ob):
        h = _rms_norm(x_in[...], g_ref[...]).astype(BF16)
        oa[...] = jnp.dot(h, wa_ref[...], preferred_element_type=F32)
        ob[...] = jnp.dot(h, wb_ref[...], preferred_element_type=F32)

    last = pl.num_programs(0) - 1
    pl.when(pl.program_id(0) < last)(lambda: project(x_ref, oa_ref, ob_ref))
    pl.when(pl.program_id(0) == last)(lambda: project(xs_ref, osa_ref, osb_ref))


def _norm_proj(x, xs, g, wa, wb, tm):
    m, ms = x.shape[0], xs.shape[0]
    main, tail = _tail_specs(tm, m // tm, ms)
    return pl.pallas_call(
        _norm_proj_kernel,
        grid=(m // tm + 1,),
        in_specs=[main(D_MODEL), tail(D_MODEL), _const_spec((1, D_MODEL)), _const_spec(wa.shape),
                  _const_spec(wb.shape)],
        out_specs=[main(RWKV_PROJ), main(HGRN_PROJ), tail(RWKV_PROJ), tail(HGRN_PROJ)],
        out_shape=[jax.ShapeDtypeStruct((m, RWKV_PROJ), F32), jax.ShapeDtypeStruct((m, HGRN_PROJ), F32),
                   jax.ShapeDtypeStruct((ms, RWKV_PROJ), F32), jax.ShapeDtypeStruct((ms, HGRN_PROJ), F32)],
        compiler_params=_params(1),
        name="norm_proj",
    )(x, xs, g, wa, wb)


def _rwkv_token_math(p, pprev, consts):
    mu_ref, w0_ref, w2_ref, a0_ref, a2_ref, g2_ref, kk_ref, ka_ref, rk_ref, ones_ref = consts
    W = RWKV_WIDTH
    ps = p + mu_ref[...] * (pprev - p)
    r = ps[:, 0:W]
    k = ps[:, W:2 * W]
    v = ps[:, 2 * W:3 * W]
    o0 = 3 * W
    wd = ps[:, o0:o0 + RWKV_DECAY_LORA]
    ad = ps[:, o0 + RWKV_DECAY_LORA:o0 + RWKV_DECAY_LORA + RWKV_A_LORA]
    gd = ps[:, o0 + RWKV_DECAY_LORA + RWKV_A_LORA:]
    z = -(w0_ref[...] + _dot_x3(jnp.tanh(wd), w2_ref[...]))
    softplus = jnp.maximum(z, 0.0) + jnp.log(1.0 + jnp.exp(-jnp.abs(z)))
    lw = -jnp.exp(-softplus - 0.5)
    a = _sigmoid(a0_ref[...] + _dot_x3(ad, a2_ref[...]))
    g = _dot_x3(_sigmoid(gd), g2_ref[...])
    ones_bd = ones_ref[...]
    kk = k * kk_ref[...]
    norm = jnp.sqrt(_head_sum(kk * kk, ones_bd))
    kk = kk / jnp.maximum(norm, L2_EPS)
    k = k * (1.0 + (a - 1.0) * ka_ref[...])
    bonus = _head_sum(r * k * rk_ref[...], ones_bd) * v
    return r, lw, k, v, kk, a, g, bonus


def _rwkv_prep_kernel(p_ref, pp_ref, *refs):
    consts, outs = refs[:10], refs[10:]
    for o_ref, val in zip(outs, _rwkv_token_math(p_ref[...], pp_ref[...], consts)):
        o_ref[...] = val.T


def _rwkv_prep(p, pprev, consts):
    m = p.shape[0]
    full = lambda shape: pl.BlockSpec(shape, lambda i: (0, 0))
    return pl.pallas_call(
        _rwkv_prep_kernel,
        grid=(1,),
        in_specs=[full((m, RWKV_PROJ))] * 2 + [_const_spec(c.shape) for c in consts],
        out_specs=[full((RWKV_WIDTH, m))] * 8,
        out_shape=[jax.ShapeDtypeStruct((RWKV_WIDTH, m), F32)] * 8,
        compiler_params=_params(1),
        name="rwkv_prep",
    )(p, pprev, *consts)


def _rwkv_mix_kernel(p_ref, sh0_ref, mu_ref, w0_ref, w2_ref, a0_ref, a2_ref, g2_ref, kk_ref, ka_ref,
                     rk_ref, ones_ref, lng_ref, lnb_ref, s0_ref, o_ref, sT_ref,
                     s_scr, prev_scr, y_scr, *, nch):
    C = CHUNK
    tb = pl.program_id(1)

    @pl.when(tb == 0)
    def _():
        s_scr[...] = s0_ref[...]
        prev_scr[...] = sh0_ref[...]

    row = lax.broadcasted_iota(jnp.int32, (C, C), 0)
    col = lax.broadcasted_iota(jnp.int32, (C, C), 1)
    tril_incl = (col <= row).astype(BF16)
    PL = 2 * RWKV_HEAD
    t_row = lax.broadcasted_iota(jnp.int32, (C, 2 * PL), 0)
    s_col = lax.broadcasted_iota(jnp.int32, (C, 2 * PL), 1) & (RWKV_HEAD - 1)
    strict = (s_col < t_row)[:, :PL]
    incl = s_col <= t_row
    eye = jnp.where((s_col == t_row)[:, :PL], 1.0, 0.0)
    low_head = lax.broadcasted_iota(jnp.int32, (C, PL), 1) < RWKV_HEAD

    def bd(y):
        return jnp.concatenate([jnp.where(low_head, y, 0.0), jnp.where(low_head, 0.0, y)], axis=0)

    pairs = range(RWKV_HEADS // 2)
    psl = [slice(j * PL, (j + 1) * PL) for j in pairs]
    seqs = range(p_ref.shape[0])
    ch = [(b, c, j) for b in seqs for c in range(nch) for j in pairs]
    consts = (mu_ref, w0_ref, w2_ref, a0_ref, a2_ref, g2_ref, kk_ref, ka_ref, rk_ref, ones_ref)

    glast, AT, RT, VV, L, R, BK, gate, bonus = {}, {}, {}, {}, {}, {}, {}, {}, {}
    for b in seqs:
        p = p_ref[b]
        rows = lax.broadcasted_iota(jnp.int32, p.shape, 0)
        pprev = jnp.where(rows == 0, prev_scr[b], pltpu.roll(p, 1, 0))
        prev_scr[b] = p[p.shape[0] - 1:, :]
        r_all, lw_all, k_all, v_all, kk_all, a_all, gate[b], bonus[b] = _rwkv_token_math(p, pprev, consts)
        kb_all = kk_all * a_all
        for c in range(nch):
            sl = slice(c * C, (c + 1) * C)
            lw = lw_all[sl]
            k = k_all[sl]
            kk = kk_all[sl]
            kb = kb_all[sl]
            cum = _dot_exact_lhs(tril_incl, lw)
            cl = cum[C - 1:C, :]
            eneg = jnp.exp(-cum)
            elast = jnp.exp(cl - cum)
            glast[b, c] = jnp.exp(cl)
            at = -kk * jnp.exp(cum - lw)
            rt = r_all[sl] * jnp.exp(cum)
            bt = kb * eneg
            kt = k * eneg
            bh = kb * elast
            kh = k * elast
            v = v_all[sl]
            for j, ps in enumerate(psl):
                AT[b, c, j], RT[b, c, j], VV[b, c, j] = at[:, ps], rt[:, ps], v[:, ps]
                L[b, c, j] = jnp.concatenate([at[:, ps], rt[:, ps]], axis=0)
                R[b, c, j] = jnp.concatenate([bd(bt[:, ps]), bd(kt[:, ps])], axis=0)
                BK[b, c, j] = jnp.concatenate([bh[:, ps], kh[:, ps]], axis=0)
    P = {i: _dot_nt(L[i], R[i]) for i in ch}
    A = {i: jnp.where(strict, P[i][:C, :PL], 0.0) for i in ch}
    Aak = {i: jnp.where(strict, P[i][:C, PL:], 0.0) for i in ch}
    Ar = {i: jnp.where(incl, P[i][C:, :], 0.0) for i in ch}
    X = {i: _dot(A[i], bd(A[i])) for i in ch}
    Tm = {i: eye + A[i] for i in ch}
    for _ in range(4):
        Z = {i: _dot(X[i], jnp.concatenate([bd(X[i]), bd(Tm[i])], axis=1)) for i in ch}
        Tm = {i: Tm[i] + Z[i][:, PL:] for i in ch}
        X = {i: Z[i][:, :PL] for i in ch}
    AV = {i: _dot(Aak[i], bd(VV[i])) for i in ch}
    M = {i: _dot(Tm[i], jnp.concatenate([bd(AT[i]), bd(AV[i])], axis=1)) for i in ch}
    WU = {i: M[i] + _dot(X[i], jnp.concatenate([bd(M[i][:, :PL]), bd(M[i][:, PL:])], axis=1))
          for i in ch}
    WR = {i: jnp.concatenate([WU[i][:, :PL], RT[i]], axis=0) for i in ch}

    bj = [(b, j) for b in seqs for j in pairs]
    S = {(b, j): s_scr[b, j] for b, j in bj}
    for c in range(nch):
        sl = slice(c * C, (c + 1) * C)
        WRS = {(b, j): _dot_nt(WR[b, c, j], bd(S[b, j])) for b, j in bj}
        U = {(b, j): WRS[b, j][:C] + WU[b, c, j][:, PL:] for b, j in bj}
        UV = {(b, j): jnp.concatenate([U[b, j], VV[b, c, j]], axis=0) for b, j in bj}
        full = {(b, j): _dot_tn(UV[b, j], BK[b, c, j]) for b, j in bj}
        S = {(b, j): S[b, j] * glast[b, c][:, psl[j]]
             + jnp.where(low_head, full[b, j][:C], full[b, j][C:]) for b, j in bj}
        for b, j in bj:
            y_scr[b, sl, psl[j]] = WRS[b, j][C:] + _dot(
                Ar[b, c, j], jnp.concatenate([bd(U[b, j]), bd(VV[b, c, j])], axis=0))
    for b, j in bj:
        s_scr[b, j] = S[b, j]

    ones_bd = ones_ref[...]
    inv_n = 1.0 / RWKV_HEAD
    for b in seqs:
        y = y_scr[b]
        d = y - _head_sum(y, ones_bd) * inv_n
        var = _head_sum(d * d, ones_bd) * inv_n
        o_ref[b] = (d * lax.rsqrt(var + RWKV_GN_EPS) * lng_ref[...] + lnb_ref[...] + bonus[b]) * gate[b]

    @pl.when(tb == pl.num_programs(1) - 1)
    def _():
        sT_ref[...] = s_scr[...]


def _rwkv_mix(p, shift0, consts, s0, tt, nb):
    b, t, _ = p.shape
    st = pl.BlockSpec((nb,) + s0.shape[1:], lambda i, j: (i, 0, 0, 0))
    return pl.pallas_call(
        functools.partial(_rwkv_mix_kernel, nch=tt // CHUNK),
        grid=(b // nb, t // tt),
        in_specs=[pl.BlockSpec((nb, tt, RWKV_PROJ), lambda i, j: (i, j, 0)),
                  pl.BlockSpec((nb, 1, RWKV_PROJ), lambda i, j: (i, 0, 0))]
                 + [_const_spec(c.shape) for c in consts] + [st],
        out_specs=[pl.BlockSpec((nb, tt, RWKV_WIDTH), lambda i, j: (i, j, 0)), st],
        out_shape=[jax.ShapeDtypeStruct((b, t, RWKV_WIDTH), F32), jax.ShapeDtypeStruct(s0.shape, F32)],
        scratch_shapes=[pltpu.VMEM((nb,) + s0.shape[1:], F32),
                        pltpu.VMEM((nb, 1, RWKV_PROJ), F32),
                        pltpu.VMEM((nb, tt, RWKV_WIDTH), F32)],
        compiler_params=_params(2),
        name="rwkv_mix",
    )(p, shift0, *consts, s0)


def _rwkv_step_kernel(r_ref, lw_ref, k_ref, kk_ref, a_ref, v_ref, bonus_ref, g_ref, lng_ref, lnb_ref,
                      s_ref, o_ref, so_ref):
    H = s_ref[0]
    over_v = lambda t: t[:, None, :]
    kk = over_v(kk_ref[...])
    sa = jnp.sum(H * (-kk), axis=0)
    Hn = (H * over_v(jnp.exp(lw_ref[...])) + (kk * over_v(a_ref[...])) * sa[None]
          + over_v(k_ref[...]) * v_ref[...][None])
    so_ref[0] = Hn
    y = jnp.sum(Hn * over_v(r_ref[...]), axis=0)
    d = y - jnp.mean(y, axis=0, keepdims=True)
    var = jnp.mean(d * d, axis=0, keepdims=True)
    o_ref[...] = (d * lax.rsqrt(var + RWKV_GN_EPS) * lng_ref[...] + lnb_ref[...]
                  + bonus_ref[...]) * g_ref[...]


def _rwkv_step(r, lw, k, kk, a, v, bonus, g, lng, lnb, s):
    nb = s.shape[-1]
    vvec = pl.BlockSpec((RWKV_HEAD, nb), lambda h: (h, 0))
    col = pl.BlockSpec((RWKV_HEAD, 1), lambda h: (h, 0))
    st = pl.BlockSpec((1, RWKV_HEAD, RWKV_HEAD, nb), lambda h: (h, 0, 0, 0))
    return pl.pallas_call(
        _rwkv_step_kernel,
        grid=(RWKV_HEADS,),
        in_specs=[vvec] * 8 + [col, col, st],
        out_specs=[vvec, st],
        out_shape=[jax.ShapeDtypeStruct((RWKV_WIDTH, nb), F32), jax.ShapeDtypeStruct(s.shape, F32)],
        compiler_params=_params(1),
        name="rwkv_step",
    )(r, lw, k, kk, a, v, bonus, g, lng, lnb, s)


def _hgrn_chunk_kernel(p_ref, lb_ref, ng_ref, s0_ref, o_ref, sT_ref, s_scr, *, nch):
    C = CHUNK
    SB = HGRN_SUB
    W = HGRN_WIDTH
    tb = pl.program_id(1)

    @pl.when(tb == 0)
    def _():
        s_scr[...] = s0_ref[...]

    tt = nch * C
    row = lax.broadcasted_iota(jnp.int32, (tt, tt), 0)
    col = lax.broadcasted_iota(jnp.int32, (tt, tt), 1)
    tril_bd = ((col <= row) & (col >= (row & -C))).astype(BF16)
    lb = lb_ref[...]
    ng = ng_ref[...]
    heads = range(HGRN_HEADS)
    chunks = range(nch)
    subs = range(C // SB)
    hsl = [slice(h * HGRN_HEAD, (h + 1) * HGRN_HEAD) for h in heads]

    seqs = range(p_ref.shape[0])
    q, iv, qe, sc, kend, glast = {}, {}, {}, {}, {}, {}
    for b in seqs:
        qb = p_ref[b, :, 0:W]
        q[b] = qb * _sigmoid(qb)
        f = lb + (1.0 - lb) * _sigmoid(p_ref[b, :, W:2 * W])
        kx = 1.0 - f
        iv[b] = p_ref[b, :, 2 * W:3 * W]
        bc = _dot_exact_lhs(tril_bd, jnp.log(f))
        qe[b] = q[b] * jnp.exp(bc)
        for c in chunks:
            c0 = c * C
            bl = bc[c0 + C - 1:c0 + C, :]
            glast[b, c] = jnp.exp(bl)
            kend[b, c] = kx[c0:c0 + C] * jnp.exp(bl - bc[c0:c0 + C])
            for i in subs:
                lo, n = c0 + i * SB, c0 + (i + 1) * SB
                if i == 0:
                    eq = jnp.exp(bc[lo:n])
                    ek = jnp.exp(jnp.minimum(-bc[c0:n], EXP_CLAMP))
                else:
                    beta = bc[lo - 1:lo]
                    eq = jnp.exp(bc[lo:n] - beta)
                    ek = jnp.exp(jnp.minimum(beta - bc[c0:n], EXP_CLAMP))
                qi = q[b][lo:n] * eq
                ks = kx[c0:n] * ek
                keep = (lax.broadcasted_iota(jnp.int32, (SB, n - c0), 1)
                        <= lax.broadcasted_iota(jnp.int32, (SB, n - c0), 0) + i * SB)
                for h in heads:
                    sc[b, c, h, i] = jnp.where(keep, _dot_nt(qi[:, hsl[h]], ks[:, hsl[h]]), 0.0)
    bch = [(b, c, h) for b in seqs for c in chunks for h in heads]
    intra = {(b, c, h): jnp.concatenate(
        [_dot(sc[b, c, h, i], iv[b][c * C:c * C + (i + 1) * SB, hsl[h]]) for i in subs], axis=0)
        for b, c, h in bch}
    upd = {(b, c, h): _dot_tn(iv[b][c * C:(c + 1) * C, hsl[h]], kend[b, c][:, hsl[h]]) for b, c, h in bch}
    St = {(b, h): s_scr[b, h] for b in seqs for h in heads}
    for c in chunks:
        sl = slice(c * C, (c + 1) * C)
        for b in seqs:
            for h in heads:
                o = _dot_nt(qe[b][sl, hsl[h]], St[b, h]) + intra[b, c, h]
                o = o * lax.rsqrt(jnp.mean(o * o, axis=-1, keepdims=True) + RMS_EPS)
                o_ref[b, sl, hsl[h]] = o * ng[:, hsl[h]] * _sigmoid(
                    p_ref[b, sl, 3 * W + h * HGRN_HEAD:3 * W + (h + 1) * HGRN_HEAD])
        St = {(b, h): St[b, h] * glast[b, c][:, hsl[h]] + upd[b, c, h] for b in seqs for h in heads}
    for b in seqs:
        for h in heads:
            s_scr[b, h] = St[b, h]

    @pl.when(tb == pl.num_programs(1) - 1)
    def _():
        sT_ref[...] = s_scr[...]


def _hgrn_chunk(p, lb, ng, s0, tt, nb):
    b, t, _ = p.shape
    st = pl.BlockSpec((nb, HGRN_HEADS, HGRN_HEAD, HGRN_HEAD), lambda i, j: (i, 0, 0, 0))
    return pl.pallas_call(
        functools.partial(_hgrn_chunk_kernel, nch=tt // CHUNK),
        grid=(b // nb, t // tt),
        in_specs=[pl.BlockSpec((nb, tt, HGRN_PROJ), lambda i, j: (i, j, 0)),
                  _const_spec((1, HGRN_WIDTH)), _const_spec((1, HGRN_WIDTH)), st],
        out_specs=[pl.BlockSpec((nb, tt, HGRN_WIDTH), lambda i, j: (i, j, 0)), st],
        out_shape=[jax.ShapeDtypeStruct((b, t, HGRN_WIDTH), F32),
                   jax.ShapeDtypeStruct(s0.shape, F32)],
        scratch_shapes=[pltpu.VMEM((nb, HGRN_HEADS, HGRN_HEAD, HGRN_HEAD), F32)],
        compiler_params=_params(2),
        name="hgrn_chunk",
    )(p, lb, ng, s0)


def _hgrn_step_kernel(qc_ref, fc_ref, p_ref, lb_ref, ng_ref, s_ref, o_ref, so_ref):
    W = HGRN_WIDTH
    q = qc_ref[0]
    q = q * _sigmoid(q)
    lb = lb_ref[...]
    f = lb + (1.0 - lb) * _sigmoid(fc_ref[0])
    iv = p_ref[:, 2 * W:3 * W]
    gate = ng_ref[...] * _sigmoid(p_ref[:, 3 * W:4 * W])
    bh = [(b, h) for b in range(s_ref.shape[0]) for h in range(HGRN_HEADS)]
    hsl = [slice(h * HGRN_HEAD, (h + 1) * HGRN_HEAD) for h in range(HGRN_HEADS)]
    shape = (HGRN_HEAD, HGRN_HEAD)
    fcol = {(b, h): jnp.broadcast_to(f[h, :, b:b + 1], shape) for b, h in bh}
    qcol = {(b, h): jnp.broadcast_to(q[h, :, b:b + 1], shape) for b, h in bh}
    Sn = {(b, h): s_ref[b, h] * fcol[b, h] + (1.0 - fcol[b, h]) * iv[b:b + 1, hsl[h]] for b, h in bh}
    for b, h in bh:
        so_ref[b, h] = Sn[b, h]
    o = {(b, h): jnp.sum(Sn[b, h] * qcol[b, h], axis=0, keepdims=True) for b, h in bh}
    ms = {i: jnp.mean(o[i] * o[i], axis=-1, keepdims=True) for i in bh}
    for b, h in bh:
        o_ref[b:b + 1, hsl[h]] = o[b, h] * lax.rsqrt(ms[b, h] + RMS_EPS) * gate[b:b + 1, hsl[h]]


def _hgrn_step(qc, fc, p, lb, ng, s):
    nb = s.shape[0]
    rows = qc.shape[-1]
    col = pl.BlockSpec((1,) + qc.shape[1:], lambda i: (i, 0, 0, 0))
    st = pl.BlockSpec((rows,) + s.shape[1:], lambda i: (i, 0, 0, 0))
    return pl.pallas_call(
        _hgrn_step_kernel,
        grid=(nb // rows,),
        in_specs=[col, col, pl.BlockSpec((rows, HGRN_PROJ), lambda i: (i, 0)),
                  _const_spec(lb.shape), _const_spec(ng.shape), st],
        out_specs=[pl.BlockSpec((rows, HGRN_WIDTH), lambda i: (i, 0)), st],
        out_shape=[jax.ShapeDtypeStruct((nb, HGRN_WIDTH), F32), jax.ShapeDtypeStruct(s.shape, F32)],
        compiler_params=_params(1),
        name="hgrn_step",
    )(qc, fc, p, lb, ng, s)


def _merge_ffn_kernel(x_ref, oa_ref, ob_ref, xs_ref, oas_ref, obs_ref, gm_ref, wc_ref, wa_ref, wb_ref,
                      wo_ref, nf_ref, wg_ref, wu_ref, wd_ref, nfin_ref, out_ref, outs_ref):
    def block(x_in, oa_in, ob_in, out):
        x = x_in[...]
        pg = jnp.dot(_rms_norm(x, gm_ref[...]).astype(BF16), wc_ref[...], preferred_element_type=F32)
        merged = (_sigmoid(pg[:, :D_MODEL]) * _dot(oa_in[...], wa_ref[...])
                  + _sigmoid(pg[:, D_MODEL:]) * _dot(ob_in[...], wb_ref[...]))
        x = x + _dot(merged, wo_ref[...])
        h = _rms_norm(x, nf_ref[...]).astype(BF16)
        gate = jnp.dot(h, wg_ref[...], preferred_element_type=F32)
        up = jnp.dot(h, wu_ref[...], preferred_element_type=F32)
        x = x + _dot(gate * _sigmoid(gate) * up, wd_ref[...])
        out[...] = _rms_norm(x, nfin_ref[...])

    last = pl.num_programs(0) - 1
    pl.when(pl.program_id(0) < last)(lambda: block(x_ref, oa_ref, ob_ref, out_ref))
    pl.when(pl.program_id(0) == last)(lambda: block(xs_ref, oas_ref, obs_ref, outs_ref))


def _merge_ffn(x, oa, ob, xs, oas, obs, consts, tm):
    m, ms = x.shape[0], xs.shape[0]
    main, tail = _tail_specs(tm, m // tm, ms)
    return pl.pallas_call(
        _merge_ffn_kernel,
        grid=(m // tm + 1,),
        in_specs=[main(D_MODEL), main(RWKV_WIDTH), main(HGRN_WIDTH),
                  tail(D_MODEL), tail(RWKV_WIDTH), tail(HGRN_WIDTH)]
                 + [_const_spec(c.shape) for c in consts],
        out_specs=[main(D_MODEL), tail(D_MODEL)],
        out_shape=[jax.ShapeDtypeStruct((m, D_MODEL), F32), jax.ShapeDtypeStruct((ms, D_MODEL), F32)],
        compiler_params=_params(1),
        name="merge_ffn",
    )(x, oa, ob, xs, oas, obs, *consts)


def kernel(x_prompt, x_sample, state_rwkv_wkv, state_rwkv_shift, state_hgrn, norm_mix_g, w_in, rwkv_mu, rwkv_w0, rwkv_w2, rwkv_a0, rwkv_a2, rwkv_g2, rwkv_k_k, rwkv_k_a, rwkv_r_k, rwkv_ln_g, rwkv_ln_b, w_up_a, hgrn_lb, hgrn_norm_g, w_up_b, w_out, norm_ffn_g, w_ffn_gate, w_ffn_up, w_ffn_down, norm_final_g):
    bp, tp, _ = x_prompt.shape
    bs = x_sample.shape[0]
    mp = bp * tp

    (wa_in, wb_in, wc_in, wg_bf, wu_bf, wd_bf, wua_bf, wub_bf, wo_bf) = _cast_weights(
        w_in[0], (w_ffn_gate[0], w_ffn_up[0], w_ffn_down[0], w_up_a[0], w_up_b[0], w_out[0]))
    vec = lambda p: p.reshape(1, -1).astype(F32)
    head_id = jnp.arange(MXU_WIDTH, dtype=jnp.int32) // RWKV_HEAD
    ones_bd = (head_id[:, None] == head_id[None, :]).astype(BF16)
    prep_consts = (vec(rwkv_mu[0]), vec(rwkv_w0[0]), rwkv_w2[0], vec(rwkv_a0[0]), rwkv_a2[0],
                   rwkv_g2[0], vec(rwkv_k_k[0]), vec(rwkv_k_a[0]), vec(rwkv_r_k[0]), ones_bd)
    lb = jnp.cumsum(jax.nn.softmax(hgrn_lb.astype(F32), axis=0), axis=0)[0]
    g_mix = vec(norm_mix_g[0])
    ffn_consts = (g_mix, wc_in, wua_bf, wub_bf, wo_bf, vec(norm_ffn_g[0]), wg_bf, wu_bf, wd_bf,
                  vec(norm_final_g))

    xp = x_prompt.reshape(mp, D_MODEL)
    xs = x_sample.reshape(bs, D_MODEL)
    p_rwkv, p_hgrn, s_rwkv, s_hgrn = _norm_proj(xp, xs, g_mix, wa_in, wb_in, tm=1024)

    p3 = p_rwkv.reshape(bp, tp, RWKV_PROJ)
    o_a, wkv_pp = _rwkv_mix(p3, jnp.zeros((bp, 1, RWKV_PROJ), F32),
                            prep_consts + (vec(rwkv_ln_g[0]), vec(rwkv_ln_b[0])),
                            jnp.zeros((bp, RWKV_HEADS // 2, RWKV_HEAD, 2 * RWKV_HEAD), F32),
                            tt=256, nb=4)
    wkv_p = (wkv_pp.reshape(bp, RWKV_HEADS // 2, RWKV_HEAD, 2, RWKV_HEAD)
             .transpose(0, 1, 3, 2, 4).reshape(bp, RWKV_HEADS, RWKV_HEAD, RWKV_HEAD))
    hg0 = jnp.zeros((bp, HGRN_HEADS, HGRN_HEAD, HGRN_HEAD), F32)
    o_b, hgT_p = _hgrn_chunk(p_hgrn.reshape(bp, tp, HGRN_PROJ), vec(lb), vec(hgrn_norm_g[0]), hg0,
                             tt=256, nb=2)
    shift_p = p3[:, -1]
    hgrn_p = jnp.swapaxes(hgT_p, -1, -2)

    r, lw, k, v, kk, a, g, bonus = _rwkv_prep(s_rwkv, state_rwkv_shift[0], prep_consts)
    wkv_t = jnp.transpose(state_rwkv_wkv[0], (1, 3, 2, 0))
    oaT, wkv_nt = _rwkv_step(r, lw, k, kk, a, v, bonus, g,
                             rwkv_ln_g[0].reshape(RWKV_WIDTH, 1), rwkv_ln_b[0].reshape(RWKV_WIDTH, 1),
                             wkv_t)
    wkv_s = jnp.transpose(wkv_nt, (3, 0, 2, 1))
    kmajor = lambda t: (t.reshape(bs // STEP_ROWS, STEP_ROWS, HGRN_HEADS, HGRN_HEAD)
                        .transpose(0, 2, 3, 1))
    o_bs, hgrn_s = _hgrn_step(kmajor(s_hgrn[:, :HGRN_WIDTH]), kmajor(s_hgrn[:, HGRN_WIDTH:2 * HGRN_WIDTH]),
                              s_hgrn, lb.reshape(HGRN_HEADS, HGRN_HEAD, 1), vec(hgrn_norm_g[0]),
                              state_hgrn[0])

    y_p, y_s = _merge_ffn(xp, o_a.reshape(mp, RWKV_WIDTH), o_b.reshape(mp, HGRN_WIDTH),
                          xs, oaT.T, o_bs, ffn_consts, tm=512)
    y_prompt = y_p.reshape(bp, tp, D_MODEL)
    y_sample = y_s.reshape(bs, 1, D_MODEL)

    return (y_prompt, y_sample, wkv_p[None], shift_p[None], hgrn_p[None],
            wkv_s[None], s_rwkv[None], hgrn_s[None])
```

```python
import functools

import jax
import jax.numpy as jnp
from jax import lax
from jax.experimental import pallas as pl
from jax.experimental.pallas import tpu as pltpu

F32 = jnp.float32
BF16 = jnp.bfloat16

D_MODEL = 1024
RWKV_WIDTH = 512
RWKV_HEAD = 64
RWKV_HEADS = 8
RWKV_DECAY_LORA = 64
RWKV_A_LORA = 64
RWKV_GATE_LORA = 128
RWKV_PROJ = 3 * RWKV_WIDTH + RWKV_DECAY_LORA + RWKV_A_LORA + RWKV_GATE_LORA
RWKV_GN_EPS = 64e-5
L2_EPS = 1e-12
HGRN_WIDTH = 512
HGRN_HEADS = 4
HGRN_HEAD = 128
HGRN_PROJ = 4 * HGRN_WIDTH
GATE_PROJ = 2 * D_MODEL
D_FF = 2816
RMS_EPS = 1e-6

MXU_WIDTH = 256
CHUNK = 64
HGRN_SUB = 16
CAST_STEPS = 8
STEP_ROWS = 16
EXP_CLAMP = 80.0
VMEM_LIMIT = 56 * 1024 * 1024


def _dot(a, b):
    return jnp.dot(a.astype(BF16), b.astype(BF16), preferred_element_type=F32)


def _dot_nt(a, b):
    return lax.dot_general(a.astype(BF16), b.astype(BF16), (((1,), (1,)), ((), ())),
                           preferred_element_type=F32)


def _dot_tn(a, b):
    return lax.dot_general(a.astype(BF16), b.astype(BF16), (((0,), (0,)), ((), ())),
                           preferred_element_type=F32)


def _split2(x):
    hi = x.astype(BF16)
    lo = (x - hi.astype(F32)).astype(BF16)
    return hi, lo


def _split3(x):
    hi = x.astype(BF16)
    r1 = x - hi.astype(F32)
    mid = r1.astype(BF16)
    lo = (r1 - mid.astype(F32)).astype(BF16)
    return hi, mid, lo


def _dot_x3(a, b):
    ah, al = _split2(a)
    bh, bl = _split2(b)
    return _dot(ah, bh) + (_dot(ah, bl) + _dot(al, bh))


def _dot_exact_lhs(a_bf16, b):
    bh, bm, bl = _split3(b)
    return _dot(a_bf16, bh) + (_dot(a_bf16, bm) + _dot(a_bf16, bl))


def _sigmoid(x):
    return 1.0 / (1.0 + jnp.exp(-x))


def _rms_norm(x, g):
    return x * lax.rsqrt(jnp.mean(x * x, axis=-1, keepdims=True) + RMS_EPS) * g


def _head_sum(x, ones_bd):
    n = ones_bd.shape[0]
    xb = x.astype(BF16)
    return jnp.concatenate([jnp.dot(xb[:, j:j + n], ones_bd, preferred_element_type=F32)
                            for j in range(0, x.shape[1], n)], axis=1)


def _const_spec(shape):
    nd = len(shape)
    return pl.BlockSpec(shape, lambda *_: (0,) * nd, pipeline_mode=pl.Buffered(1))


def _params(n_grid):
    return pltpu.CompilerParams(dimension_semantics=("arbitrary",) * n_grid,
                                vmem_limit_bytes=VMEM_LIMIT)


def _cast_weights_kernel(*refs):
    n = (len(refs) - 4) // 2
    win_ref, ins = refs[0], refs[1:n + 1]
    oa_ref, ob_ref, oc_ref, outs = refs[n + 1], refs[n + 2], refs[n + 3], refs[n + 4:]
    oa_ref[...] = win_ref[:, :RWKV_PROJ].astype(BF16)
    ob_ref[...] = win_ref[:, RWKV_PROJ:RWKV_PROJ + HGRN_PROJ].astype(BF16)
    oc_ref[...] = win_ref[:, RWKV_PROJ + HGRN_PROJ:].astype(BF16)
    for i_ref, o_ref in zip(ins, outs):
        o_ref[...] = i_ref[...].astype(BF16)


def _cast_weights(w_in, others):
    blk = lambda rows, cols: pl.BlockSpec((rows // CAST_STEPS, cols), lambda i: (i, 0))
    rows = w_in.shape[0]
    widths = (RWKV_PROJ, HGRN_PROJ, GATE_PROJ)
    return pl.pallas_call(
        _cast_weights_kernel,
        grid=(CAST_STEPS,),
        in_specs=[blk(*w_in.shape)] + [blk(*w.shape) for w in others],
        out_specs=[blk(rows, n) for n in widths] + [blk(*w.shape) for w in others],
        out_shape=[jax.ShapeDtypeStruct((rows, n), BF16) for n in widths]
                  + [jax.ShapeDtypeStruct(w.shape, BF16) for w in others],
        compiler_params=_params(1),
        name="cast_weights",
    )(w_in, *others)


def _tail_specs(tm, n_tiles, rows_tail):
    main = lambda n: pl.BlockSpec((tm, n), lambda i: (jnp.minimum(i, n_tiles - 1), 0))
    tail = lambda n: pl.BlockSpec((rows_tail, n), lambda i: (0, 0))
    return main, tail


def _norm_proj_kernel(x_ref, xs_ref, g_ref, wa_ref, wb_ref, oa_ref, ob_ref, osa_ref, osb_ref):
    def project(x_in, oa, ob):
        h = _rms_norm(x_in[...], g_ref[...]).astype(BF16)
        oa[...] = jnp.dot(h, wa_ref[...], preferred_element_type=F32)
        ob[...] = jnp.dot(h, wb_ref[...], preferred_element_type=F32)

    last = pl.num_programs(0) - 1
    pl.when(pl.program_id(0) < last)(lambda: project(x_ref, oa_ref, ob_ref))
    pl.when(pl.program_id(0) == last)(lambda: project(xs_ref, osa_ref, osb_ref))


def _norm_proj(x, xs, g, wa, wb, tm):
    m, ms = x.shape[0], xs.shape[0]
    main, tail = _tail_specs(tm, m // tm, ms)
    return pl.pallas_call(
        _norm_proj_kernel,
        grid=(m // tm + 1,),
        in_specs=[main(D_MODEL), tail(D_MODEL), _const_spec((1, D_MODEL)), _const_spec(wa.shape),
                  _const_spec(wb.shape)],
        out_specs=[main(RWKV_PROJ), main(HGRN_PROJ), tail(RWKV_PROJ), tail(HGRN_PROJ)],
        out_shape=[jax.ShapeDtypeStruct((m, RWKV_PROJ), F32), jax.ShapeDtypeStruct((m, HGRN_PROJ), F32),
                   jax.ShapeDtypeStruct((ms, RWKV_PROJ), F32), jax.ShapeDtypeStruct((ms, HGRN_PROJ), F32)],
        compiler_params=_params(1),
        name="norm_proj",
    )(x, xs, g, wa, wb)


def _rwkv_token_math(p, pprev, consts):
    mu_ref, w0_ref, w2_ref, a0_ref, a2_ref, g2_ref, kk_ref, ka_ref, rk_ref, ones_ref = consts
    W = RWKV_WIDTH
    ps = p + mu_ref[...] * (pprev - p)
    r = ps[:, 0:W]
    k = ps[:, W:2 * W]
    v = ps[:, 2 * W:3 * W]
    o0 = 3 * W
    wd = ps[:, o0:o0 + RWKV_DECAY_LORA]
    ad = ps[:, o0 + RWKV_DECAY_LORA:o0 + RWKV_DECAY_LORA + RWKV_A_LORA]
    gd = ps[:, o0 + RWKV_DECAY_LORA + RWKV_A_LORA:]
    z = -(w0_ref[...] + _dot_x3(jnp.tanh(wd), w2_ref[...]))
    softplus = jnp.maximum(z, 0.0) + jnp.log(1.0 + jnp.exp(-jnp.abs(z)))
    lw = -jnp.exp(-softplus - 0.5)
    a = _sigmoid(a0_ref[...] + _dot_x3(ad, a2_ref[...]))
    g = _dot_x3(_sigmoid(gd), g2_ref[...])
    ones_bd = ones_ref[...]
    kk = k * kk_ref[...]
    norm = jnp.sqrt(_head_sum(kk * kk, ones_bd))
    kk = kk / jnp.maximum(norm, L2_EPS)
    k = k * (1.0 + (a - 1.0) * ka_ref[...])
    bonus = _head_sum(r * k * rk_ref[...], ones_bd) * v
    return r, lw, k, v, kk, a, g, bonus


def _rwkv_prep_kernel(p_ref, pp_ref, *refs):
    consts, outs = refs[:10], refs[10:]
    for o_ref, val in zip(outs, _rwkv_token_math(p_ref[...], pp_ref[...], consts)):
        o_ref[...] = val.T


def _rwkv_prep(p, pprev, consts):
    m = p.shape[0]
    full = lambda shape: pl.BlockSpec(shape, lambda i: (0, 0))
    return pl.pallas_call(
        _rwkv_prep_kernel,
        grid=(1,),
        in_specs=[full((m, RWKV_PROJ))] * 2 + [_const_spec(c.shape) for c in consts],
        out_specs=[full((RWKV_WIDTH, m))] * 8,
        out_shape=[jax.ShapeDtypeStruct((RWKV_WIDTH, m), F32)] * 8,
        compiler_params=_params(1),
        name="rwkv_prep",
    )(p, pprev, *consts)


def _rwkv_mix_kernel(p_ref, sh0_ref, mu_ref, w0_ref, w2_ref, a0_ref, a2_ref, g2_ref, kk_ref, ka_ref,
                     rk_ref, ones_ref, lng_ref, lnb_ref, s0_ref, o_ref, sT_ref,
                     s_scr, prev_scr, y_scr, *, nch):
    C = CHUNK
    tb = pl.program_id(1)

    @pl.when(tb == 0)
    def _():
        s_scr[...] = s0_ref[...]
        prev_scr[...] = sh0_ref[...]

    row = lax.broadcasted_iota(jnp.int32, (C, C), 0)
    col = lax.broadcasted_iota(jnp.int32, (C, C), 1)
    tril_incl = (col <= row).astype(BF16)
    PL = 2 * RWKV_HEAD
    t_row = lax.broadcasted_iota(jnp.int32, (C, 2 * PL), 0)
    s_col = lax.broadcasted_iota(jnp.int32, (C, 2 * PL), 1) & (RWKV_HEAD - 1)
    strict = (s_col < t_row)[:, :PL]
    incl = s_col <= t_row
    eye = jnp.where((s_col == t_row)[:, :PL], 1.0, 0.0)
    low_head = lax.broadcasted_iota(jnp.int32, (C, PL), 1) < RWKV_HEAD

    def bd(y):
        return jnp.concatenate([jnp.where(low_head, y, 0.0), jnp.where(low_head, 0.0, y)], axis=0)

    pairs = range(RWKV_HEADS // 2)
    psl = [slice(j * PL, (j + 1) * PL) for j in pairs]
    seqs = range(p_ref.shape[0])
    ch = [(b, c, j) for b in seqs for c in range(nch) for j in pairs]
    consts = (mu_ref, w0_ref, w2_ref, a0_ref, a2_ref, g2_ref, kk_ref, ka_ref, rk_ref, ones_ref)

    glast, AT, RT, VV, L, R, BK, gate, bonus = {}, {}, {}, {}, {}, {}, {}, {}, {}
    for b in seqs:
        p = p_ref[b]
        rows = lax.broadcasted_iota(jnp.int32, p.shape, 0)
        pprev = jnp.where(rows == 0, prev_scr[b], pltpu.roll(p, 1, 0))
        prev_scr[b] = p[p.shape[0] - 1:, :]
        r_all, lw_all, k_all, v_all, kk_all, a_all, gate[b], bonus[b] = _rwkv_token_math(p, pprev, consts)
        kb_all = kk_all * a_all
        for c in range(nch):
            sl = slice(c * C, (c + 1) * C)
            lw = lw_all[sl]
            k = k_all[sl]
            kk = kk_all[sl]
            kb = kb_all[sl]
            cum = _dot_exact_lhs(tril_incl, lw)
            cl = cum[C - 1:C, :]
            eneg = jnp.exp(-cum)
            elast = jnp.exp(cl - cum)
            glast[b, c] = jnp.exp(cl)
            at = -kk * jnp.exp(cum - lw)
            rt = r_all[sl] * jnp.exp(cum)
            bt = kb * eneg
            kt = k * eneg
            bh = kb * elast
            kh = k * elast
            v = v_all[sl]
            for j, ps in enumerate(psl):
                AT[b, c, j], RT[b, c, j], VV[b, c, j] = at[:, ps], rt[:, ps], v[:, ps]
                L[b, c, j] = jnp.concatenate([at[:, ps], rt[:, ps]], axis=0)
                R[b, c, j] = jnp.concatenate([bd(bt[:, ps]), bd(kt[:, ps])], axis=0)
                BK[b, c, j] = jnp.concatenate([bh[:, ps], kh[:, ps]], axis=0)
    P = {i: _dot_nt(L[i], R[i]) for i in ch}
    A = {i: jnp.where(strict, P[i][:C, :PL], 0.0) for i in ch}
    Aak = {i: jnp.where(strict, P[i][:C, PL:], 0.0) for i in ch}
    Ar = {i: jnp.where(incl, P[i][C:, :], 0.0) for i in ch}
    X = {i: _dot(A[i], bd(A[i])) for i in ch}
    Tm = {i: eye + A[i] for i in ch}
    for _ in range(4):
        Z = {i: _dot(X[i], jnp.concatenate([bd(X[i]), bd(Tm[i])], axis=1)) for i in ch}
        Tm = {i: Tm[i] + Z[i][:, PL:] for i in ch}
        X = {i: Z[i][:, :PL] for i in ch}
    AV = {i: _dot(Aak[i], bd(VV[i])) for i in ch}
    M = {i: _dot(Tm[i], jnp.concatenate([bd(AT[i]), bd(AV[i])], axis=1)) for i in ch}
    WU = {i: M[i] + _dot(X[i], jnp.concatenate([bd(M[i][:, :PL]), bd(M[i][:, PL:])], axis=1))
          for i in ch}
    WR = {i: jnp.concatenate([WU[i][:, :PL], RT[i]], axis=0) for i in ch}

    bj = [(b, j) for b in seqs for j in pairs]
    S = {(b, j): s_scr[b, j] for b, j in bj}
    for c in range(nch):
        sl = slice(c * C, (c + 1) * C)
        WRS = {(b, j): _dot_nt(WR[b, c, j], bd(S[b, j])) for b, j in bj}
        U = {(b, j): WRS[b, j][:C] + WU[b, c, j][:, PL:] for b, j in bj}
        UV = {(b, j): jnp.concatenate([U[b, j], VV[b, c, j]], axis=0) for b, j in bj}
        full = {(b, j): _dot_tn(UV[b, j], BK[b, c, j]) for b, j in bj}
        S = {(b, j): S[b, j] * glast[b, c][:, psl[j]]
             + jnp.where(low_head, full[b, j][:C], full[b, j][C:]) for b, j in bj}
        for b, j in bj:
            y_scr[b, sl, psl[j]] = WRS[b, j][C:] + _dot(
                Ar[b, c, j], jnp.concatenate([bd(U[b, j]), bd(VV[b, c, j])], axis=0))
    for b, j in bj:
        s_scr[b, j] = S[b, j]

    ones_bd = ones_ref[...]
    inv_n = 1.0 / RWKV_HEAD
    for b in seqs:
        y = y_scr[b]
        d = y - _head_sum(y, ones_bd) * inv_n
        var = _head_sum(d * d, ones_bd) * inv_n
        o_ref[b] = (d * lax.rsqrt(var + RWKV_GN_EPS) * lng_ref[...] + lnb_ref[...] + bonus[b]) * gate[b]

    @pl.when(tb == pl.num_programs(1) - 1)
    def _():
        sT_ref[...] = s_scr[...]


def _rwkv_mix(p, shift0, consts, s0, tt, nb):
    b, t, _ = p.shape
    st = pl.BlockSpec((nb,) + s0.shape[1:], lambda i, j: (i, 0, 0, 0))
    return pl.pallas_call(
        functools.partial(_rwkv_mix_kernel, nch=tt // CHUNK),
        grid=(b // nb, t // tt),
        in_specs=[pl.BlockSpec((nb, tt, RWKV_PROJ), lambda i, j: (i, j, 0)),
                  pl.BlockSpec((nb, 1, RWKV_PROJ), lambda i, j: (i, 0, 0))]
                 + [_const_spec(c.shape) for c in consts] + [st],
        out_specs=[pl.BlockSpec((nb, tt, RWKV_WIDTH), lambda i, j: (i, j, 0)), st],
        out_shape=[jax.ShapeDtypeStruct((b, t, RWKV_WIDTH), F32), jax.ShapeDtypeStruct(s0.shape, F32)],
        scratch_shapes=[pltpu.VMEM((nb,) + s0.shape[1:], F32),
                        pltpu.VMEM((nb, 1, RWKV_PROJ), F32),
                        pltpu.VMEM((nb, tt, RWKV_WIDTH), F32)],
        compiler_params=_params(2),
        name="rwkv_mix",
    )(p, shift0, *consts, s0)


def _rwkv_step_kernel(r_ref, lw_ref, k_ref, kk_ref, a_ref, v_ref, bonus_ref, g_ref, lng_ref, lnb_ref,
                      s_ref, o_ref, so_ref):
    nb = s_ref.shape[0]
    S = s_ref[...].T.reshape(RWKV_HEAD, RWKV_HEAD, nb)
    over_k = lambda t: t[:, None, :]
    kk = kk_ref[...]
    sa = jnp.sum(S * (-kk)[None], axis=1)
    Sn = (S * jnp.exp(lw_ref[...])[None] + (kk * a_ref[...])[None] * over_k(sa)
          + k_ref[...][None] * over_k(v_ref[...]))
    so_ref[...] = Sn.reshape(RWKV_HEAD * RWKV_HEAD, nb).T
    y = jnp.sum(Sn * r_ref[...][None], axis=1)
    d = y - jnp.mean(y, axis=0, keepdims=True)
    var = jnp.mean(d * d, axis=0, keepdims=True)
    o_ref[...] = (d * lax.rsqrt(var + RWKV_GN_EPS) * lng_ref[...] + lnb_ref[...]
                  + bonus_ref[...]) * g_ref[...]


def _rwkv_step(r, lw, k, kk, a, v, bonus, g, lng, lnb, s):
    nb = s.shape[0]
    vvec = pl.BlockSpec((RWKV_HEAD, nb), lambda h: (h, 0))
    col = pl.BlockSpec((RWKV_HEAD, 1), lambda h: (h, 0))
    st = pl.BlockSpec((nb, RWKV_HEAD * RWKV_HEAD), lambda h: (0, h))
    return pl.pallas_call(
        _rwkv_step_kernel,
        grid=(RWKV_HEADS,),
        in_specs=[vvec] * 8 + [col, col, st],
        out_specs=[vvec, st],
        out_shape=[jax.ShapeDtypeStruct((RWKV_WIDTH, nb), F32), jax.ShapeDtypeStruct(s.shape, F32)],
        compiler_params=_params(1),
        name="rwkv_step",
    )(r, lw, k, kk, a, v, bonus, g, lng, lnb, s)


def _hgrn_chunk_kernel(p_ref, lb_ref, ng_ref, s0_ref, o_ref, sT_ref, s_scr, *, nch):
    C = CHUNK
    SB = HGRN_SUB
    W = HGRN_WIDTH
    tb = pl.program_id(1)

    @pl.when(tb == 0)
    def _():
        s_scr[...] = s0_ref[...]

    tt = nch * C
    row = lax.broadcasted_iota(jnp.int32, (tt, tt), 0)
    col = lax.broadcasted_iota(jnp.int32, (tt, tt), 1)
    tril_bd = ((col <= row) & (col >= (row & -C))).astype(BF16)
    lb = lb_ref[...]
    ng = ng_ref[...]
    heads = range(HGRN_HEADS)
    chunks = range(nch)
    subs = range(C // SB)
    hsl = [slice(h * HGRN_HEAD, (h + 1) * HGRN_HEAD) for h in heads]

    seqs = range(p_ref.shape[0])
    q, iv, qe, sc, kend, glast = {}, {}, {}, {}, {}, {}
    for b in seqs:
        qb = p_ref[b, :, 0:W]
        q[b] = qb * _sigmoid(qb)
        f = lb + (1.0 - lb) * _sigmoid(p_ref[b, :, W:2 * W])
        kx = 1.0 - f
        iv[b] = p_ref[b, :, 2 * W:3 * W]
        bc = _dot_exact_lhs(tril_bd, jnp.log(f))
        qe[b] = q[b] * jnp.exp(bc)
        for c in chunks:
            c0 = c * C
            bl = bc[c0 + C - 1:c0 + C, :]
            glast[b, c] = jnp.exp(bl)
            kend[b, c] = kx[c0:c0 + C] * jnp.exp(bl - bc[c0:c0 + C])
            for i in subs:
                lo, n = c0 + i * SB, c0 + (i + 1) * SB
                if i == 0:
                    eq = jnp.exp(bc[lo:n])
                    ek = jnp.exp(jnp.minimum(-bc[c0:n], EXP_CLAMP))
                else:
                    beta = bc[lo - 1:lo]
                    eq = jnp.exp(bc[lo:n] - beta)
                    ek = jnp.exp(jnp.minimum(beta - bc[c0:n], EXP_CLAMP))
                qi = q[b][lo:n] * eq
                ks = kx[c0:n] * ek
                keep = (lax.broadcasted_iota(jnp.int32, (SB, n - c0), 1)
                        <= lax.broadcasted_iota(jnp.int32, (SB, n - c0), 0) + i * SB)
                for h in heads:
                    sc[b, c, h, i] = jnp.where(keep, _dot_nt(qi[:, hsl[h]], ks[:, hsl[h]]), 0.0)
    bch = [(b, c, h) for b in seqs for c in chunks for h in heads]
    intra = {(b, c, h): jnp.concatenate(
        [_dot(sc[b, c, h, i], iv[b][c * C:c * C + (i + 1) * SB, hsl[h]]) for i in subs], axis=0)
        for b, c, h in bch}
    upd = {(b, c, h): _dot_tn(iv[b][c * C:(c + 1) * C, hsl[h]], kend[b, c][:, hsl[h]]) for b, c, h in bch}
    St = {(b, h): s_scr[b, h] for b in seqs for h in heads}
    for c in chunks:
        sl = slice(c * C, (c + 1) * C)
        for b in seqs:
            for h in heads:
                o = _dot_nt(qe[b][sl, hsl[h]], St[b, h]) + intra[b, c, h]
                o = o * lax.rsqrt(jnp.mean(o * o, axis=-1, keepdims=True) + RMS_EPS)
                o_ref[b, sl, hsl[h]] = o * ng[:, hsl[h]] * _sigmoid(
                    p_ref[b, sl, 3 * W + h * HGRN_HEAD:3 * W + (h + 1) * HGRN_HEAD])
        St = {(b, h): St[b, h] * glast[b, c][:, hsl[h]] + upd[b, c, h] for b in seqs for h in heads}
    for b in seqs:
        for h in heads:
            s_scr[b, h] = St[b, h]

    @pl.when(tb == pl.num_programs(1) - 1)
    def _():
        sT_ref[...] = s_scr[...]


def _hgrn_chunk(p, lb, ng, s0, tt, nb):
    b, t, _ = p.shape
    st = pl.BlockSpec((nb, HGRN_HEADS, HGRN_HEAD, HGRN_HEAD), lambda i, j: (i, 0, 0, 0))
    return pl.pallas_call(
        functools.partial(_hgrn_chunk_kernel, nch=tt // CHUNK),
        grid=(b // nb, t // tt),
        in_specs=[pl.BlockSpec((nb, tt, HGRN_PROJ), lambda i, j: (i, j, 0)),
                  _const_spec((1, HGRN_WIDTH)), _const_spec((1, HGRN_WIDTH)), st],
        out_specs=[pl.BlockSpec((nb, tt, HGRN_WIDTH), lambda i, j: (i, j, 0)), st],
        out_shape=[jax.ShapeDtypeStruct((b, t, HGRN_WIDTH), F32),
                   jax.ShapeDtypeStruct(s0.shape, F32)],
        scratch_shapes=[pltpu.VMEM((nb, HGRN_HEADS, HGRN_HEAD, HGRN_HEAD), F32)],
        compiler_params=_params(2),
        name="hgrn_chunk",
    )(p, lb, ng, s0)


def _hgrn_step_kernel(qc_ref, fc_ref, p_ref, lb_ref, ng_ref, s_ref, o_ref, so_ref):
    W = HGRN_WIDTH
    q = qc_ref[0]
    q = q * _sigmoid(q)
    lb = lb_ref[...]
    f = lb + (1.0 - lb) * _sigmoid(fc_ref[0])
    iv = p_ref[:, 2 * W:3 * W]
    gate = ng_ref[...] * _sigmoid(p_ref[:, 3 * W:4 * W])
    bh = [(b, h) for b in range(s_ref.shape[0]) for h in range(HGRN_HEADS)]
    hsl = [slice(h * HGRN_HEAD, (h + 1) * HGRN_HEAD) for h in range(HGRN_HEADS)]
    shape = (HGRN_HEAD, HGRN_HEAD)
    fcol = {(b, h): jnp.broadcast_to(f[h, :, b:b + 1], shape) for b, h in bh}
    qcol = {(b, h): jnp.broadcast_to(q[h, :, b:b + 1], shape) for b, h in bh}
    Sn = {(b, h): s_ref[b, h] * fcol[b, h] + (1.0 - fcol[b, h]) * iv[b:b + 1, hsl[h]] for b, h in bh}
    for b, h in bh:
        so_ref[b, h] = Sn[b, h]
    o = {(b, h): jnp.sum(Sn[b, h] * qcol[b, h], axis=0, keepdims=True) for b, h in bh}
    ms = {i: jnp.mean(o[i] * o[i], axis=-1, keepdims=True) for i in bh}
    for b, h in bh:
        o_ref[b:b + 1, hsl[h]] = o[b, h] * lax.rsqrt(ms[b, h] + RMS_EPS) * gate[b:b + 1, hsl[h]]


def _hgrn_step(qc, fc, p, lb, ng, s):
    nb = s.shape[0]
    rows = qc.shape[-1]
    col = pl.BlockSpec((1,) + qc.shape[1:], lambda i: (i, 0, 0, 0))
    st = pl.BlockSpec((rows,) + s.shape[1:], lambda i: (i, 0, 0, 0))
    return pl.pallas_call(
        _hgrn_step_kernel,
        grid=(nb // rows,),
        in_specs=[col, col, pl.BlockSpec((rows, HGRN_PROJ), lambda i: (i, 0)),
                  _const_spec(lb.shape), _const_spec(ng.shape), st],
        out_specs=[pl.BlockSpec((rows, HGRN_WIDTH), lambda i: (i, 0)), st],
        out_shape=[jax.ShapeDtypeStruct((nb, HGRN_WIDTH), F32), jax.ShapeDtypeStruct(s.shape, F32)],
        compiler_params=_params(1),
        name="hgrn_step",
    )(qc, fc, p, lb, ng, s)


def _merge_ffn_kernel(x_ref, oa_ref, ob_ref, xs_ref, oas_ref, obs_ref, gm_ref, wc_ref, wa_ref, wb_ref,
                      wo_ref, nf_ref, wg_ref, wu_ref, wd_ref, nfin_ref, out_ref, outs_ref):
    def block(x_in, oa_in, ob_in, out):
        x = x_in[...]
        pg = jnp.dot(_rms_norm(x, gm_ref[...]).astype(BF16), wc_ref[...], preferred_element_type=F32)
        merged = (_sigmoid(pg[:, :D_MODEL]) * _dot(oa_in[...], wa_ref[...])
                  + _sigmoid(pg[:, D_MODEL:]) * _dot(ob_in[...], wb_ref[...]))
        x = x + _dot(merged, wo_ref[...])
        h = _rms_norm(x, nf_ref[...]).astype(BF16)
        gate = jnp.dot(h, wg_ref[...], preferred_element_type=F32)
        up = jnp.dot(h, wu_ref[...], preferred_element_type=F32)
        x = x + _dot(gate * _sigmoid(gate) * up, wd_ref[...])
        out[...] = _rms_norm(x, nfin_ref[...])

    last = pl.num_programs(0) - 1
    pl.when(pl.program_id(0) < last)(lambda: block(x_ref, oa_ref, ob_ref, out_ref))
    pl.when(pl.program_id(0) == last)(lambda: block(xs_ref, oas_ref, obs_ref, outs_ref))


def _merge_ffn(x, oa, ob, xs, oas, obs, consts, tm):
    m, ms = x.shape[0], xs.shape[0]
    main, tail = _tail_specs(tm, m // tm, ms)
    return pl.pallas_call(
        _merge_ffn_kernel,
        grid=(m // tm + 1,),
        in_specs=[main(D_MODEL), main(RWKV_WIDTH), main(HGRN_WIDTH),
                  tail(D_MODEL), tail(RWKV_WIDTH), tail(HGRN_WIDTH)]
                 + [_const_spec(c.shape) for c in consts],
        out_specs=[main(D_MODEL), tail(D_MODEL)],
        out_shape=[jax.ShapeDtypeStruct((m, D_MODEL), F32), jax.ShapeDtypeStruct((ms, D_MODEL), F32)],
        compiler_params=_params(1),
        name="merge_ffn",
    )(x, oa, ob, xs, oas, obs, *consts)


def kernel(x_prompt, x_sample, state_rwkv_wkv, state_rwkv_shift, state_hgrn, norm_mix_g, w_in, rwkv_mu, rwkv_w0, rwkv_w2, rwkv_a0, rwkv_a2, rwkv_g2, rwkv_k_k, rwkv_k_a, rwkv_r_k, rwkv_ln_g, rwkv_ln_b, w_up_a, hgrn_lb, hgrn_norm_g, w_up_b, w_out, norm_ffn_g, w_ffn_gate, w_ffn_up, w_ffn_down, norm_final_g):
    bp, tp, _ = x_prompt.shape
    bs = x_sample.shape[0]
    mp = bp * tp

    (wa_in, wb_in, wc_in, wg_bf, wu_bf, wd_bf, wua_bf, wub_bf, wo_bf) = _cast_weights(
        w_in[0], (w_ffn_gate[0], w_ffn_up[0], w_ffn_down[0], w_up_a[0], w_up_b[0], w_out[0]))
    vec = lambda p: p.reshape(1, -1).astype(F32)
    head_id = jnp.arange(MXU_WIDTH, dtype=jnp.int32) // RWKV_HEAD
    ones_bd = (head_id[:, None] == head_id[None, :]).astype(BF16)
    prep_consts = (vec(rwkv_mu[0]), vec(rwkv_w0[0]), rwkv_w2[0], vec(rwkv_a0[0]), rwkv_a2[0],
                   rwkv_g2[0], vec(rwkv_k_k[0]), vec(rwkv_k_a[0]), vec(rwkv_r_k[0]), ones_bd)
    lb = jnp.cumsum(jax.nn.softmax(hgrn_lb.astype(F32), axis=0), axis=0)[0]
    g_mix = vec(norm_mix_g[0])
    ffn_consts = (g_mix, wc_in, wua_bf, wub_bf, wo_bf, vec(norm_ffn_g[0]), wg_bf, wu_bf, wd_bf,
                  vec(norm_final_g))

    xp = x_prompt.reshape(mp, D_MODEL)
    xs = x_sample.reshape(bs, D_MODEL)
    p_rwkv, p_hgrn, s_rwkv, s_hgrn = _norm_proj(xp, xs, g_mix, wa_in, wb_in, tm=1024)

    p3 = p_rwkv.reshape(bp, tp, RWKV_PROJ)
    o_a, wkv_pp = _rwkv_mix(p3, jnp.zeros((bp, 1, RWKV_PROJ), F32),
                            prep_consts + (vec(rwkv_ln_g[0]), vec(rwkv_ln_b[0])),
                            jnp.zeros((bp, RWKV_HEADS // 2, RWKV_HEAD, 2 * RWKV_HEAD), F32),
                            tt=256, nb=4)
    wkv_p = (wkv_pp.reshape(bp, RWKV_HEADS // 2, RWKV_HEAD, 2, RWKV_HEAD)
             .transpose(0, 1, 3, 2, 4).reshape(bp, RWKV_HEADS, RWKV_HEAD, RWKV_HEAD))
    hg0 = jnp.zeros((bp, HGRN_HEADS, HGRN_HEAD, HGRN_HEAD), F32)
    o_b, hgT_p = _hgrn_chunk(p_hgrn.reshape(bp, tp, HGRN_PROJ), vec(lb), vec(hgrn_norm_g[0]), hg0,
                             tt=256, nb=2)
    shift_p = p3[:, -1]
    hgrn_p = jnp.swapaxes(hgT_p, -1, -2)

    r, lw, k, v, kk, a, g, bonus = _rwkv_prep(s_rwkv, state_rwkv_shift[0], prep_consts)
    oaT, wkv_flat = _rwkv_step(r, lw, k, kk, a, v, bonus, g,
                               rwkv_ln_g[0].reshape(RWKV_WIDTH, 1), rwkv_ln_b[0].reshape(RWKV_WIDTH, 1),
                               state_rwkv_wkv[0].reshape(bs, RWKV_HEADS * RWKV_HEAD * RWKV_HEAD))
    wkv_s = wkv_flat.reshape(bs, RWKV_HEADS, RWKV_HEAD, RWKV_HEAD)
    kmajor = lambda t: (t.reshape(bs // STEP_ROWS, STEP_ROWS, HGRN_HEADS, HGRN_HEAD)
                        .transpose(0, 2, 3, 1))
    o_bs, hgrn_s = _hgrn_step(kmajor(s_hgrn[:, :HGRN_WIDTH]), kmajor(s_hgrn[:, HGRN_WIDTH:2 * HGRN_WIDTH]),
                              s_hgrn, lb.reshape(HGRN_HEADS, HGRN_HEAD, 1), vec(hgrn_norm_g[0]),
                              state_hgrn[0])

    y_p, y_s = _merge_ffn(xp, o_a.reshape(mp, RWKV_WIDTH), o_b.reshape(mp, HGRN_WIDTH),
                          xs, oaT.T, o_bs, ffn_consts, tm=512)
    y_prompt = y_p.reshape(bp, tp, D_MODEL)
    y_sample = y_s.reshape(bs, 1, D_MODEL)

    return (y_prompt, y_sample, wkv_p[None], shift_p[None], hgrn_p[None],
            wkv_s[None], s_rwkv[None], hgrn_s[None])
```

```python
import functools

import jax
import jax.numpy as jnp
from jax import lax
from jax.experimental import pallas as pl
from jax.experimental.pallas import tpu as pltpu

F32 = jnp.float32
BF16 = jnp.bfloat16

D_MODEL = 1024
RWKV_WIDTH = 512
RWKV_HEAD = 64
RWKV_HEADS = 8
RWKV_DECAY_LORA = 64
RWKV_A_LORA = 64
RWKV_GATE_LORA = 128
RWKV_PROJ = 3 * RWKV_WIDTH + RWKV_DECAY_LORA + RWKV_A_LORA + RWKV_GATE_LORA
RWKV_GN_EPS = 64e-5
L2_EPS = 1e-12
HGRN_WIDTH = 512
HGRN_HEADS = 4
HGRN_HEAD = 128
HGRN_PROJ = 4 * HGRN_WIDTH
GATE_PROJ = 2 * D_MODEL
D_FF = 2816
RMS_EPS = 1e-6

MXU_WIDTH = 256
CHUNK = 64
HGRN_SUB = 16
STEP_ROWS = 16
EXP_CLAMP = 80.0
VMEM_LIMIT = 56 * 1024 * 1024


def _dot(a, b):
    return jnp.dot(a.astype(BF16), b.astype(BF16), preferred_element_type=F32)


def _dot_nt(a, b):
    return lax.dot_general(a.astype(BF16), b.astype(BF16), (((1,), (1,)), ((), ())),
                           preferred_element_type=F32)


def _dot_tn(a, b):
    return lax.dot_general(a.astype(BF16), b.astype(BF16), (((0,), (0,)), ((), ())),
                           preferred_element_type=F32)


def _split2(x):
    hi = x.astype(BF16)
    lo = (x - hi.astype(F32)).astype(BF16)
    return hi, lo


def _dot_x3(a, b):
    ah, al = _split2(a)
    bh, bl = _split2(b)
    return _dot(ah, bh) + (_dot(ah, bl) + _dot(al, bh))


def _dot_exact_lhs(a_bf16, b):
    bh, bl = _split2(b)
    return _dot(a_bf16, bh) + _dot(a_bf16, bl)


def _sigmoid(x):
    return 1.0 / (1.0 + jnp.exp(-x))


def _rms_norm(x, g):
    return x * lax.rsqrt(jnp.mean(x * x, axis=-1, keepdims=True) + RMS_EPS) * g


def _head_sum(x, ones_bd):
    n = ones_bd.shape[0]
    xb = x.astype(BF16)
    return jnp.concatenate([jnp.dot(xb[:, j:j + n], ones_bd, preferred_element_type=F32)
                            for j in range(0, x.shape[1], n)], axis=1)


def _const_spec(shape):
    nd = len(shape)
    return pl.BlockSpec(shape, lambda *_: (0,) * nd, pipeline_mode=pl.Buffered(1))


def _params(n_grid):
    return pltpu.CompilerParams(dimension_semantics=("arbitrary",) * n_grid,
                                vmem_limit_bytes=VMEM_LIMIT)


def _tail_specs(tm, n_tiles, rows_tail):
    main = lambda n: pl.BlockSpec((tm, n), lambda i: (jnp.minimum(i, n_tiles - 1), 0))
    tail = lambda n: pl.BlockSpec((rows_tail, n), lambda i: (0, 0))
    return main, tail


def _norm_proj_kernel(x_ref, xs_ref, g_ref, wa_ref, wb_ref, oa_ref, ob_ref, osa_ref, osb_ref):
    def project(x_in, oa, ob):
        h = _rms_norm(x_in[...], g_ref[...]).astype(BF16)
        oa[...] = jnp.dot(h, wa_ref[...], preferred_element_type=F32)
        ob[...] = jnp.dot(h, wb_ref[...], preferred_element_type=F32)

    last = pl.num_programs(0) - 1
    pl.when(pl.program_id(0) < last)(lambda: project(x_ref, oa_ref, ob_ref))
    pl.when(pl.program_id(0) == last)(lambda: project(xs_ref, osa_ref, osb_ref))


def _norm_proj(x, xs, g, wa, wb, tm):
    m, ms = x.shape[0], xs.shape[0]
    main, tail = _tail_specs(tm, m // tm, ms)
    return pl.pallas_call(
        _norm_proj_kernel,
        grid=(m // tm + 1,),
        in_specs=[main(D_MODEL), tail(D_MODEL), _const_spec((1, D_MODEL)), _const_spec(wa.shape),
                  _const_spec(wb.shape)],
        out_specs=[main(RWKV_PROJ), main(HGRN_PROJ), tail(RWKV_PROJ), tail(HGRN_PROJ)],
        out_shape=[jax.ShapeDtypeStruct((m, RWKV_PROJ), F32), jax.ShapeDtypeStruct((m, HGRN_PROJ), F32),
                   jax.ShapeDtypeStruct((ms, RWKV_PROJ), F32), jax.ShapeDtypeStruct((ms, HGRN_PROJ), F32)],
        compiler_params=_params(1),
        name="norm_proj",
    )(x, xs, g, wa, wb)


def _rwkv_token_math(p, pprev, consts):
    mu_ref, w0_ref, w2_ref, a0_ref, a2_ref, g2_ref, kk_ref, ka_ref, rk_ref, ones_ref = consts
    W = RWKV_WIDTH
    ps = p + mu_ref[...] * (pprev - p)
    r = ps[:, 0:W]
    k = ps[:, W:2 * W]
    v = ps[:, 2 * W:3 * W]
    o0 = 3 * W
    wd = ps[:, o0:o0 + RWKV_DECAY_LORA]
    ad = ps[:, o0 + RWKV_DECAY_LORA:o0 + RWKV_DECAY_LORA + RWKV_A_LORA]
    gd = ps[:, o0 + RWKV_DECAY_LORA + RWKV_A_LORA:]
    z = -(w0_ref[...] + _dot_x3(jnp.tanh(wd), w2_ref[...]))
    softplus = jnp.maximum(z, 0.0) + jnp.log(1.0 + jnp.exp(-jnp.abs(z)))
    lw = -jnp.exp(-softplus - 0.5)
    a = _sigmoid(a0_ref[...] + _dot_x3(ad, a2_ref[...]))
    g = _dot_x3(_sigmoid(gd), g2_ref[...])
    ones_bd = ones_ref[...]
    kk = k * kk_ref[...]
    norm = jnp.sqrt(_head_sum(kk * kk, ones_bd))
    kk = kk / jnp.maximum(norm, L2_EPS)
    k = k * (1.0 + (a - 1.0) * ka_ref[...])
    bonus = _head_sum(r * k * rk_ref[...], ones_bd) * v
    return r, lw, k, v, kk, a, g, bonus


def _rwkv_prep_kernel(p_ref, pp_ref, *refs):
    consts, outs = refs[:10], refs[10:]
    for o_ref, val in zip(outs, _rwkv_token_math(p_ref[...], pp_ref[...], consts)):
        o_ref[...] = val.T


def _rwkv_prep(p, pprev, consts):
    m = p.shape[0]
    full = lambda shape: pl.BlockSpec(shape, lambda i: (0, 0))
    return pl.pallas_call(
        _rwkv_prep_kernel,
        grid=(1,),
        in_specs=[full((m, RWKV_PROJ))] * 2 + [_const_spec(c.shape) for c in consts],
        out_specs=[full((RWKV_WIDTH, m))] * 8,
        out_shape=[jax.ShapeDtypeStruct((RWKV_WIDTH, m), F32)] * 8,
        compiler_params=_params(1),
        name="rwkv_prep",
    )(p, pprev, *consts)


def _rwkv_mix_kernel(p_ref, sh0_ref, mu_ref, w0_ref, w2_ref, a0_ref, a2_ref, g2_ref, kk_ref, ka_ref,
                     rk_ref, ones_ref, lng_ref, lnb_ref, s0_ref, o_ref, sT_ref,
                     s_scr, prev_scr, y_scr, *, nch):
    C = CHUNK
    tb = pl.program_id(1)

    @pl.when(tb == 0)
    def _():
        s_scr[...] = s0_ref[...]
        prev_scr[...] = sh0_ref[...]

    row = lax.broadcasted_iota(jnp.int32, (C, C), 0)
    col = lax.broadcasted_iota(jnp.int32, (C, C), 1)
    tril_incl = (col <= row).astype(BF16)
    PL = 2 * RWKV_HEAD
    t_row = lax.broadcasted_iota(jnp.int32, (C, 2 * PL), 0)
    s_col = lax.broadcasted_iota(jnp.int32, (C, 2 * PL), 1) & (RWKV_HEAD - 1)
    strict = (s_col < t_row)[:, :PL]
    incl = s_col <= t_row
    eye = jnp.where((s_col == t_row)[:, :PL], 1.0, 0.0)
    low_head = lax.broadcasted_iota(jnp.int32, (C, PL), 1) < RWKV_HEAD

    def bd(y):
        return jnp.concatenate([jnp.where(low_head, y, 0.0), jnp.where(low_head, 0.0, y)], axis=0)

    pairs = range(RWKV_HEADS // 2)
    psl = [slice(j * PL, (j + 1) * PL) for j in pairs]
    seqs = range(p_ref.shape[0])
    ch = [(b, c, j) for b in seqs for c in range(nch) for j in pairs]
    consts = (mu_ref, w0_ref, w2_ref, a0_ref, a2_ref, g2_ref, kk_ref, ka_ref, rk_ref, ones_ref)

    glast, AT, RT, VV, L, R, BK, gate, bonus = {}, {}, {}, {}, {}, {}, {}, {}, {}
    for b in seqs:
        p = p_ref[b]
        rows = lax.broadcasted_iota(jnp.int32, p.shape, 0)
        pprev = jnp.where(rows == 0, prev_scr[b], pltpu.roll(p, 1, 0))
        prev_scr[b] = p[p.shape[0] - 1:, :]
        r_all, lw_all, k_all, v_all, kk_all, a_all, gate[b], bonus[b] = _rwkv_token_math(p, pprev, consts)
        kb_all = kk_all * a_all
        for c in range(nch):
            sl = slice(c * C, (c + 1) * C)
            lw = lw_all[sl]
            k = k_all[sl]
            kk = kk_all[sl]
            kb = kb_all[sl]
            cum = _dot_exact_lhs(tril_incl, lw)
            cl = cum[C - 1:C, :]
            eneg = jnp.exp(-cum)
            elast = jnp.exp(cl - cum)
            glast[b, c] = jnp.exp(cl)
            at = -kk * jnp.exp(cum - lw)
            rt = r_all[sl] * jnp.exp(cum)
            bt = kb * eneg
            kt = k * eneg
            bh = kb * elast
            kh = k * elast
            v = v_all[sl]
            for j, ps in enumerate(psl):
                AT[b, c, j], RT[b, c, j], VV[b, c, j] = at[:, ps], rt[:, ps], v[:, ps]
                L[b, c, j] = jnp.concatenate([at[:, ps], rt[:, ps]], axis=0)
                R[b, c, j] = jnp.concatenate([bd(bt[:, ps]), bd(kt[:, ps])], axis=0)
                BK[b, c, j] = jnp.concatenate([bh[:, ps], kh[:, ps]], axis=0)
    P = {i: _dot_nt(L[i], R[i]) for i in ch}
    A = {i: jnp.where(strict, P[i][:C, :PL], 0.0) for i in ch}
    Aak = {i: jnp.where(strict, P[i][:C, PL:], 0.0) for i in ch}
    Ar = {i: jnp.where(incl, P[i][C:, :], 0.0) for i in ch}
    X = {i: _dot(A[i], bd(A[i])) for i in ch}
    Tm = {i: eye + A[i] for i in ch}
    for _ in range(4):
        Z = {i: _dot(X[i], jnp.concatenate([bd(X[i]), bd(Tm[i])], axis=1)) for i in ch}
        Tm = {i: Tm[i] + Z[i][:, PL:] for i in ch}
        X = {i: Z[i][:, :PL] for i in ch}
    AV = {i: _dot(Aak[i], bd(VV[i])) for i in ch}
    M = {i: _dot(Tm[i], jnp.concatenate([bd(AT[i]), bd(AV[i])], axis=1)) for i in ch}
    WU = {i: M[i] + _dot(X[i], jnp.concatenate([bd(M[i][:, :PL]), bd(M[i][:, PL:])], axis=1))
          for i in ch}
    WR = {i: jnp.concatenate([WU[i][:, :PL], RT[i]], axis=0) for i in ch}

    bj = [(b, j) for b in seqs for j in pairs]
    S = {(b, j): s_scr[b, j] for b, j in bj}
    for c in range(nch):
        sl = slice(c * C, (c + 1) * C)
        WRS = {(b, j): _dot_nt(WR[b, c, j], bd(S[b, j])) for b, j in bj}
        U = {(b, j): WRS[b, j][:C] + WU[b, c, j][:, PL:] for b, j in bj}
        UV = {(b, j): jnp.concatenate([U[b, j], VV[b, c, j]], axis=0) for b, j in bj}
        full = {(b, j): _dot_tn(UV[b, j], BK[b, c, j]) for b, j in bj}
        S = {(b, j): S[b, j] * glast[b, c][:, psl[j]]
             + jnp.where(low_head, full[b, j][:C], full[b, j][C:]) for b, j in bj}
        for b, j in bj:
            y_scr[b, sl, psl[j]] = WRS[b, j][C:] + _dot(
                Ar[b, c, j], jnp.concatenate([bd(U[b, j]), bd(VV[b, c, j])], axis=0))
    for b, j in bj:
        s_scr[b, j] = S[b, j]

    ones_bd = ones_ref[...]
    inv_n = 1.0 / RWKV_HEAD
    for b in seqs:
        y = y_scr[b]
        d = y - _head_sum(y, ones_bd) * inv_n
        var = _head_sum(d * d, ones_bd) * inv_n
        o_ref[b] = ((d * lax.rsqrt(var + RWKV_GN_EPS) * lng_ref[...] + lnb_ref[...] + bonus[b])
                    * gate[b]).astype(BF16)

    @pl.when(tb == pl.num_programs(1) - 1)
    def _():
        sT_ref[...] = s_scr[...]


def _rwkv_mix(p, shift0, consts, s0, tt, nb):
    b, t, _ = p.shape
    st = pl.BlockSpec((nb,) + s0.shape[1:], lambda i, j: (i, 0, 0, 0))
    return pl.pallas_call(
        functools.partial(_rwkv_mix_kernel, nch=tt // CHUNK),
        grid=(b // nb, t // tt),
        in_specs=[pl.BlockSpec((nb, tt, RWKV_PROJ), lambda i, j: (i, j, 0)),
                  pl.BlockSpec((nb, 1, RWKV_PROJ), lambda i, j: (i, 0, 0))]
                 + [_const_spec(c.shape) for c in consts] + [st],
        out_specs=[pl.BlockSpec((nb, tt, RWKV_WIDTH), lambda i, j: (i, j, 0)), st],
        out_shape=[jax.ShapeDtypeStruct((b, t, RWKV_WIDTH), BF16), jax.ShapeDtypeStruct(s0.shape, F32)],
        scratch_shapes=[pltpu.VMEM((nb,) + s0.shape[1:], F32),
                        pltpu.VMEM((nb, 1, RWKV_PROJ), F32),
                        pltpu.VMEM((nb, tt, RWKV_WIDTH), F32)],
        compiler_params=_params(2),
        name="rwkv_mix",
    )(p, shift0, *consts, s0)


def _rwkv_step_kernel(r_ref, lw_ref, k_ref, kk_ref, a_ref, v_ref, bonus_ref, g_ref, lng_ref, lnb_ref,
                      s_ref, o_ref, so_ref):
    nb = s_ref.shape[0]
    S = s_ref[...].T.reshape(RWKV_HEAD, RWKV_HEAD, nb)
    over_k = lambda t: t[:, None, :]
    kk = kk_ref[...]
    sa = jnp.sum(S * (-kk)[None], axis=1)
    Sn = (S * jnp.exp(lw_ref[...])[None] + (kk * a_ref[...])[None] * over_k(sa)
          + k_ref[...][None] * over_k(v_ref[...]))
    so_ref[...] = Sn.reshape(RWKV_HEAD * RWKV_HEAD, nb).T
    y = jnp.sum(Sn * r_ref[...][None], axis=1)
    d = y - jnp.mean(y, axis=0, keepdims=True)
    var = jnp.mean(d * d, axis=0, keepdims=True)
    o_ref[...] = (d * lax.rsqrt(var + RWKV_GN_EPS) * lng_ref[...] + lnb_ref[...]
                  + bonus_ref[...]) * g_ref[...]


def _rwkv_step(r, lw, k, kk, a, v, bonus, g, lng, lnb, s):
    nb = s.shape[0]
    vvec = pl.BlockSpec((RWKV_HEAD, nb), lambda h: (h, 0))
    col = pl.BlockSpec((RWKV_HEAD, 1), lambda h: (h, 0))
    st = pl.BlockSpec((nb, RWKV_HEAD * RWKV_HEAD), lambda h: (0, h))
    return pl.pallas_call(
        _rwkv_step_kernel,
        grid=(RWKV_HEADS,),
        in_specs=[vvec] * 8 + [col, col, st],
        out_specs=[vvec, st],
        out_shape=[jax.ShapeDtypeStruct((RWKV_WIDTH, nb), F32), jax.ShapeDtypeStruct(s.shape, F32)],
        compiler_params=_params(1),
        name="rwkv_step",
    )(r, lw, k, kk, a, v, bonus, g, lng, lnb, s)


def _hgrn_chunk_kernel(p_ref, lb_ref, ng_ref, s0_ref, o_ref, sT_ref, s_scr, *, nch):
    C = CHUNK
    SB = HGRN_SUB
    W = HGRN_WIDTH
    tb = pl.program_id(1)

    @pl.when(tb == 0)
    def _():
        s_scr[...] = s0_ref[...]

    tt = nch * C
    row = lax.broadcasted_iota(jnp.int32, (tt, tt), 0)
    col = lax.broadcasted_iota(jnp.int32, (tt, tt), 1)
    tril_bd = ((col <= row) & (col >= (row & -C))).astype(BF16)
    lb = lb_ref[...]
    ng = ng_ref[...]
    heads = range(HGRN_HEADS)
    chunks = range(nch)
    subs = range(C // SB)
    hsl = [slice(h * HGRN_HEAD, (h + 1) * HGRN_HEAD) for h in heads]

    seqs = range(p_ref.shape[0])
    q, iv, qe, sc, kend, glast = {}, {}, {}, {}, {}, {}
    for b in seqs:
        qb = p_ref[b, :, 0:W]
        q[b] = qb * _sigmoid(qb)
        f = lb + (1.0 - lb) * _sigmoid(p_ref[b, :, W:2 * W])
        kx = 1.0 - f
        iv[b] = p_ref[b, :, 2 * W:3 * W]
        bc = _dot_exact_lhs(tril_bd, jnp.log(f))
        qe[b] = q[b] * jnp.exp(bc)
        for c in chunks:
            c0 = c * C
            bl = bc[c0 + C - 1:c0 + C, :]
            glast[b, c] = jnp.exp(bl)
            kend[b, c] = kx[c0:c0 + C] * jnp.exp(bl - bc[c0:c0 + C])
            for i in subs:
                lo, n = c0 + i * SB, c0 + (i + 1) * SB
                if i == 0:
                    eq = jnp.exp(bc[lo:n])
                    ek = jnp.exp(jnp.minimum(-bc[c0:n], EXP_CLAMP))
                else:
                    beta = bc[lo - 1:lo]
                    eq = jnp.exp(bc[lo:n] - beta)
                    ek = jnp.exp(jnp.minimum(beta - bc[c0:n], EXP_CLAMP))
                qi = q[b][lo:n] * eq
                ks = kx[c0:n] * ek
                keep = (lax.broadcasted_iota(jnp.int32, (SB, n - c0), 1)
                        <= lax.broadcasted_iota(jnp.int32, (SB, n - c0), 0) + i * SB)
                for h in heads:
                    sc[b, c, h, i] = jnp.where(keep, _dot_nt(qi[:, hsl[h]], ks[:, hsl[h]]), 0.0)
    bch = [(b, c, h) for b in seqs for c in chunks for h in heads]
    intra = {(b, c, h): jnp.concatenate(
        [_dot(sc[b, c, h, i], iv[b][c * C:c * C + (i + 1) * SB, hsl[h]]) for i in subs], axis=0)
        for b, c, h in bch}
    upd = {(b, c, h): _dot_tn(iv[b][c * C:(c + 1) * C, hsl[h]], kend[b, c][:, hsl[h]]) for b, c, h in bch}
    St = {(b, h): s_scr[b, h] for b in seqs for h in heads}
    for c in chunks:
        sl = slice(c * C, (c + 1) * C)
        for b in seqs:
            for h in heads:
                o = _dot_nt(qe[b][sl, hsl[h]], St[b, h]) + intra[b, c, h]
                o = o * lax.rsqrt(jnp.mean(o * o, axis=-1, keepdims=True) + RMS_EPS)
                o_ref[b, sl, hsl[h]] = (o * ng[:, hsl[h]] * _sigmoid(
                    p_ref[b, sl, 3 * W + h * HGRN_HEAD:3 * W + (h + 1) * HGRN_HEAD])).astype(BF16)
        St = {(b, h): St[b, h] * glast[b, c][:, hsl[h]] + upd[b, c, h] for b in seqs for h in heads}
    for b in seqs:
        for h in heads:
            s_scr[b, h] = St[b, h]

    @pl.when(tb == pl.num_programs(1) - 1)
    def _():
        sT_ref[...] = s_scr[...]


def _hgrn_chunk(p, lb, ng, s0, tt, nb):
    b, t, _ = p.shape
    st = pl.BlockSpec((nb, HGRN_HEADS, HGRN_HEAD, HGRN_HEAD), lambda i, j: (i, 0, 0, 0))
    return pl.pallas_call(
        functools.partial(_hgrn_chunk_kernel, nch=tt // CHUNK),
        grid=(b // nb, t // tt),
        in_specs=[pl.BlockSpec((nb, tt, HGRN_PROJ), lambda i, j: (i, j, 0)),
                  _const_spec((1, HGRN_WIDTH)), _const_spec((1, HGRN_WIDTH)), st],
        out_specs=[pl.BlockSpec((nb, tt, HGRN_WIDTH), lambda i, j: (i, j, 0)), st],
        out_shape=[jax.ShapeDtypeStruct((b, t, HGRN_WIDTH), BF16),
                   jax.ShapeDtypeStruct(s0.shape, F32)],
        scratch_shapes=[pltpu.VMEM((nb, HGRN_HEADS, HGRN_HEAD, HGRN_HEAD), F32)],
        compiler_params=_params(2),
        name="hgrn_chunk",
    )(p, lb, ng, s0)


def _hgrn_step_kernel(qc_ref, fc_ref, p_ref, lb_ref, ng_ref, s_ref, o_ref, so_ref):
    W = HGRN_WIDTH
    q = qc_ref[0]
    q = q * _sigmoid(q)
    lb = lb_ref[...]
    f = lb + (1.0 - lb) * _sigmoid(fc_ref[0])
    iv = p_ref[:, 2 * W:3 * W]
    gate = ng_ref[...] * _sigmoid(p_ref[:, 3 * W:4 * W])
    bh = [(b, h) for b in range(s_ref.shape[0]) for h in range(HGRN_HEADS)]
    hsl = [slice(h * HGRN_HEAD, (h + 1) * HGRN_HEAD) for h in range(HGRN_HEADS)]
    shape = (HGRN_HEAD, HGRN_HEAD)
    fcol = {(b, h): jnp.broadcast_to(f[h, :, b:b + 1], shape) for b, h in bh}
    qcol = {(b, h): jnp.broadcast_to(q[h, :, b:b + 1], shape) for b, h in bh}
    Sn = {(b, h): s_ref[b, h] * fcol[b, h] + (1.0 - fcol[b, h]) * iv[b:b + 1, hsl[h]] for b, h in bh}
    for b, h in bh:
        so_ref[b, h] = Sn[b, h]
    o = {(b, h): jnp.sum(Sn[b, h] * qcol[b, h], axis=0, keepdims=True) for b, h in bh}
    ms = {i: jnp.mean(o[i] * o[i], axis=-1, keepdims=True) for i in bh}
    for b, h in bh:
        o_ref[b:b + 1, hsl[h]] = o[b, h] * lax.rsqrt(ms[b, h] + RMS_EPS) * gate[b:b + 1, hsl[h]]


def _hgrn_step(qc, fc, p, lb, ng, s):
    nb = s.shape[0]
    rows = qc.shape[-1]
    col = pl.BlockSpec((1,) + qc.shape[1:], lambda i: (i, 0, 0, 0))
    st = pl.BlockSpec((rows,) + s.shape[1:], lambda i: (i, 0, 0, 0))
    return pl.pallas_call(
        _hgrn_step_kernel,
        grid=(nb // rows,),
        in_specs=[col, col, pl.BlockSpec((rows, HGRN_PROJ), lambda i: (i, 0)),
                  _const_spec(lb.shape), _const_spec(ng.shape), st],
        out_specs=[pl.BlockSpec((rows, HGRN_WIDTH), lambda i: (i, 0)), st],
        out_shape=[jax.ShapeDtypeStruct((nb, HGRN_WIDTH), F32), jax.ShapeDtypeStruct(s.shape, F32)],
        compiler_params=_params(1),
        name="hgrn_step",
    )(qc, fc, p, lb, ng, s)


def _merge_ffn_kernel(x_ref, oa_ref, ob_ref, xs_ref, oas_ref, obs_ref, gm_ref, wc_ref, wa_ref, wb_ref,
                      wo_ref, nf_ref, wg_ref, wu_ref, wd_ref, nfin_ref, out_ref, outs_ref):
    def block(x_in, oa_in, ob_in, out):
        x = x_in[...]
        pg = jnp.dot(_rms_norm(x, gm_ref[...]).astype(BF16), wc_ref[...], preferred_element_type=F32)
        merged = (_sigmoid(pg[:, :D_MODEL]) * _dot(oa_in[...], wa_ref[...])
                  + _sigmoid(pg[:, D_MODEL:]) * _dot(ob_in[...], wb_ref[...]))
        x = x + _dot(merged, wo_ref[...])
        h = _rms_norm(x, nf_ref[...]).astype(BF16)
        gate = jnp.dot(h, wg_ref[...], preferred_element_type=F32)
        up = jnp.dot(h, wu_ref[...], preferred_element_type=F32)
        x = x + _dot(gate * _sigmoid(gate) * up, wd_ref[...])
        out[...] = _rms_norm(x, nfin_ref[...])

    last = pl.num_programs(0) - 1
    pl.when(pl.program_id(0) < last)(lambda: block(x_ref, oa_ref, ob_ref, out_ref))
    pl.when(pl.program_id(0) == last)(lambda: block(xs_ref, oas_ref, obs_ref, outs_ref))


def _merge_ffn(x, oa, ob, xs, oas, obs, consts, tm):
    m, ms = x.shape[0], xs.shape[0]
    main, tail = _tail_specs(tm, m // tm, ms)
    return pl.pallas_call(
        _merge_ffn_kernel,
        grid=(m // tm + 1,),
        in_specs=[main(D_MODEL), main(RWKV_WIDTH), main(HGRN_WIDTH),
                  tail(D_MODEL), tail(RWKV_WIDTH), tail(HGRN_WIDTH)]
                 + [_const_spec(c.shape) for c in consts],
        out_specs=[main(D_MODEL), tail(D_MODEL)],
        out_shape=[jax.ShapeDtypeStruct((m, D_MODEL), F32), jax.ShapeDtypeStruct((ms, D_MODEL), F32)],
        compiler_params=_params(1),
        name="merge_ffn",
    )(x, oa, ob, xs, oas, obs, *consts)


def kernel(x_prompt, x_sample, state_rwkv_wkv, state_rwkv_shift, state_hgrn, norm_mix_g, w_in, rwkv_mu, rwkv_w0, rwkv_w2, rwkv_a0, rwkv_a2, rwkv_g2, rwkv_k_k, rwkv_k_a, rwkv_r_k, rwkv_ln_g, rwkv_ln_b, w_up_a, hgrn_lb, hgrn_norm_g, w_up_b, w_out, norm_ffn_g, w_ffn_gate, w_ffn_up, w_ffn_down, norm_final_g):
    bp, tp, _ = x_prompt.shape
    bs = x_sample.shape[0]
    mp = bp * tp

    w_in0 = w_in[0]
    wa_in = w_in0[:, :RWKV_PROJ].astype(BF16)
    wb_in = w_in0[:, RWKV_PROJ:RWKV_PROJ + HGRN_PROJ].astype(BF16)
    wc_in = w_in0[:, RWKV_PROJ + HGRN_PROJ:].astype(BF16)
    vec = lambda p: p.reshape(1, -1).astype(F32)
    head_id = jnp.arange(MXU_WIDTH, dtype=jnp.int32) // RWKV_HEAD
    ones_bd = (head_id[:, None] == head_id[None, :]).astype(BF16)
    prep_consts = (vec(rwkv_mu[0]), vec(rwkv_w0[0]), rwkv_w2[0], vec(rwkv_a0[0]), rwkv_a2[0],
                   rwkv_g2[0], vec(rwkv_k_k[0]), vec(rwkv_k_a[0]), vec(rwkv_r_k[0]), ones_bd)
    lb = jnp.cumsum(jax.nn.softmax(hgrn_lb.astype(F32), axis=0), axis=0)[0]
    g_mix = vec(norm_mix_g[0])
    ffn_consts = (g_mix, wc_in, w_up_a[0].astype(BF16), w_up_b[0].astype(BF16), w_out[0].astype(BF16),
                  vec(norm_ffn_g[0]), w_ffn_gate[0].astype(BF16), w_ffn_up[0].astype(BF16),
                  w_ffn_down[0].astype(BF16), vec(norm_final_g))

    xp = x_prompt.reshape(mp, D_MODEL)
    xs = x_sample.reshape(bs, D_MODEL)
    p_rwkv, p_hgrn, s_rwkv, s_hgrn = _norm_proj(xp, xs, g_mix, wa_in, wb_in, tm=1024)

    p3 = p_rwkv.reshape(bp, tp, RWKV_PROJ)
    o_a, wkv_pp = _rwkv_mix(p3, jnp.zeros((bp, 1, RWKV_PROJ), F32),
                            prep_consts + (vec(rwkv_ln_g[0]), vec(rwkv_ln_b[0])),
                            jnp.zeros((bp, RWKV_HEADS // 2, RWKV_HEAD, 2 * RWKV_HEAD), F32),
                            tt=256, nb=4)
    wkv_p = (wkv_pp.reshape(bp, RWKV_HEADS // 2, RWKV_HEAD, 2, RWKV_HEAD)
             .transpose(0, 1, 3, 2, 4).reshape(bp, RWKV_HEADS, RWKV_HEAD, RWKV_HEAD))
    hg0 = jnp.zeros((bp, HGRN_HEADS, HGRN_HEAD, HGRN_HEAD), F32)
    o_b, hgT_p = _hgrn_chunk(p_hgrn.reshape(bp, tp, HGRN_PROJ), vec(lb), vec(hgrn_norm_g[0]), hg0,
                             tt=256, nb=4)
    shift_p = p3[:, -1]
    hgrn_p = jnp.swapaxes(hgT_p, -1, -2)

    r, lw, k, v, kk, a, g, bonus = _rwkv_prep(s_rwkv, state_rwkv_shift[0], prep_consts)
    oaT, wkv_flat = _rwkv_step(r, lw, k, kk, a, v, bonus, g,
                               rwkv_ln_g[0].reshape(RWKV_WIDTH, 1), rwkv_ln_b[0].reshape(RWKV_WIDTH, 1),
                               state_rwkv_wkv[0].reshape(bs, RWKV_HEADS * RWKV_HEAD * RWKV_HEAD))
    wkv_s = wkv_flat.reshape(bs, RWKV_HEADS, RWKV_HEAD, RWKV_HEAD)
    kmajor = lambda t: (t.reshape(bs // STEP_ROWS, STEP_ROWS, HGRN_HEADS, HGRN_HEAD)
                        .transpose(0, 2, 3, 1))
    o_bs, hgrn_s = _hgrn_step(kmajor(s_hgrn[:, :HGRN_WIDTH]), kmajor(s_hgrn[:, HGRN_WIDTH:2 * HGRN_WIDTH]),
                              s_hgrn, lb.reshape(HGRN_HEADS, HGRN_HEAD, 1), vec(hgrn_norm_g[0]),
                              state_hgrn[0])

    y_p, y_s = _merge_ffn(xp, o_a.reshape(mp, RWKV_WIDTH), o_b.reshape(mp, HGRN_WIDTH),
                          xs, oaT.T, o_bs, ffn_consts, tm=512)
    y_prompt = y_p.reshape(bp, tp, D_MODEL)
    y_sample = y_s.reshape(bs, 1, D_MODEL)

    return (y_prompt, y_sample, wkv_p[None], shift_p[None], hgrn_p[None],
            wkv_s[None], s_rwkv[None], hgrn_s[None])
```
